```python
import math
import jax
import jax.numpy as jnp
from jax import lax
import numpy as np

D_MODEL = 2048
BATCH = 16
SEQ = 2048
DEPTH = 4

GRID_W = 64
CTX_LEN = 256
Q_BLOCK = 128
ROPE_THETA = 10000.0
NORM_EPS = 1e-6

A_HEAD_DIM = 128
A_HEADS = (D_MODEL // 2) // A_HEAD_DIM
A_KV_HEADS = A_HEADS // 4
A_GROUP = A_HEADS // A_KV_HEADS
A_Q_WIDTH = A_HEADS * A_HEAD_DIM
A_KV_WIDTH = A_KV_HEADS * A_HEAD_DIM

B_WIDTH = D_MODEL // 2
B_BLOCK = 128
B_HEADS = B_WIDTH // B_BLOCK
B_CONV = 4
LRU_C = 8.0

HYB_IN = A_Q_WIDTH + 2 * A_KV_WIDTH + 2 * B_WIDTH
HYB_OUT = A_Q_WIDTH + B_WIDTH

C_HEAD_DIM = 64
C_HEADS = D_MODEL // (2 * C_HEAD_DIM)
C_QK_WIDTH = C_HEADS * 2 * C_HEAD_DIM
C_V_WIDTH = C_HEADS * 2 * C_HEAD_DIM
C_IN = 2 * C_QK_WIDTH + C_V_WIDTH
C_OUT = C_V_WIDTH

N_EXPERTS = 32
TOP_K = 4
D_EXPERT = D_MODEL // 4
SWIGLU_LIMIT = 7.0
SWIGLU_ALPHA = 1.702

N_EVEN = (DEPTH + 1) // 2
N_ODD = DEPTH // 2

kernel_name = 'hybrid_flow_gqa_rglru_diffattn_moe'


def rms_norm(x, g):
    xf = x.astype(jnp.float32)
    y = xf * lax.rsqrt(jnp.mean(xf * xf, axis=-1, keepdims=True) + NORM_EPS)
    return (y * g.astype(jnp.float32)).astype(x.dtype)


def modulate(x, g, shift, scale):
    return rms_norm(x, g) * (1 + scale) + shift


def axial_rope_tables(rows, head_dim):
    n_freq = head_dim // 4
    inv_freq = ROPE_THETA ** (-jnp.arange(n_freq, dtype=jnp.float32) / n_freq)
    row = jnp.repeat(jnp.arange(rows, dtype=jnp.float32), GRID_W)
    col = jnp.tile(jnp.arange(GRID_W, dtype=jnp.float32), rows)
    ang_r = row[:, None] * inv_freq
    ang_c = col[:, None] * inv_freq
    return (jnp.cos(ang_r), jnp.sin(ang_r), jnp.cos(ang_c), jnp.sin(ang_c))


def rope_1d(x, cos, sin):
    shape = (cos.shape[0],) + (1,) * (x.ndim - 3) + (cos.shape[1],)
    cos = cos.reshape(shape)
    sin = sin.reshape(shape)
    x1, x2 = jnp.split(x.astype(jnp.float32), 2, axis=-1)
    return jnp.concatenate([x1 * cos - x2 * sin, x2 * cos + x1 * sin], axis=-1)


def apply_axial_rope(x, tables):
    cos_r, sin_r, cos_c, sin_c = tables
    x_row, x_col = jnp.split(x, 2, axis=-1)
    out = jnp.concatenate([rope_1d(x_row, cos_r, sin_r), rope_1d(x_col, cos_c, sin_c)], axis=-1)
    return out.astype(x.dtype)


def to_query_blocks(q):
    b, n = q.shape[:2]
    return jnp.moveaxis(q.reshape((b, n // Q_BLOCK, Q_BLOCK) + q.shape[2:]), 1, 0)


def from_query_blocks(o):
    o = jnp.moveaxis(o, 0, 1)
    return o.reshape((o.shape[0], o.shape[1] * o.shape[2]) + o.shape[3:])


def gqa_attend(q, k, v):
    scale = q.shape[-1] ** -0.5

    def block(qb):
        s = jnp.einsum('bqkgd,bskd->bkgqs', qb, k, preferred_element_type=jnp.float32) * scale
        p = jax.nn.softmax(s, axis=-1)
        return jnp.einsum('bkgqs,bskd->bqkgd', p.astype(v.dtype), v)

    return from_query_blocks(lax.map(block, to_query_blocks(q)))


def diff_attend(q, k, v, lam):
    scale = q.shape[-1] ** -0.5

    def block(qb):
        s = jnp.einsum('bqhmd,bshmd->bhmqs', qb, k, preferred_element_type=jnp.float32) * scale
        p = jax.nn.softmax(s, axis=-1)
        w = p[:, :, 0] - lam * p[:, :, 1]
        return jnp.einsum('bhqs,bshe->bqhe', w.astype(v.dtype), v)

    return from_query_blocks(lax.map(block, to_query_blocks(q)))


def centred_dwconv(x, w, b):
    k_w, ch = w.shape
    y = lax.conv_general_dilated(
        x, w[:, None, :], window_strides=(1,), padding=[(k_w // 2, k_w - 1 - k_w // 2)],
        dimension_numbers=('NWC', 'WIO', 'NWC'), feature_group_count=ch)
    return y + b


def rglru_coeffs(xc, wa, ba, wx, bx, lam):
    b, n, ch = xc.shape
    xh = xc.reshape(b, n, B_HEADS, B_BLOCK)
    r = jax.nn.sigmoid((jnp.einsum('bnhi,hij->bnhj', xh, wa).reshape(b, n, ch) + ba).astype(jnp.float32))
    i = jax.nn.sigmoid((jnp.einsum('bnhi,hij->bnhj', xh, wx).reshape(b, n, ch) + bx).astype(jnp.float32))
    log_a = -LRU_C * r * jax.nn.softplus(-lam.astype(jnp.float32))
    a = jnp.exp(log_a)
    u = jnp.sqrt(-jnp.expm1(2.0 * log_a)) * i * xc.astype(jnp.float32)
    return a, u


def linear_scan(a, u, h0, reverse):
    def combine(e1, e2):
        a1, u1 = e1
        a2, u2 = e2
        return a1 * a2, a2 * u1 + u2

    a_cum, h = lax.associative_scan(combine, (a, u), reverse=reverse, axis=1)
    if h0 is not None:
        h = h + a_cum * h0[:, None, :]
    return h


def hybrid_mixer(h_lat, h_ctx, rope, w_in, w_out, q_g, k_g, conv_w, conv_b,
                 ga_w, ga_b, gx_w, gx_b, lru_lam, with_ctx):
    split_at = [A_Q_WIDTH, A_Q_WIDTH + A_KV_WIDTH, A_Q_WIDTH + 2 * A_KV_WIDTH,
                A_Q_WIDTH + 2 * A_KV_WIDTH + B_WIDTH]

    def project(h):
        b, n = h.shape[:2]
        q, k, v, xb, yb = jnp.split(h @ w_in, split_at, axis=-1)
        q = rms_norm(q.reshape(b, n, A_KV_HEADS, A_GROUP, A_HEAD_DIM), q_g)
        k = rms_norm(k.reshape(b, n, A_KV_HEADS, A_HEAD_DIM), k_g)
        v = v.reshape(b, n, A_KV_HEADS, A_HEAD_DIM)
        return q, k, v, centred_dwconv(xb, conv_w, conv_b), jax.nn.gelu(yb)

    q_l, k_l, v_l, xc_l, y_l = project(h_lat)
    q_c, k_c, v_c, xc_c, y_c = project(h_ctx)
    q_l = apply_axial_rope(q_l, rope)
    k_l = apply_axial_rope(k_l, rope)
    b, n = h_lat.shape[:2]
    att_l = gqa_attend(q_l, jnp.concatenate([k_l, k_c], axis=1),
                       jnp.concatenate([v_l, v_c], axis=1)).reshape(b, n, A_Q_WIDTH)

    rec_l = []
    rec_c = []
    for d, reverse in enumerate((False, True)):
        a_c, u_c = rglru_coeffs(xc_c, ga_w[d], ga_b[d], gx_w[d], gx_b[d], lru_lam[d])
        hs_c = linear_scan(a_c, u_c, None, reverse)
        h_final = hs_c[:, 0] if reverse else hs_c[:, -1]
        a_l, u_l = rglru_coeffs(xc_l, ga_w[d], ga_b[d], gx_w[d], gx_b[d], lru_lam[d])
        rec_l.append(linear_scan(a_l, u_l, h_final, reverse))
        rec_c.append(hs_c)
    out_l = jnp.concatenate([att_l, (rec_l[0] + rec_l[1]).astype(y_l.dtype) * y_l], axis=-1) @ w_out
    if not with_ctx:
        return out_l, None
    bc, nc = h_ctx.shape[:2]
    att_c = gqa_attend(q_c, k_c, v_c).reshape(bc, nc, A_Q_WIDTH)
    out_c = jnp.concatenate([att_c, (rec_c[0] + rec_c[1]).astype(y_c.dtype) * y_c], axis=-1) @ w_out
    return out_l, out_c


def diff_mixer(h_lat, h_ctx, rope, w_in, w_out, lq1, lk1, lq2, lk2, subln_g, lam_init, with_ctx):
    def project(h):
        b, n = h.shape[:2]
        q, k, v = jnp.split(h @ w_in, [C_QK_WIDTH, 2 * C_QK_WIDTH], axis=-1)
        return (q.reshape(b, n, C_HEADS, 2, C_HEAD_DIM), k.reshape(b, n, C_HEADS, 2, C_HEAD_DIM),
                v.reshape(b, n, C_HEADS, 2 * C_HEAD_DIM))

    f32 = jnp.float32
    lam = (jnp.exp(jnp.sum(lq1.astype(f32) * lk1.astype(f32)))
           - jnp.exp(jnp.sum(lq2.astype(f32) * lk2.astype(f32))) + lam_init)

    def finish(o):
        b, n = o.shape[:2]
        return (rms_norm(o, subln_g) * (1 - lam_init)).reshape(b, n, C_OUT) @ w_out

    q_l, k_l, v_l = project(h_lat)
    q_c, k_c, v_c = project(h_ctx)
    q_l = apply_axial_rope(q_l, rope)
    k_l = apply_axial_rope(k_l, rope)
    out_l = finish(diff_attend(q_l, jnp.concatenate([k_l, k_c], axis=1),
                               jnp.concatenate([v_l, v_c], axis=1), lam))
    if not with_ctx:
        return out_l, None
    return out_l, finish(diff_attend(q_c, k_c, v_c, lam))


def moe_ffn(h, router_w, router_b, w_gu, b_gu, w_d, b_d):
    logits = (h @ router_w).astype(jnp.float32) + router_b.astype(jnp.float32)
    top_val, top_idx = lax.top_k(logits, TOP_K)
    top_w = jax.nn.softmax(top_val, axis=-1)
    gates = jnp.einsum('tk,tke->te', top_w, jax.nn.one_hot(top_idx, N_EXPERTS, dtype=jnp.float32))
    y = jnp.zeros(h.shape, jnp.float32)
    for e in range(N_EXPERTS):
        gu = h @ w_gu[e] + b_gu[e]
        gate = jnp.minimum(gu[:, :D_EXPERT], SWIGLU_LIMIT)
        up = jnp.clip(gu[:, D_EXPERT:], -SWIGLU_LIMIT, SWIGLU_LIMIT)
        act = (up + 1) * gate * jax.nn.sigmoid(gate * SWIGLU_ALPHA)
        y = y + gates[:, e:e + 1] * (act @ w_d[e] + b_d[e]).astype(jnp.float32)
    return y.astype(h.dtype)


def setup_inputs(seed: int = 0) -> dict:
    key = jax.random.key(seed)
    keys = iter(jax.random.split(key, 40))
    f32 = jnp.float32
    D = D_MODEL
    E = N_EXPERTS

    def normal(shape, std):
        return std * jax.random.normal(next(keys), shape, f32)

    def gain(shape):
        return 1.0 + normal(shape, 0.02)

    u = jax.random.uniform(next(keys), (N_EVEN, 2, B_WIDTH), f32, 0.9, 0.999)
    a_base = u ** (1.0 / LRU_C)
    lru_lambda = jnp.log(a_base) - jnp.log1p(-a_base)
    return {
        'x': normal((BATCH, SEQ, D), 1.0),
        'c': normal((BATCH, D), 1.0),
        'ctx': normal((BATCH, CTX_LEN, D), 1.0),
        'c_ctx': normal((D,), 1.0),
        'mod_w': normal((DEPTH, D, 6 * D), 0.5 * D ** -0.5),
        'mod_b': normal((DEPTH, 6 * D), 0.01),
        'norm1_g': gain((DEPTH, D)),
        'norm2_g': gain((DEPTH, D)),
        'hyb_w_in': normal((N_EVEN, D, HYB_IN), D ** -0.5),
        'hyb_w_out': normal((N_EVEN, HYB_OUT, D), HYB_OUT ** -0.5),
        'a_q_norm_g': gain((N_EVEN, A_HEAD_DIM)),
        'a_k_norm_g': gain((N_EVEN, A_HEAD_DIM)),
        'b_conv_w': normal((N_EVEN, B_CONV, B_WIDTH), B_CONV ** -0.5),
        'b_conv_b': normal((N_EVEN, B_WIDTH), 0.01),
        'b_gate_a_w': normal((N_EVEN, 2, B_HEADS, B_BLOCK, B_BLOCK), B_BLOCK ** -0.5),
        'b_gate_a_b': normal((N_EVEN, 2, B_WIDTH), 0.01),
        'b_gate_x_w': normal((N_EVEN, 2, B_HEADS, B_BLOCK, B_BLOCK), B_BLOCK ** -0.5),
        'b_gate_x_b': normal((N_EVEN, 2, B_WIDTH), 0.01),
        'b_lru_lambda': lru_lambda,
        'c_w_in': normal((N_ODD, D, C_IN), D ** -0.5),
        'c_w_out': normal((N_ODD, C_OUT, D), C_OUT ** -0.5),
        'c_lambda_q1': normal((N_ODD, C_HEAD_DIM), 0.1),
        'c_lambda_k1': normal((N_ODD, C_HEAD_DIM), 0.1),
        'c_lambda_q2': normal((N_ODD, C_HEAD_DIM), 0.1),
        'c_lambda_k2': normal((N_ODD, C_HEAD_DIM), 0.1),
        'c_subln_g': gain((N_ODD, 2 * C_HEAD_DIM)),
        'router_w': normal((DEPTH, D, E), D ** -0.5),
        'router_b': normal((DEPTH, E), 0.01),
        'exp_w_gu': normal((DEPTH, E, D, 2 * D_EXPERT), D ** -0.5),
        'exp_b_gu': normal((DEPTH, E, 2 * D_EXPERT), 0.01),
        'exp_w_d': normal((DEPTH, E, D_EXPERT, D), D_EXPERT ** -0.5),
        'exp_b_d': normal((DEPTH, E, D), 0.01),
        'final_g': gain((D,)),
    }


def reference(x, c, ctx, c_ctx, mod_w, mod_b, norm1_g, norm2_g, hyb_w_in, hyb_w_out,
              a_q_norm_g, a_k_norm_g, b_conv_w, b_conv_b, b_gate_a_w, b_gate_a_b,
              b_gate_x_w, b_gate_x_b, b_lru_lambda, c_w_in, c_w_out, c_lambda_q1, c_lambda_k1,
              c_lambda_q2, c_lambda_k2, c_subln_g, router_w, router_b, exp_w_gu, exp_b_gu,
              exp_w_d, exp_b_d, final_g):
    ROWS = x.shape[1] // GRID_W
    rope_a = axial_rope_tables(ROWS, A_HEAD_DIM)
    rope_c = axial_rope_tables(ROWS, C_HEAD_DIM)
    silu_c = jax.nn.silu(c)
    silu_cc = jax.nn.silu(c_ctx)
    z = ctx
    for l in range(DEPTH):
        with_ctx = l < DEPTH - 1
        mod_l = silu_c @ mod_w[l] + mod_b[l]
        mod_c = silu_cc @ mod_w[l] + mod_b[l]
        sh1, sc1, g1, sh2, sc2, g2 = jnp.split(mod_l[:, None, :], 6, axis=-1)
        sh1c, sc1c, g1c, sh2c, sc2c, g2c = jnp.split(mod_c, 6, axis=-1)
        h_l = modulate(x, norm1_g[l], sh1, sc1)
        h_c = modulate(z, norm1_g[l], sh1c, sc1c)
        i = l // 2
        if l % 2 == 0:
            o_l, o_c = hybrid_mixer(h_l, h_c, rope_a, hyb_w_in[i], hyb_w_out[i], a_q_norm_g[i],
                                    a_k_norm_g[i], b_conv_w[i], b_conv_b[i], b_gate_a_w[i],
                                    b_gate_a_b[i], b_gate_x_w[i], b_gate_x_b[i], b_lru_lambda[i],
                                    with_ctx)
        else:
            lam_init = 0.8 - 0.6 * math.exp(-0.3 * l)
            o_l, o_c = diff_mixer(h_l, h_c, rope_c, c_w_in[i], c_w_out[i], c_lambda_q1[i],
                                  c_lambda_k1[i], c_lambda_q2[i], c_lambda_k2[i], c_subln_g[i],
                                  lam_init, with_ctx)
        x = x + g1 * o_l
        h2_l = modulate(x, norm2_g[l], sh2, sc2).reshape(-1, x.shape[-1])
        if with_ctx:
            z = z + g1c * o_c
            h2_c = modulate(z, norm2_g[l], sh2c, sc2c).reshape(-1, z.shape[-1])
            n_lat = h2_l.shape[0]
            y = moe_ffn(jnp.concatenate([h2_l, h2_c], axis=0), router_w[l], router_b[l],
                        exp_w_gu[l], exp_b_gu[l], exp_w_d[l], exp_b_d[l])
            x = x + g2 * y[:n_lat].reshape(x.shape)
            z = z + g2c * y[n_lat:].reshape(z.shape)
        else:
            y = moe_ffn(h2_l, router_w[l], router_b[l], exp_w_gu[l], exp_b_gu[l],
                        exp_w_d[l], exp_b_d[l])
            x = x + g2 * y.reshape(x.shape)
    return rms_norm(x, final_g)
```

```python
import functools
import math

import jax
import jax.numpy as jnp
from jax import lax
from jax.experimental import pallas as pl
from jax.experimental.pallas import tpu as pltpu

F32 = jnp.float32
BF16 = jnp.bfloat16

GRID_W = 64
ROPE_THETA = 10000.0
NORM_EPS = 1e-6
A_HEAD_DIM = 128
A_GROUP = 4
B_BLOCK = 128
LRU_C = 8.0
C_HEAD_DIM = 64
TOP_K = 4
SWIGLU_LIMIT = 7.0
SWIGLU_ALPHA = 1.702

LANE = 128
SUBLANE = 8
VMEM_LIMIT_BYTES = 56 * 1024 * 1024


def _params():
    return pltpu.CompilerParams(vmem_limit_bytes=VMEM_LIMIT_BYTES)


def _tile(n, target, mult=SUBLANE):
    best = None
    for t in range(mult, min(n, target) + 1, mult):
        if n % t == 0:
            best = t
    assert best is not None, (n, target, mult)
    return best


def _ctx_rows(i, bm, rpb, n_lat):
    pos0 = lax.rem(i * bm, rpb)
    row = lax.broadcasted_iota(jnp.int32, (bm, 1), 0)
    return (pos0 + row) >= n_lat


def _mod_row(modb_ref, modc_ref, k, is_ctx):
    return jnp.where(is_ctx, modc_ref[0, k:k + 1, :], modb_ref[0, k:k + 1, :])


def _rms(x, g):
    return x * lax.rsqrt(jnp.mean(x * x, axis=-1, keepdims=True) + NORM_EPS) * g


def _dot(a, b):
    return jnp.dot(a, b, preferred_element_type=F32)


def _dot_nt(a, b):
    return lax.dot_general(a, b, (((1,), (1,)), ((), ())), preferred_element_type=F32)


def _modmat_kernel(c_ref, w_ref, b_ref, o_ref):
    c = c_ref[...]
    s = (c * jax.nn.sigmoid(c)).astype(BF16)
    o_ref[0] = _dot(s, w_ref[0].astype(BF16)) + b_ref[0]


def _modmat(cvec, mod_w, mod_b):
    depth, d, n = mod_w.shape
    r = cvec.shape[0]
    bn = _tile(n, 512, LANE)
    return pl.pallas_call(
        _modmat_kernel,
        grid=(depth, n // bn),
        in_specs=[
            pl.BlockSpec((r, d), lambda l, j: (0, 0)),
            pl.BlockSpec((1, d, bn), lambda l, j: (l, 0, j)),
            pl.BlockSpec((1, 1, bn), lambda l, j: (l, 0, j)),
        ],
        out_specs=pl.BlockSpec((1, r, bn), lambda l, j: (l, 0, j)),
        out_shape=jax.ShapeDtypeStruct((depth, r, n), F32),
        compiler_params=_params(),
        name="modmat",
    )(cvec, mod_w, mod_b.reshape(depth, 1, n))


def _modulate_kernel(x_ref, g_ref, modb_ref, modc_ref, h_ref, *, k, bm, rpb, n_lat):
    is_ctx = _ctx_rows(pl.program_id(0), bm, rpb, n_lat)
    y = _rms(x_ref[...], g_ref[...])
    h = y * (1.0 + _mod_row(modb_ref, modc_ref, k + 1, is_ctx)) + _mod_row(modb_ref, modc_ref, k, is_ctx)
    h_ref[...] = h.astype(BF16)


def _mod_specs(bm, rpb, batch, d):
    return [
        pl.BlockSpec((1, 6, d), lambda i: ((i * bm) // rpb, 0, 0)),
        pl.BlockSpec((1, 6, d), lambda i: (batch, 0, 0)),
    ]


def _modulate(xs, g, mod3, *, k, rpb, n_lat, batch):
    t, d = xs.shape
    bm = _tile(rpb, 256)
    return pl.pallas_call(
        functools.partial(_modulate_kernel, k=k, bm=bm, rpb=rpb, n_lat=n_lat),
        grid=(t // bm,),
        in_specs=[pl.BlockSpec((bm, d), lambda i: (i, 0)), pl.BlockSpec((1, d), lambda i: (0, 0))]
        + _mod_specs(bm, rpb, batch, d),
        out_specs=pl.BlockSpec((bm, d), lambda i: (i, 0)),
        out_shape=jax.ShapeDtypeStruct((t, d), BF16),
        compiler_params=_params(),
        name="modulate",
    )(xs, g.reshape(1, d), mod3, mod3)


def _router_kernel(x_ref, g_ref, modb_ref, modc_ref, rw_ref, rb_ref, h_ref, idx_ref, wt_ref,
                   *, k, bm, rpb, n_lat):
    is_ctx = _ctx_rows(pl.program_id(0), bm, rpb, n_lat)
    y = _rms(x_ref[...], g_ref[...])
    h = y * (1.0 + _mod_row(modb_ref, modc_ref, k + 1, is_ctx)) + _mod_row(modb_ref, modc_ref, k, is_ctx)
    h_ref[...] = h.astype(BF16)
    logits = jnp.dot(h, rw_ref[...], precision=lax.Precision.HIGHEST, preferred_element_type=F32) + rb_ref[...]
    lane = lax.broadcasted_iota(jnp.int32, logits.shape, 1)
    vals, idxs = [], []
    for _ in range(TOP_K):
        m = jnp.max(logits, axis=-1, keepdims=True)
        ik = jnp.min(jnp.where(logits == m, lane, LANE), axis=-1, keepdims=True)
        vals.append(m)
        idxs.append(ik)
        logits = jnp.where(lane == ik, -jnp.inf, logits)
    es = [jnp.exp(v - vals[0]) for v in vals]
    den = es[0] + es[1] + es[2] + es[3]
    idx_out = jnp.zeros(lane.shape, jnp.int32)
    wt_out = jnp.zeros(lane.shape, F32)
    for j in range(TOP_K):
        idx_out = jnp.where(lane == j, idxs[j], idx_out)
        wt_out = jnp.where(lane == j, es[j] / den, wt_out)
    idx_ref[...] = idx_out
    wt_ref[...] = wt_out


def _modulate_route(xs, g, mod3, router_w, router_b, *, k, rpb, n_lat, batch):
    t, d = xs.shape
    e = router_w.shape[1]
    bm = _tile(rpb, 256)
    rw = jnp.zeros((d, LANE), F32).at[:, :e].set(router_w)
    rb = jnp.full((1, LANE), -1e30, F32).at[0, :e].set(router_b)
    return pl.pallas_call(
        functools.partial(_router_kernel, k=k, bm=bm, rpb=rpb, n_lat=n_lat),
        grid=(t // bm,),
        in_specs=[pl.BlockSpec((bm, d), lambda i: (i, 0)), pl.BlockSpec((1, d), lambda i: (0, 0))]
        + _mod_specs(bm, rpb, batch, d)
        + [pl.BlockSpec((d, LANE), lambda i: (0, 0)), pl.BlockSpec((1, LANE), lambda i: (0, 0))],
        out_specs=[pl.BlockSpec((bm, d), lambda i: (i, 0)),
                   pl.BlockSpec((bm, LANE), lambda i: (i, 0)),
                   pl.BlockSpec((bm, LANE), lambda i: (i, 0))],
        out_shape=[jax.ShapeDtypeStruct((t, d), BF16),
                   jax.ShapeDtypeStruct((t, LANE), jnp.int32),
                   jax.ShapeDtypeStruct((t, LANE), F32)],
        compiler_params=_params(),
        name="modulate_route",
    )(xs, g.reshape(1, d), mod3, mod3, rw, rb)


def _gelu(x):
    return 0.5 * x * (1.0 + jnp.tanh(math.sqrt(2.0 / math.pi) * (x + 0.044715 * (x * x * x))))


def _proj_kernel(*refs, norm, rope_shift, act, bn):
    a_ref, w_ref = refs[0], refs[1]
    o_ref = refs[-1]
    pos = 2
    g_ref = cos_ref = sin_ref = None
    if norm:
        g_ref = refs[pos]
        pos += 1
    if rope_shift:
        cos_ref, sin_ref = refs[pos], refs[pos + 1]
    acc = _dot(a_ref[...], w_ref[...])
    if not norm and not rope_shift:
        if act == "gelu":
            acc = _gelu(acc)
        o_ref[...] = acc.astype(o_ref.dtype)
        return
    lane = lax.broadcasted_iota(jnp.int32, (acc.shape[0], LANE), 1)
    first_half = jnp.bitwise_and(lane, 2 * rope_shift - 1) < rope_shift if rope_shift else None
    for c in range(bn // LANE):
        y = acc[:, c * LANE:(c + 1) * LANE]
        if norm:
            y = _rms(y, g_ref[...])
        if rope_shift:
            partner = jnp.where(first_half, pltpu.roll(y, LANE - rope_shift, 1), pltpu.roll(y, rope_shift, 1))
            y = y * cos_ref[...] + partner * sin_ref[...]
        o_ref[:, c * LANE:(c + 1) * LANE] = y.astype(o_ref.dtype)


def _proj(a, w, *, rpb, out_dtype, norm_g=None, rope=None, act=None):
    t, kdim = a.shape
    n = w.shape[1]
    bm = _tile(rpb, 1152, 16)
    bn = _tile(n, 1024, LANE)
    tiles_per_sample = rpb // bm
    in_specs = [pl.BlockSpec((bm, kdim), lambda j, i: (i, 0)), pl.BlockSpec((kdim, bn), lambda j, i: (0, j))]
    args = [a, w]
    if norm_g is not None:
        in_specs.append(pl.BlockSpec((1, LANE), lambda j, i: (0, 0)))
        args.append(norm_g.reshape(1, LANE).astype(F32))
    rope_shift = 0
    if rope is not None:
        cos, sin, rope_shift = rope
        in_specs += [pl.BlockSpec((bm, LANE), lambda j, i: (i % tiles_per_sample, 0))] * 2
        args += [cos, sin]
    return pl.pallas_call(
        functools.partial(_proj_kernel, norm=norm_g is not None, rope_shift=rope_shift, act=act, bn=bn),
        grid=(n // bn, t // bm),
        in_specs=in_specs,
        out_specs=pl.BlockSpec((bm, bn), lambda j, i: (i, j)),
        out_shape=jax.ShapeDtypeStruct((t, n), out_dtype),
        compiler_params=_params(),
        name="proj",
    )(*args)


def _resid_kernel(*refs, n_pairs, k, bm, rpb, n_lat):
    x_ref, modb_ref, modc_ref = refs[0], refs[1], refs[2]
    o_ref = refs[-1]
    acc = None
    for p in range(n_pairs):
        part = _dot(refs[3 + 2 * p][...], refs[4 + 2 * p][...])
        acc = part if acc is None else acc + part
    is_ctx = _ctx_rows(pl.program_id(1), bm, rpb, n_lat)
    o_ref[...] = x_ref[...] + _mod_row(modb_ref, modc_ref, k, is_ctx) * acc


def _resid_proj(xs, mod3, pairs, *, k, rpb, n_lat, batch):
    t, d = xs.shape
    bm = _tile(rpb, 768, 16)
    bn = _tile(d, 1024, LANE)
    in_specs = [
        pl.BlockSpec((bm, bn), lambda j, i: (i, j)),
        pl.BlockSpec((1, 6, bn), lambda j, i: ((i * bm) // rpb, 0, j)),
        pl.BlockSpec((1, 6, bn), lambda j, i: (batch, 0, j)),
    ]
    args = [xs, mod3, mod3]
    for a, w in pairs:
        kdim = a.shape[1]
        in_specs += [pl.BlockSpec((bm, kdim), lambda j, i: (i, 0)), pl.BlockSpec((kdim, bn), lambda j, i: (0, j))]
        args += [a, w]
    return pl.pallas_call(
        functools.partial(_resid_kernel, n_pairs=len(pairs), k=k, bm=bm, rpb=rpb, n_lat=n_lat),
        grid=(d // bn, t // bm),
        in_specs=in_specs,
        out_specs=pl.BlockSpec((bm, bn), lambda j, i: (i, j)),
        out_shape=jax.ShapeDtypeStruct((t, d), F32),
        compiler_params=_params(),
        name="resid_proj",
    )(*args)


def _softmax_parts(q, k, scale):
    s = _dot_nt(q, k) * scale
    m = jnp.max(s, axis=-1, keepdims=True)
    p = jnp.exp(s - m)
    return p, jnp.sum(p, axis=-1, keepdims=True)


def _gqa_kernel(q_ref, k_ref, v_ref, prev_ref, o_ref, *, group, scale):
    del prev_ref
    k = k_ref[...]
    v = v_ref[...]
    for g in range(group):
        q = q_ref[:, g * A_HEAD_DIM:(g + 1) * A_HEAD_DIM]
        p, den = _softmax_parts(q, k, scale)
        o = _dot(p.astype(BF16), v)
        o_ref[:, g * A_HEAD_DIM:(g + 1) * A_HEAD_DIM] = (o / den).astype(o_ref.dtype)


def _gqa(q, k, v, att, *, rpb, n_lat, batch, ctx_queries):
    t = q.shape[0]
    kvh = k.shape[1] // A_HEAD_DIM
    gw = A_GROUP * A_HEAD_DIM
    n_ctx = rpb - n_lat
    if ctx_queries:
        tq, nq, lk = n_ctx, 1, n_ctx
        q_row = lambda b, h, i: b * (rpb // n_ctx) + n_lat // n_ctx
        k_row = q_row
    else:
        tq = _tile(math.gcd(n_lat, rpb), 256)
        nq, lk = n_lat // tq, rpb
        q_row = lambda b, h, i: b * (rpb // tq) + i
        k_row = lambda b, h, i: b
    kernel = functools.partial(_gqa_kernel, group=A_GROUP, scale=A_HEAD_DIM ** -0.5)
    return pl.pallas_call(
        kernel,
        grid=(batch, kvh, nq),
        in_specs=[
            pl.BlockSpec((tq, gw), lambda b, h, i: (q_row(b, h, i), h)),
            pl.BlockSpec((lk, A_HEAD_DIM), lambda b, h, i: (k_row(b, h, i), h)),
            pl.BlockSpec((lk, A_HEAD_DIM), lambda b, h, i: (k_row(b, h, i), h)),
            pl.BlockSpec(memory_space=pl.ANY),
        ],
        out_specs=pl.BlockSpec((tq, gw), lambda b, h, i: (q_row(b, h, i), h)),
        out_shape=jax.ShapeDtypeStruct((t, q.shape[1]), BF16),
        input_output_aliases={3: 0},
        compiler_params=_params(),
        name="gqa_ctx" if ctx_queries else "gqa_lat",
    )(q, k, v, att)


def _diff_kernel(lam_ref, g_ref, q_ref, k_ref, v_ref, prev_ref, o_ref, *, lam_init, scale):
    del prev_ref
    lv = lam_ref[...]
    lam = (jnp.exp(jnp.sum(lv[0:1] * lv[1:2], axis=-1, keepdims=True))
           - jnp.exp(jnp.sum(lv[2:3] * lv[3:4], axis=-1, keepdims=True)) + lam_init)
    q = q_ref[...]
    k = k_ref[...]
    lane = lax.broadcasted_iota(jnp.int32, q.shape, 1)
    zero = jnp.zeros_like(q)
    p1, d1 = _softmax_parts(jnp.where(lane < C_HEAD_DIM, q, zero), k, scale)
    p2, d2 = _softmax_parts(jnp.where(lane >= C_HEAD_DIM, q, zero), k, scale)
    w = p1 * (1.0 / d1) - lam * (p2 * (1.0 / d2))
    o = _dot(w.astype(BF16), v_ref[...])
    o_ref[...] = (_rms(o, g_ref[...]) * (1.0 - lam_init)).astype(o_ref.dtype)


def _diff_attn(qk, v, lamvec, subln_g, out, *, lam_init, rpb, n_lat, batch, ctx_queries):
    t = qk.shape[0]
    heads = v.shape[1] // (2 * C_HEAD_DIM)
    hw = 2 * C_HEAD_DIM
    n_ctx = rpb - n_lat
    if ctx_queries:
        tq, nq, lk = n_ctx, 1, n_ctx
        q_row = lambda b, h, i: b * (rpb // n_ctx) + n_lat // n_ctx
        k_row = q_row
    else:
        tq = _tile(math.gcd(n_lat, rpb), 256)
        nq, lk = n_lat // tq, rpb
        q_row = lambda b, h, i: b * (rpb // tq) + i
        k_row = lambda b, h, i: b
    kernel = functools.partial(_diff_kernel, lam_init=lam_init, scale=C_HEAD_DIM ** -0.5)
    return pl.pallas_call(
        kernel,
        grid=(batch, heads, nq),
        in_specs=[
            pl.BlockSpec(lamvec.shape, lambda b, h, i: (0, 0)),
            pl.BlockSpec((1, hw), lambda b, h, i: (0, 0)),
            pl.BlockSpec((tq, hw), lambda b, h, i: (q_row(b, h, i), h)),
            pl.BlockSpec((lk, hw), lambda b, h, i: (k_row(b, h, i), heads + h)),
            pl.BlockSpec((lk, hw), lambda b, h, i: (k_row(b, h, i), h)),
            pl.BlockSpec(memory_space=pl.ANY),
        ],
        out_specs=pl.BlockSpec((tq, hw), lambda b, h, i: (q_row(b, h, i), h)),
        out_shape=jax.ShapeDtypeStruct((t, v.shape[1]), BF16),
        input_output_aliases={5: 0},
        compiler_params=_params(),
        name="diff_ctx" if ctx_queries else "diff_lat",
    )(lamvec, subln_g.reshape(1, hw).astype(F32), qk, qk, v, out)


CONV_PAD = SUBLANE


def _block_scan(a, u, carry, reverse):
    row = lax.broadcasted_iota(jnp.int32, a.shape, 0)
    for s in (1, 2, 4):
        shift = SUBLANE - s if reverse else s
        valid = (row < SUBLANE - s) if reverse else (row >= s)
        a_prev = pltpu.roll(a, shift, 0)
        u_prev = pltpu.roll(u, shift, 0)
        u = jnp.where(valid, a * u_prev + u, u)
        a = jnp.where(valid, a * a_prev, a)
    h = u + a * carry
    last = h[0:1] if reverse else h[SUBLANE - 1:SUBLANE]
    return h, jnp.broadcast_to(last, h.shape)


def _rglru_kernel(xb_ref, yb_ref, cw_ref, cb_ref, gw_ref, gb_ref, lam_ref, o_ref,
                  xpad_ref, xc_ref, a_ref, u_ref, hf_ref, hb_ref, *, n_lat, n_ctx):
    cw = cw_ref[...]
    taps = cw.shape[0]
    zpad = jnp.zeros((CONV_PAD, B_BLOCK), F32)

    def conv_segment(r0, n):
        xpad_ref[0:CONV_PAD] = zpad
        xpad_ref[CONV_PAD:CONV_PAD + n] = xb_ref[r0:r0 + n]
        xpad_ref[CONV_PAD + n:2 * CONV_PAD + n] = zpad
        acc = jnp.broadcast_to(cb_ref[...], (n, B_BLOCK))
        for j in range(taps):
            off = CONV_PAD + j - taps // 2
            acc = acc + cw[j:j + 1] * xpad_ref[off:off + n]
        xc_ref[r0:r0 + n] = acc

    conv_segment(0, n_lat)
    conv_segment(n_lat, n_ctx)

    xc = xc_ref[...]
    gates = _dot(xc.astype(BF16), gw_ref[0]) + gb_ref[0]
    lam = lam_ref[...]
    softplus_neg = jnp.maximum(-lam, 0.0) + jnp.log(1.0 + jnp.exp(-jnp.abs(lam)))
    for d in range(2):
        r = jax.nn.sigmoid(gates[:, (2 * d) * B_BLOCK:(2 * d + 1) * B_BLOCK])
        i = jax.nn.sigmoid(gates[:, (2 * d + 1) * B_BLOCK:(2 * d + 2) * B_BLOCK])
        log_a = -LRU_C * r * softplus_neg[d:d + 1]
        a_ref[d] = jnp.exp(log_a)
        u_ref[d] = jnp.sqrt(1.0 - jnp.exp(2.0 * log_a)) * i * xc

    def scan_segment(r0, n, carry):
        nblk = n // SUBLANE

        def body(b, c):
            cf, cb = c
            rf = pl.multiple_of(r0 + b * SUBLANE, SUBLANE)
            rb = pl.multiple_of(r0 + (nblk - 1 - b) * SUBLANE, SUBLANE)
            hf, cf = _block_scan(a_ref[0, pl.ds(rf, SUBLANE), :], u_ref[0, pl.ds(rf, SUBLANE), :], cf, False)
            hb, cb = _block_scan(a_ref[1, pl.ds(rb, SUBLANE), :], u_ref[1, pl.ds(rb, SUBLANE), :], cb, True)
            hf_ref[pl.ds(rf, SUBLANE), :] = hf
            hb_ref[pl.ds(rb, SUBLANE), :] = hb
            return cf, cb

        return lax.fori_loop(0, nblk, body, carry)

    zero = jnp.zeros((SUBLANE, B_BLOCK), F32)
    carry = scan_segment(n_lat, n_ctx, (zero, zero))
    scan_segment(0, n_lat, carry)
    o_ref[...] = ((hf_ref[...] + hb_ref[...]) * _gelu(yb_ref[...])).astype(o_ref.dtype)


def _rglru(xb, yb, conv_w, conv_b, gw, gb, lam, *, rpb, n_lat, batch):
    t, width = xb.shape
    nh = width // B_BLOCK
    n_ctx = rpb - n_lat
    taps = conv_w.shape[0]
    return pl.pallas_call(
        functools.partial(_rglru_kernel, n_lat=n_lat, n_ctx=n_ctx),
        grid=(batch, nh),
        in_specs=[
            pl.BlockSpec((rpb, B_BLOCK), lambda b, h: (b, h)),
            pl.BlockSpec((rpb, B_BLOCK), lambda b, h: (b, h)),
            pl.BlockSpec((taps, B_BLOCK), lambda b, h: (0, h)),
            pl.BlockSpec((1, B_BLOCK), lambda b, h: (0, h)),
            pl.BlockSpec((1, B_BLOCK, 4 * B_BLOCK), lambda b, h: (h, 0, 0)),
            pl.BlockSpec((1, 1, 4 * B_BLOCK), lambda b, h: (h, 0, 0)),
            pl.BlockSpec((2, B_BLOCK), lambda b, h: (0, h)),
        ],
        out_specs=pl.BlockSpec((rpb, B_BLOCK), lambda b, h: (b, h)),
        out_shape=jax.ShapeDtypeStruct((t, width), BF16),
        scratch_shapes=[
            pltpu.VMEM((n_lat + 2 * CONV_PAD, B_BLOCK), F32),
            pltpu.VMEM((rpb, B_BLOCK), F32),
            pltpu.VMEM((2, rpb, B_BLOCK), F32),
            pltpu.VMEM((2, rpb, B_BLOCK), F32),
            pltpu.VMEM((rpb, B_BLOCK), F32),
            pltpu.VMEM((rpb, B_BLOCK), F32),
        ],
        compiler_params=_params(),
        name="rglru",
    )(xb, yb, conv_w, conv_b.reshape(1, width), gw, gb, lam)


def _expert_kernel(te_ref, nu_ref, x_ref, wgu_ref, bgu_ref, wd_ref, bd_ref, rw_ref, o_ref, *, f):
    del te_ref

    @pl.when(pl.program_id(0) < nu_ref[0])
    def _():
        gu = _dot(x_ref[...], wgu_ref[0]) + bgu_ref[0]
        gate = jnp.minimum(gu[:, :f], SWIGLU_LIMIT)
        up = jnp.clip(gu[:, f:], -SWIGLU_LIMIT, SWIGLU_LIMIT)
        act = (up + 1.0) * gate * jax.nn.sigmoid(gate * SWIGLU_ALPHA)
        y = _dot(act.astype(BF16), wd_ref[0]) + bd_ref[0]
        o_ref[...] = (rw_ref[...] * y).astype(o_ref.dtype)

    @pl.when(pl.program_id(0) >= nu_ref[0])
    def _():
        o_ref[...] = jnp.zeros(o_ref.shape, o_ref.dtype)


def _experts(x_sorted, row_w, tile_expert, n_used, w_gu, b_gu, w_d, b_d, *, tm):
    r, d = x_sorted.shape
    e, _, f2 = w_gu.shape
    f = f2 // 2
    grid_spec = pltpu.PrefetchScalarGridSpec(
        num_scalar_prefetch=2,
        grid=(r // tm,),
        in_specs=[
            pl.BlockSpec((tm, d), lambda j, te, nu: (j, 0)),
            pl.BlockSpec((1, d, f2), lambda j, te, nu: (te[j], 0, 0)),
            pl.BlockSpec((1, 1, f2), lambda j, te, nu: (te[j], 0, 0)),
            pl.BlockSpec((1, f, d), lambda j, te, nu: (te[j], 0, 0)),
            pl.BlockSpec((1, 1, d), lambda j, te, nu: (te[j], 0, 0)),
            pl.BlockSpec((tm, 1), lambda j, te, nu: (j, 0)),
        ],
        out_specs=pl.BlockSpec((tm, d), lambda j, te, nu: (j, 0)),
    )
    return pl.pallas_call(
        functools.partial(_expert_kernel, f=f),
        grid_spec=grid_spec,
        out_shape=jax.ShapeDtypeStruct((r, d), BF16),
        compiler_params=_params(),
        name="experts",
    )(tile_expert, n_used, x_sorted, w_gu, b_gu.reshape(e, 1, f2), w_d, b_d.reshape(e, 1, d), row_w)


def _combine_kernel(y_ref, x_ref, modb_ref, modc_ref, o_ref, *, k, bm, rpb, n_lat):
    is_ctx = _ctx_rows(pl.program_id(0), bm, rpb, n_lat)
    y = y_ref[0].astype(F32)
    for j in range(1, TOP_K):
        y = y + y_ref[j].astype(F32)
    o_ref[...] = x_ref[...] + _mod_row(modb_ref, modc_ref, k, is_ctx) * y


def _combine(y4, xs, mod3, *, k, rpb, n_lat, batch):
    t, d = xs.shape
    bm = _tile(rpb, 256)
    return pl.pallas_call(
        functools.partial(_combine_kernel, k=k, bm=bm, rpb=rpb, n_lat=n_lat),
        grid=(t // bm,),
        in_specs=[pl.BlockSpec((TOP_K, bm, d), lambda i: (0, i, 0)), pl.BlockSpec((bm, d), lambda i: (i, 0))]
        + _mod_specs(bm, rpb, batch, d),
        out_specs=pl.BlockSpec((bm, d), lambda i: (i, 0)),
        out_shape=jax.ShapeDtypeStruct((t, d), F32),
        compiler_params=_params(),
        name="combine",
    )(y4, xs, mod3, mod3)


def _moe(xs, g, mod3, router_w, router_b, w_gu, b_gu, w_d, b_d, *, rpb, n_lat, batch):
    t, d = xs.shape
    e = router_w.shape[1]
    tm = 512
    h2, idx, wt = _modulate_route(xs, g, mod3, router_w, router_b, k=3, rpb=rpb, n_lat=n_lat, batch=batch)
    top_idx = idx[:, :TOP_K]
    top_w = wt[:, :TOP_K]
    chosen = (top_idx[:, :, None] == jnp.arange(e, dtype=jnp.int32)[None, None, :]).any(axis=1).astype(jnp.int32)
    before = jnp.cumsum(chosen, axis=0) - chosen
    counts = before[-1] + chosen[-1]
    padded = ((counts + tm - 1) // tm) * tm
    group_end = jnp.cumsum(padded)
    group_start = group_end - padded
    dest = group_start[top_idx] + jnp.take_along_axis(before, top_idx, axis=1)
    n_rows = TOP_K * t + e * tm
    n_tiles = n_rows // tm
    tile_expert = jnp.minimum(
        jnp.searchsorted(group_end // tm, jnp.arange(n_tiles, dtype=jnp.int32), side="right"), e - 1).astype(jnp.int32)
    n_used = (group_end[-1] // tm).astype(jnp.int32).reshape(1)
    flat_dest = dest.reshape(-1)
    row_token = jnp.zeros((n_rows,), jnp.int32).at[flat_dest].set(
        jnp.repeat(jnp.arange(t, dtype=jnp.int32), TOP_K), unique_indices=True)
    row_w = jnp.zeros((n_rows,), F32).at[flat_dest].set(top_w.reshape(-1), unique_indices=True)
    x_sorted = jnp.take(h2, row_token, axis=0, mode="clip")
    rows = _experts(x_sorted, row_w.reshape(n_rows, 1), tile_expert, n_used, w_gu, b_gu, w_d, b_d, tm=tm)
    y4 = jnp.take(rows, dest.T.reshape(-1), axis=0, mode="clip").reshape(TOP_K, t, d)
    return _combine(y4, xs, mod3, k=5, rpb=rpb, n_lat=n_lat, batch=batch)


def _final_kernel(x_ref, g_ref, o_ref):
    o_ref[0] = _rms(x_ref[0], g_ref[...])


def _final_norm(xs3, g, n_lat):
    batch, rpb, d = xs3.shape
    bm = _tile(n_lat, 256)
    return pl.pallas_call(
        _final_kernel,
        grid=(batch, n_lat // bm),
        in_specs=[pl.BlockSpec((1, bm, d), lambda b, i: (b, i, 0)), pl.BlockSpec((1, d), lambda b, i: (0, 0))],
        out_specs=pl.BlockSpec((1, bm, d), lambda b, i: (b, i, 0)),
        out_shape=jax.ShapeDtypeStruct((batch, n_lat, d), F32),
        compiler_params=_params(),
        name="final_norm",
    )(xs3, g.reshape(1, d))


def _rope_tables(n_lat, n_ctx, head_dim):
    n_freq = head_dim // 4
    inv_freq = ROPE_THETA ** (-jnp.arange(n_freq, dtype=F32) / n_freq)
    tok = jnp.arange(n_lat, dtype=jnp.int32)
    ang_r = (tok // GRID_W).astype(F32)[:, None] * inv_freq
    ang_c = (tok % GRID_W).astype(F32)[:, None] * inv_freq
    cos = jnp.concatenate([jnp.cos(ang_r), jnp.cos(ang_r), jnp.cos(ang_c), jnp.cos(ang_c)], axis=-1)
    sin = jnp.concatenate([-jnp.sin(ang_r), jnp.sin(ang_r), -jnp.sin(ang_c), jnp.sin(ang_c)], axis=-1)
    reps = LANE // head_dim
    cos = jnp.concatenate([jnp.tile(cos, (1, reps)), jnp.ones((n_ctx, LANE), F32)], axis=0)
    sin = jnp.concatenate([jnp.tile(sin, (1, reps)), jnp.zeros((n_ctx, LANE), F32)], axis=0)
    return cos, sin, n_freq


def kernel(x, c, ctx, c_ctx, mod_w, mod_b, norm1_g, norm2_g, hyb_w_in, hyb_w_out, a_q_norm_g, a_k_norm_g, b_conv_w, b_conv_b, b_gate_a_w, b_gate_a_b, b_gate_x_w, b_gate_x_b, b_lru_lambda, c_w_in, c_w_out, c_lambda_q1, c_lambda_k1, c_lambda_q2, c_lambda_k2, c_subln_g, router_w, router_b, exp_w_gu, exp_b_gu, exp_w_d, exp_b_d, final_g):
    batch, n_lat, d = x.shape
    n_ctx = ctx.shape[1]
    rpb = n_lat + n_ctx
    t = batch * rpb
    depth = mod_w.shape[0]
    geo = dict(rpb=rpb, n_lat=n_lat, batch=batch)

    xs = jnp.concatenate([x, ctx], axis=1).reshape(t, d)
    mod_rows = -(-(batch + 1) // SUBLANE) * SUBLANE
    cvec = jnp.zeros((mod_rows, d), F32).at[:batch].set(c).at[batch].set(c_ctx)
    mod = _modmat(cvec, mod_w, mod_b)

    rope_a = _rope_tables(n_lat, n_ctx, A_HEAD_DIM)
    rope_c = _rope_tables(n_lat, n_ctx, C_HEAD_DIM)
    q_w = A_GROUP * A_HEAD_DIM * (hyb_w_in.shape[2] - 2 * (d // 2)) // ((A_GROUP + 2) * A_HEAD_DIM)
    kv_w = q_w // A_GROUP
    b_w = d // 2
    nh = b_w // B_BLOCK

    for l in range(depth):
        i = l // 2
        mod3 = mod[l].reshape(mod_rows, 6, d)
        h = _modulate(xs, norm1_g[l], mod3, k=0, **geo)
        if l % 2 == 0:
            w_in = hyb_w_in[i].astype(BF16)
            o = 0
            q = _proj(h, w_in[:, o:o + q_w], rpb=rpb, out_dtype=BF16, norm_g=a_q_norm_g[i], rope=rope_a)
            o += q_w
            kk = _proj(h, w_in[:, o:o + kv_w], rpb=rpb, out_dtype=BF16, norm_g=a_k_norm_g[i], rope=rope_a)
            o += kv_w
            v = _proj(h, w_in[:, o:o + kv_w], rpb=rpb, out_dtype=BF16)
            o += kv_w
            xb = _proj(h, w_in[:, o:o + b_w], rpb=rpb, out_dtype=F32)
            o += b_w
            yb = _proj(h, w_in[:, o:o + b_w], rpb=rpb, out_dtype=F32)
            att = jnp.zeros((t, q_w), BF16)
            att = _gqa(q, kk, v, att, ctx_queries=False, **geo)
            att = _gqa(q, kk, v, att, ctx_queries=True, **geo)
            gw = jnp.concatenate([b_gate_a_w[i, 0], b_gate_x_w[i, 0], b_gate_a_w[i, 1], b_gate_x_w[i, 1]],
                                 axis=-1).astype(BF16)
            gb = jnp.concatenate([b_gate_a_b[i, 0].reshape(nh, 1, B_BLOCK), b_gate_x_b[i, 0].reshape(nh, 1, B_BLOCK),
                                  b_gate_a_b[i, 1].reshape(nh, 1, B_BLOCK), b_gate_x_b[i, 1].reshape(nh, 1, B_BLOCK)],
                                 axis=-1)
            ry = _rglru(xb, yb, b_conv_w[i], b_conv_b[i], gw, gb, b_lru_lambda[i], **geo)
            w_out = hyb_w_out[i].astype(BF16)
            pairs = [(att, w_out[:q_w]), (ry, w_out[q_w:])]
        else:
            lam_init = 0.8 - 0.6 * math.exp(-0.3 * l)
            w_in = c_w_in[i].astype(BF16)
            qk_w = 2 * (w_in.shape[1] // 3)
            qk = _proj(h, w_in[:, :qk_w], rpb=rpb, out_dtype=BF16, rope=rope_c)
            v = _proj(h, w_in[:, qk_w:], rpb=rpb, out_dtype=BF16)
            lamvec = jnp.stack([c_lambda_q1[i], c_lambda_k1[i], c_lambda_q2[i], c_lambda_k2[i]]).astype(F32)
            att = jnp.zeros((t, v.shape[1]), BF16)
            att = _diff_attn(qk, v, lamvec, c_subln_g[i], att, lam_init=lam_init, ctx_queries=False, **geo)
            att = _diff_attn(qk, v, lamvec, c_subln_g[i], att, lam_init=lam_init, ctx_queries=True, **geo)
            pairs = [(att, c_w_out[i].astype(BF16))]
        xs = _resid_proj(xs, mod3, pairs, k=2, **geo)
        xs = _moe(xs, norm2_g[l], mod3, router_w[l], router_b[l], exp_w_gu[l].astype(BF16), exp_b_gu[l],
                  exp_w_d[l].astype(BF16), exp_b_d[l], **geo)
    return _final_norm(xs.reshape(batch, rpb, d), final_g, n_lat)
```

```python
import functools
import math

import jax
import jax.numpy as jnp
from jax import lax
from jax.experimental import pallas as pl
from jax.experimental.pallas import tpu as pltpu

F32 = jnp.float32
BF16 = jnp.bfloat16

GRID_W = 64
ROPE_THETA = 10000.0
NORM_EPS = 1e-6
A_HEAD_DIM = 128
A_GROUP = 4
B_BLOCK = 128
LRU_C = 8.0
C_HEAD_DIM = 64
TOP_K = 4
SWIGLU_LIMIT = 7.0
SWIGLU_ALPHA = 1.702
LOG2E = math.log2(math.e)

LANE = 128
SUBLANE = 8
VMEM_LIMIT_BYTES = 56 * 1024 * 1024


def _params():
    return pltpu.CompilerParams(vmem_limit_bytes=VMEM_LIMIT_BYTES)


def _tile(n, target, mult=SUBLANE):
    best = None
    for t in range(mult, min(n, target) + 1, mult):
        if n % t == 0:
            best = t
    assert best is not None, (n, target, mult)
    return best


def _ctx_rows(i, bm, rpb, n_lat):
    pos0 = lax.rem(i * bm, rpb)
    row = lax.broadcasted_iota(jnp.int32, (bm, 1), 0)
    return (pos0 + row) >= n_lat


def _mod_row(modb_ref, modc_ref, k, is_ctx):
    return jnp.where(is_ctx, modc_ref[0, k:k + 1, :], modb_ref[0, k:k + 1, :])


def _rms(x, g):
    return x * lax.rsqrt(jnp.mean(x * x, axis=-1, keepdims=True) + NORM_EPS) * g


def _dot(a, b):
    return jnp.dot(a, b, preferred_element_type=F32)


def _dot_nt(a, b):
    return lax.dot_general(a, b, (((1,), (1,)), ((), ())), preferred_element_type=F32)


def _modmat_kernel(c_ref, w_ref, b_ref, o_ref):
    c = c_ref[...]
    s = (c * jax.nn.sigmoid(c)).astype(BF16)
    o_ref[0] = _dot(s, w_ref[0].astype(BF16)) + b_ref[0]


def _modmat(cvec, mod_w, mod_b):
    depth, d, n = mod_w.shape
    r = cvec.shape[0]
    bn = _tile(n, 512, LANE)
    return pl.pallas_call(
        _modmat_kernel,
        grid=(depth, n // bn),
        in_specs=[
            pl.BlockSpec((r, d), lambda l, j: (0, 0)),
            pl.BlockSpec((1, d, bn), lambda l, j: (l, 0, j)),
            pl.BlockSpec((1, 1, bn), lambda l, j: (l, 0, j)),
        ],
        out_specs=pl.BlockSpec((1, r, bn), lambda l, j: (l, 0, j)),
        out_shape=jax.ShapeDtypeStruct((depth, r, n), F32),
        compiler_params=_params(),
        name="modmat",
    )(cvec, mod_w, mod_b.reshape(depth, 1, n))


def _modulate_kernel(x_ref, g_ref, modb_ref, modc_ref, h_ref, *, k, bm, rpb, n_lat):
    is_ctx = _ctx_rows(pl.program_id(0), bm, rpb, n_lat)
    y = _rms(x_ref[...], g_ref[...])
    h = y * (1.0 + _mod_row(modb_ref, modc_ref, k + 1, is_ctx)) + _mod_row(modb_ref, modc_ref, k, is_ctx)
    h_ref[...] = h.astype(BF16)


def _mod_specs(bm, rpb, batch, d):
    return [
        pl.BlockSpec((1, 6, d), lambda i: ((i * bm) // rpb, 0, 0)),
        pl.BlockSpec((1, 6, d), lambda i: (batch, 0, 0)),
    ]


def _modulate(xs, g, mod3, *, k, rpb, n_lat, batch):
    t, d = xs.shape
    bm = _tile(rpb, 256)
    return pl.pallas_call(
        functools.partial(_modulate_kernel, k=k, bm=bm, rpb=rpb, n_lat=n_lat),
        grid=(t // bm,),
        in_specs=[pl.BlockSpec((bm, d), lambda i: (i, 0)), pl.BlockSpec((1, d), lambda i: (0, 0))]
        + _mod_specs(bm, rpb, batch, d),
        out_specs=pl.BlockSpec((bm, d), lambda i: (i, 0)),
        out_shape=jax.ShapeDtypeStruct((t, d), BF16),
        compiler_params=_params(),
        name="modulate",
    )(xs, g.reshape(1, d), mod3, mod3)


def _router_kernel(x_ref, g_ref, modb_ref, modc_ref, rw_ref, rb_ref, h_ref, idx_ref, wt_ref,
                   *, k, bm, rpb, n_lat):
    is_ctx = _ctx_rows(pl.program_id(0), bm, rpb, n_lat)
    y = _rms(x_ref[...], g_ref[...])
    h = y * (1.0 + _mod_row(modb_ref, modc_ref, k + 1, is_ctx)) + _mod_row(modb_ref, modc_ref, k, is_ctx)
    h_ref[...] = h.astype(BF16)
    logits = jnp.dot(h, rw_ref[...], precision=lax.Precision.HIGHEST, preferred_element_type=F32) + rb_ref[...]
    lane = lax.broadcasted_iota(jnp.int32, logits.shape, 1)
    vals, idxs = [], []
    for _ in range(TOP_K):
        m = jnp.max(logits, axis=-1, keepdims=True)
        ik = jnp.min(jnp.where(logits == m, lane, LANE), axis=-1, keepdims=True)
        vals.append(m)
        idxs.append(ik)
        logits = jnp.where(lane == ik, -jnp.inf, logits)
    es = [jnp.exp(v - vals[0]) for v in vals]
    den = es[0] + es[1] + es[2] + es[3]
    idx_out = jnp.zeros(lane.shape, jnp.int32)
    wt_out = jnp.zeros(lane.shape, F32)
    for j in range(TOP_K):
        idx_out = jnp.where(lane == j, idxs[j], idx_out)
        wt_out = jnp.where(lane == j, es[j] / den, wt_out)
    idx_ref[...] = idx_out
    wt_ref[...] = wt_out


def _modulate_route(xs, g, mod3, router_w, router_b, *, k, rpb, n_lat, batch):
    t, d = xs.shape
    e = router_w.shape[1]
    bm = _tile(rpb, 256)
    rw = jnp.zeros((d, LANE), F32).at[:, :e].set(router_w)
    rb = jnp.full((1, LANE), -1e30, F32).at[0, :e].set(router_b)
    return pl.pallas_call(
        functools.partial(_router_kernel, k=k, bm=bm, rpb=rpb, n_lat=n_lat),
        grid=(t // bm,),
        in_specs=[pl.BlockSpec((bm, d), lambda i: (i, 0)), pl.BlockSpec((1, d), lambda i: (0, 0))]
        + _mod_specs(bm, rpb, batch, d)
        + [pl.BlockSpec((d, LANE), lambda i: (0, 0)), pl.BlockSpec((1, LANE), lambda i: (0, 0))],
        out_specs=[pl.BlockSpec((bm, d), lambda i: (i, 0)),
                   pl.BlockSpec((bm, LANE), lambda i: (i, 0)),
                   pl.BlockSpec((bm, LANE), lambda i: (i, 0))],
        out_shape=[jax.ShapeDtypeStruct((t, d), BF16),
                   jax.ShapeDtypeStruct((t, LANE), jnp.int32),
                   jax.ShapeDtypeStruct((t, LANE), F32)],
        compiler_params=_params(),
        name="modulate_route",
    )(xs, g.reshape(1, d), mod3, mod3, rw, rb)


def _gelu(x):
    return 0.5 * x * (1.0 + jnp.tanh(math.sqrt(2.0 / math.pi) * (x + 0.044715 * (x * x * x))))


def _proj_kernel(*refs, norm, rope_shift, q_cols, q_scale, bn):
    a_ref, w_ref = refs[0], refs[1]
    o_ref = refs[-1]
    pos = 2
    g_ref = cos_ref = sin_ref = None
    if norm:
        g_ref = refs[pos]
        pos += 1
    if rope_shift:
        cos_ref, sin_ref = refs[pos], refs[pos + 1]
    acc = _dot(a_ref[...], w_ref[...])
    if not norm and not rope_shift:
        o_ref[...] = acc.astype(o_ref.dtype)
        return
    col_scale = jnp.where(pl.program_id(0) * bn < q_cols, q_scale, 1.0)
    lane = lax.broadcasted_iota(jnp.int32, (acc.shape[0], LANE), 1)
    first_half = jnp.bitwise_and(lane, 2 * rope_shift - 1) < rope_shift
    for c in range(bn // LANE):
        y = acc[:, c * LANE:(c + 1) * LANE]
        if norm:
            y = _rms(y, g_ref[...])
        partner = jnp.where(first_half, pltpu.roll(y, LANE - rope_shift, 1), pltpu.roll(y, rope_shift, 1))
        y = (y * cos_ref[...] + partner * sin_ref[...]) * col_scale
        o_ref[:, c * LANE:(c + 1) * LANE] = y.astype(o_ref.dtype)


def _proj(a, w, *, rpb, out_dtype, norm_g=None, rope=None, q_cols=0, q_scale=1.0):
    t, kdim = a.shape
    n = w.shape[1]
    bm = _tile(rpb, 1152, 16)
    bn = _tile(math.gcd(n, q_cols) if q_cols else n, 1024, LANE)
    tiles_per_sample = rpb // bm
    in_specs = [pl.BlockSpec((bm, kdim), lambda j, i: (i, 0)), pl.BlockSpec((kdim, bn), lambda j, i: (0, j))]
    args = [a, w]
    if norm_g is not None:
        in_specs.append(pl.BlockSpec((1, LANE), lambda j, i: (0, 0)))
        args.append(norm_g.reshape(1, LANE).astype(F32))
    rope_shift = 0
    if rope is not None:
        cos, sin, rope_shift = rope
        in_specs += [pl.BlockSpec((bm, LANE), lambda j, i: (i % tiles_per_sample, 0))] * 2
        args += [cos, sin]
    return pl.pallas_call(
        functools.partial(_proj_kernel, norm=norm_g is not None, rope_shift=rope_shift, q_cols=q_cols,
                          q_scale=q_scale, bn=bn),
        grid=(n // bn, t // bm),
        in_specs=in_specs,
        out_specs=pl.BlockSpec((bm, bn), lambda j, i: (i, j)),
        out_shape=jax.ShapeDtypeStruct((t, n), out_dtype),
        compiler_params=_params(),
        name="proj",
    )(*args)


def _resid_kernel(*refs, n_pairs, k, bm, rpb, n_lat):
    x_ref, modb_ref, modc_ref = refs[0], refs[1], refs[2]
    o_ref = refs[-1]
    acc = None
    for p in range(n_pairs):
        part = _dot(refs[3 + 2 * p][...], refs[4 + 2 * p][...])
        acc = part if acc is None else acc + part
    is_ctx = _ctx_rows(pl.program_id(1), bm, rpb, n_lat)
    o_ref[...] = x_ref[...] + _mod_row(modb_ref, modc_ref, k, is_ctx) * acc


def _resid_proj(xs, mod3, pairs, *, k, rpb, n_lat, batch):
    t, d = xs.shape
    bm = _tile(rpb, 768, 16)
    bn = _tile(d, 1024, LANE)
    in_specs = [
        pl.BlockSpec((bm, bn), lambda j, i: (i, j)),
        pl.BlockSpec((1, 6, bn), lambda j, i: ((i * bm) // rpb, 0, j)),
        pl.BlockSpec((1, 6, bn), lambda j, i: (batch, 0, j)),
    ]
    args = [xs, mod3, mod3]
    for a, w in pairs:
        kdim = a.shape[1]
        in_specs += [pl.BlockSpec((bm, kdim), lambda j, i: (i, 0)), pl.BlockSpec((kdim, bn), lambda j, i: (0, j))]
        args += [a, w]
    return pl.pallas_call(
        functools.partial(_resid_kernel, n_pairs=len(pairs), k=k, bm=bm, rpb=rpb, n_lat=n_lat),
        grid=(d // bn, t // bm),
        in_specs=in_specs,
        out_specs=pl.BlockSpec((bm, bn), lambda j, i: (i, j)),
        out_shape=jax.ShapeDtypeStruct((t, d), F32),
        compiler_params=_params(),
        name="resid_proj",
    )(*args)


def _fill_v_ext(v_ref, vext_ref):
    width = v_ref.shape[1]
    vext_ref[:, :width] = v_ref[...]
    vext_ref[:, width:] = jnp.ones((v_ref.shape[0], vext_ref.shape[1] - width), vext_ref.dtype)


def _softmax_ext(q, k, v_ext):
    s = _dot_nt(q, k)
    p = jnp.exp2(s - jnp.max(s, axis=-1, keepdims=True))
    return _dot(p.astype(BF16), v_ext)


def _gqa_kernel(q_ref, k_ref, v_ref, prev_ref, o_ref, vext_ref, *, group):
    del prev_ref

    @pl.when(pl.program_id(2) == 0)
    def _():
        _fill_v_ext(v_ref, vext_ref)

    k = k_ref[...]
    for g in range(group):
        oe = _softmax_ext(q_ref[:, g * A_HEAD_DIM:(g + 1) * A_HEAD_DIM], k, vext_ref[...])
        o_ref[:, g * A_HEAD_DIM:(g + 1) * A_HEAD_DIM] = (oe[:, :A_HEAD_DIM] / oe[:, A_HEAD_DIM:]).astype(o_ref.dtype)


def _gqa(q, k, v, att, *, rpb, n_lat, batch, ctx_queries):
    t = q.shape[0]
    kvh = k.shape[1] // A_HEAD_DIM
    gw = A_GROUP * A_HEAD_DIM
    n_ctx = rpb - n_lat
    if ctx_queries:
        tq, nq, lk = n_ctx, 1, n_ctx
        q_row = lambda b, h, i: b * (rpb // n_ctx) + n_lat // n_ctx
        k_row = q_row
    else:
        tq = _tile(math.gcd(n_lat, rpb), 256)
        nq, lk = n_lat // tq, rpb
        q_row = lambda b, h, i: b * (rpb // tq) + i
        k_row = lambda b, h, i: b
    kernel = functools.partial(_gqa_kernel, group=A_GROUP)
    return pl.pallas_call(
        kernel,
        grid=(batch, kvh, nq),
        in_specs=[
            pl.BlockSpec((tq, gw), lambda b, h, i: (q_row(b, h, i), h)),
            pl.BlockSpec((lk, A_HEAD_DIM), lambda b, h, i: (k_row(b, h, i), h)),
            pl.BlockSpec((lk, A_HEAD_DIM), lambda b, h, i: (k_row(b, h, i), h)),
            pl.BlockSpec(memory_space=pl.ANY),
        ],
        out_specs=pl.BlockSpec((tq, gw), lambda b, h, i: (q_row(b, h, i), h)),
        out_shape=jax.ShapeDtypeStruct((t, q.shape[1]), BF16),
        scratch_shapes=[pltpu.VMEM((lk, 2 * A_HEAD_DIM), BF16)],
        input_output_aliases={3: 0},
        compiler_params=_params(),
        name="gqa_ctx" if ctx_queries else "gqa_lat",
    )(q, k, v, att)


def _diff_kernel(lam_ref, g_ref, q_ref, k_ref, v_ref, prev_ref, o_ref, vext_ref, *, lam_init):
    del prev_ref

    @pl.when(pl.program_id(2) == 0)
    def _():
        _fill_v_ext(v_ref, vext_ref)

    lv = lam_ref[...]
    lam = (jnp.exp(jnp.sum(lv[0:1] * lv[1:2], axis=-1, keepdims=True))
           - jnp.exp(jnp.sum(lv[2:3] * lv[3:4], axis=-1, keepdims=True)) + lam_init)
    q = q_ref[...]
    k = k_ref[...]
    hw = q.shape[1]
    lane = lax.broadcasted_iota(jnp.int32, q.shape, 1)
    zero = jnp.zeros_like(q)
    oe1 = _softmax_ext(jnp.where(lane < C_HEAD_DIM, q, zero), k, vext_ref[...])
    oe2 = _softmax_ext(jnp.where(lane >= C_HEAD_DIM, q, zero), k, vext_ref[...])
    o = oe1[:, :hw] / oe1[:, hw:] - lam * (oe2[:, :hw] / oe2[:, hw:])
    o_ref[...] = (_rms(o, g_ref[...]) * (1.0 - lam_init)).astype(o_ref.dtype)


def _diff_attn(qk, v, lamvec, subln_g, out, *, lam_init, rpb, n_lat, batch, ctx_queries):
    t = qk.shape[0]
    heads = v.shape[1] // (2 * C_HEAD_DIM)
    hw = 2 * C_HEAD_DIM
    n_ctx = rpb - n_lat
    if ctx_queries:
        tq, nq, lk = n_ctx, 1, n_ctx
        q_row = lambda b, h, i: b * (rpb // n_ctx) + n_lat // n_ctx
        k_row = q_row
    else:
        tq = _tile(math.gcd(n_lat, rpb), 256)
        nq, lk = n_lat // tq, rpb
        q_row = lambda b, h, i: b * (rpb // tq) + i
        k_row = lambda b, h, i: b
    kernel = functools.partial(_diff_kernel, lam_init=lam_init)
    return pl.pallas_call(
        kernel,
        grid=(batch, heads, nq),
        in_specs=[
            pl.BlockSpec(lamvec.shape, lambda b, h, i: (0, 0)),
            pl.BlockSpec((1, hw), lambda b, h, i: (0, 0)),
            pl.BlockSpec((tq, hw), lambda b, h, i: (q_row(b, h, i), h)),
            pl.BlockSpec((lk, hw), lambda b, h, i: (k_row(b, h, i), heads + h)),
            pl.BlockSpec((lk, hw), lambda b, h, i: (k_row(b, h, i), h)),
            pl.BlockSpec(memory_space=pl.ANY),
        ],
        out_specs=pl.BlockSpec((tq, hw), lambda b, h, i: (q_row(b, h, i), h)),
        out_shape=jax.ShapeDtypeStruct((t, v.shape[1]), BF16),
        scratch_shapes=[pltpu.VMEM((lk, 2 * hw), BF16)],
        input_output_aliases={5: 0},
        compiler_params=_params(),
        name="diff_ctx" if ctx_queries else "diff_lat",
    )(lamvec, subln_g.reshape(1, hw).astype(F32), qk, qk, v, out)


CONV_PAD = SUBLANE


def _block_scan(a, u, carry, reverse):
    row = lax.broadcasted_iota(jnp.int32, a.shape, 0)
    for s in (1, 2, 4):
        shift = SUBLANE - s if reverse else s
        valid = (row < SUBLANE - s) if reverse else (row >= s)
        a_prev = pltpu.roll(a, shift, 0)
        u_prev = pltpu.roll(u, shift, 0)
        u = jnp.where(valid, a * u_prev + u, u)
        a = jnp.where(valid, a * a_prev, a)
    h = u + a * carry
    last = h[0:1] if reverse else h[SUBLANE - 1:SUBLANE]
    return h, jnp.broadcast_to(last, h.shape)


def _rglru_kernel(xb_ref, yb_ref, cw_ref, cb_ref, gw_ref, gb_ref, lam_ref, o_ref,
                  xpad_ref, xc_ref, a_ref, u_ref, hf_ref, hb_ref, *, n_lat, n_ctx):
    cw = cw_ref[...]
    taps = cw.shape[0]
    zpad = jnp.zeros((CONV_PAD, B_BLOCK), F32)

    def conv_segment(r0, n):
        xpad_ref[0:CONV_PAD] = zpad
        xpad_ref[CONV_PAD:CONV_PAD + n] = xb_ref[r0:r0 + n]
        xpad_ref[CONV_PAD + n:2 * CONV_PAD + n] = zpad
        acc = jnp.broadcast_to(cb_ref[...], (n, B_BLOCK))
        for j in range(taps):
            off = CONV_PAD + j - taps // 2
            acc = acc + cw[j:j + 1] * xpad_ref[off:off + n]
        xc_ref[r0:r0 + n] = acc

    conv_segment(0, n_lat)
    conv_segment(n_lat, n_ctx)

    xc = xc_ref[...]
    gates = _dot(xc.astype(BF16), gw_ref[0]) + gb_ref[0]
    lam = lam_ref[...]
    softplus_neg = jnp.maximum(-lam, 0.0) + jnp.log(1.0 + jnp.exp(-jnp.abs(lam)))
    for d in range(2):
        r = jax.nn.sigmoid(gates[:, (2 * d) * B_BLOCK:(2 * d + 1) * B_BLOCK])
        i = jax.nn.sigmoid(gates[:, (2 * d + 1) * B_BLOCK:(2 * d + 2) * B_BLOCK])
        log_a = -LRU_C * r * softplus_neg[d:d + 1]
        a_ref[d] = jnp.exp(log_a)
        u_ref[d] = jnp.sqrt(1.0 - jnp.exp(2.0 * log_a)) * i * xc

    def scan_segment(r0, n, carry):
        nblk = n // SUBLANE

        def body(b, c):
            cf, cb = c
            rf = pl.multiple_of(r0 + b * SUBLANE, SUBLANE)
            rb = pl.multiple_of(r0 + (nblk - 1 - b) * SUBLANE, SUBLANE)
            hf, cf = _block_scan(a_ref[0, pl.ds(rf, SUBLANE), :], u_ref[0, pl.ds(rf, SUBLANE), :], cf, False)
            hb, cb = _block_scan(a_ref[1, pl.ds(rb, SUBLANE), :], u_ref[1, pl.ds(rb, SUBLANE), :], cb, True)
            hf_ref[pl.ds(rf, SUBLANE), :] = hf
            hb_ref[pl.ds(rb, SUBLANE), :] = hb
            return cf, cb

        return lax.fori_loop(0, nblk, body, carry)

    zero = jnp.zeros((SUBLANE, B_BLOCK), F32)
    carry = scan_segment(n_lat, n_ctx, (zero, zero))
    scan_segment(0, n_lat, carry)
    o_ref[...] = ((hf_ref[...] + hb_ref[...]) * _gelu(yb_ref[...])).astype(o_ref.dtype)


def _rglru(xb, yb, conv_w, conv_b, gw, gb, lam, *, rpb, n_lat, batch):
    t, width = xb.shape
    nh = width // B_BLOCK
    n_ctx = rpb - n_lat
    taps = conv_w.shape[0]
    return pl.pallas_call(
        functools.partial(_rglru_kernel, n_lat=n_lat, n_ctx=n_ctx),
        grid=(batch, nh),
        in_specs=[
            pl.BlockSpec((rpb, B_BLOCK), lambda b, h: (b, h)),
            pl.BlockSpec((rpb, B_BLOCK), lambda b, h: (b, h)),
            pl.BlockSpec((taps, B_BLOCK), lambda b, h: (0, h)),
            pl.BlockSpec((1, B_BLOCK), lambda b, h: (0, h)),
            pl.BlockSpec((1, B_BLOCK, 4 * B_BLOCK), lambda b, h: (h, 0, 0)),
            pl.BlockSpec((1, 1, 4 * B_BLOCK), lambda b, h: (h, 0, 0)),
            pl.BlockSpec((2, B_BLOCK), lambda b, h: (0, h)),
        ],
        out_specs=pl.BlockSpec((rpb, B_BLOCK), lambda b, h: (b, h)),
        out_shape=jax.ShapeDtypeStruct((t, width), BF16),
        scratch_shapes=[
            pltpu.VMEM((n_lat + 2 * CONV_PAD, B_BLOCK), F32),
            pltpu.VMEM((rpb, B_BLOCK), F32),
            pltpu.VMEM((2, rpb, B_BLOCK), F32),
            pltpu.VMEM((2, rpb, B_BLOCK), F32),
            pltpu.VMEM((rpb, B_BLOCK), F32),
            pltpu.VMEM((rpb, B_BLOCK), F32),
        ],
        compiler_params=_params(),
        name="rglru",
    )(xb, yb, conv_w, conv_b.reshape(1, width), gw, gb, lam)


CAST_ROWS = 256


def _expert_kernel(te_ref, nu_ref, x_ref, wgu_ref, bgu_ref, wd_ref, bd_ref, o_ref, wgu_bf, wd_bf, *, f):
    j = pl.program_id(0)

    @pl.when((j == 0) | (te_ref[j] != te_ref[jnp.maximum(j - 1, 0)]))
    def _():
        for r in range(0, wgu_bf.shape[0], CAST_ROWS):
            wgu_bf[r:r + CAST_ROWS] = wgu_ref[0, 0, r:r + CAST_ROWS].astype(BF16)
        for r in range(0, wd_bf.shape[0], CAST_ROWS):
            wd_bf[r:r + CAST_ROWS] = wd_ref[0, 0, r:r + CAST_ROWS].astype(BF16)

    @pl.when(j < nu_ref[0])
    def _():
        gu = _dot(x_ref[...], wgu_bf[...]) + bgu_ref[0, 0]
        gate = jnp.minimum(gu[:, :f], SWIGLU_LIMIT)
        up = jnp.clip(gu[:, f:], -SWIGLU_LIMIT, SWIGLU_LIMIT)
        act = (up + 1.0) * gate * jax.nn.sigmoid(gate * SWIGLU_ALPHA)
        o_ref[...] = (_dot(act.astype(BF16), wd_bf[...]) + bd_ref[0, 0]).astype(o_ref.dtype)

    @pl.when(j >= nu_ref[0])
    def _():
        o_ref[...] = jnp.zeros(o_ref.shape, o_ref.dtype)


def _experts(x_sorted, tile_expert, n_used, w_gu, b_gu, w_d, b_d, *, layer, tm):
    r, d = x_sorted.shape
    depth, e, _, f2 = w_gu.shape
    f = f2 // 2
    assert d % CAST_ROWS == 0 and f % CAST_ROWS == 0
    grid_spec = pltpu.PrefetchScalarGridSpec(
        num_scalar_prefetch=2,
        grid=(r // tm,),
        in_specs=[
            pl.BlockSpec((tm, d), lambda j, te, nu: (j, 0)),
            pl.BlockSpec((1, 1, d, f2), lambda j, te, nu: (layer, te[j], 0, 0)),
            pl.BlockSpec((1, 1, 1, f2), lambda j, te, nu: (layer, te[j], 0, 0)),
            pl.BlockSpec((1, 1, f, d), lambda j, te, nu: (layer, te[j], 0, 0)),
            pl.BlockSpec((1, 1, 1, d), lambda j, te, nu: (layer, te[j], 0, 0)),
        ],
        out_specs=pl.BlockSpec((tm, d), lambda j, te, nu: (j, 0)),
        scratch_shapes=[pltpu.VMEM((d, f2), BF16), pltpu.VMEM((f, d), BF16)],
    )
    return pl.pallas_call(
        functools.partial(_expert_kernel, f=f),
        grid_spec=grid_spec,
        out_shape=jax.ShapeDtypeStruct((r, d), BF16),
        compiler_params=_params(),
        name="experts",
    )(tile_expert, n_used, x_sorted, w_gu, b_gu.reshape(depth, e, 1, f2), w_d, b_d.reshape(depth, e, 1, d))


def _combine_kernel(y_ref, wt_ref, x_ref, modb_ref, modc_ref, o_ref, *, k, bm, rpb, n_lat):
    is_ctx = _ctx_rows(pl.program_id(0), bm, rpb, n_lat)
    wt = wt_ref[...]
    y = wt[:, 0:1] * y_ref[0].astype(F32)
    for j in range(1, TOP_K):
        y = y + wt[:, j:j + 1] * y_ref[j].astype(F32)
    o_ref[...] = x_ref[...] + _mod_row(modb_ref, modc_ref, k, is_ctx) * y


def _combine(y4, wt, xs, mod3, *, k, rpb, n_lat, batch):
    t, d = xs.shape
    bm = _tile(rpb, 256)
    return pl.pallas_call(
        functools.partial(_combine_kernel, k=k, bm=bm, rpb=rpb, n_lat=n_lat),
        grid=(t // bm,),
        in_specs=[pl.BlockSpec((TOP_K, bm, d), lambda i: (0, i, 0)), pl.BlockSpec((bm, LANE), lambda i: (i, 0)),
                  pl.BlockSpec((bm, d), lambda i: (i, 0))]
        + _mod_specs(bm, rpb, batch, d),
        out_specs=pl.BlockSpec((bm, d), lambda i: (i, 0)),
        out_shape=jax.ShapeDtypeStruct((t, d), F32),
        compiler_params=_params(),
        name="combine",
    )(y4, wt, xs, mod3, mod3)


def _moe(xs, g, mod3, router_w, router_b, w_gu, b_gu, w_d, b_d, *, layer, rpb, n_lat, batch):
    t, d = xs.shape
    e = router_w.shape[1]
    tm = 512
    h2, idx, wt = _modulate_route(xs, g, mod3, router_w, router_b, k=3, rpb=rpb, n_lat=n_lat, batch=batch)
    top_idx = idx[:, :TOP_K]
    chosen = (top_idx[:, :, None] == jnp.arange(e, dtype=jnp.int32)[None, None, :]).any(axis=1).astype(jnp.int32)
    before = jnp.cumsum(chosen, axis=0) - chosen
    counts = before[-1] + chosen[-1]
    padded = ((counts + tm - 1) // tm) * tm
    group_end = jnp.cumsum(padded)
    group_start = group_end - padded
    dest = group_start[top_idx] + jnp.take_along_axis(before, top_idx, axis=1)
    n_rows = TOP_K * t + e * tm
    n_tiles = n_rows // tm
    tile_expert = jnp.minimum(
        jnp.sum(jnp.arange(n_tiles, dtype=jnp.int32)[:, None] >= (group_end // tm)[None, :], axis=1), e - 1
    ).astype(jnp.int32)
    n_used = (group_end[-1] // tm).astype(jnp.int32).reshape(1)
    row_token = (jnp.arange(n_rows, dtype=jnp.int32) % t).at[dest.reshape(-1)].set(
        jnp.repeat(jnp.arange(t, dtype=jnp.int32), TOP_K), unique_indices=True)
    x_sorted = jnp.take(h2, row_token, axis=0, mode="clip")
    rows = _experts(x_sorted, tile_expert, n_used, w_gu, b_gu, w_d, b_d, layer=layer, tm=tm)
    y4 = jnp.take(rows, dest.T.reshape(-1), axis=0, mode="clip").reshape(TOP_K, t, d)
    return _combine(y4, wt, xs, mod3, k=5, rpb=rpb, n_lat=n_lat, batch=batch)


def _final_kernel(x_ref, g_ref, o_ref):
    o_ref[0] = _rms(x_ref[0], g_ref[...])


def _final_norm(xs3, g, n_lat):
    batch, rpb, d = xs3.shape
    bm = _tile(n_lat, 256)
    return pl.pallas_call(
        _final_kernel,
        grid=(batch, n_lat // bm),
        in_specs=[pl.BlockSpec((1, bm, d), lambda b, i: (b, i, 0)), pl.BlockSpec((1, d), lambda b, i: (0, 0))],
        out_specs=pl.BlockSpec((1, bm, d), lambda b, i: (b, i, 0)),
        out_shape=jax.ShapeDtypeStruct((batch, n_lat, d), F32),
        compiler_params=_params(),
        name="final_norm",
    )(xs3, g.reshape(1, d))


def _rope_tables(n_lat, n_ctx, head_dim):
    n_freq = head_dim // 4
    inv_freq = ROPE_THETA ** (-jnp.arange(n_freq, dtype=F32) / n_freq)
    tok = jnp.arange(n_lat, dtype=jnp.int32)
    ang_r = (tok // GRID_W).astype(F32)[:, None] * inv_freq
    ang_c = (tok % GRID_W).astype(F32)[:, None] * inv_freq
    cos = jnp.concatenate([jnp.cos(ang_r), jnp.cos(ang_r), jnp.cos(ang_c), jnp.cos(ang_c)], axis=-1)
    sin = jnp.concatenate([-jnp.sin(ang_r), jnp.sin(ang_r), -jnp.sin(ang_c), jnp.sin(ang_c)], axis=-1)
    reps = LANE // head_dim
    cos = jnp.concatenate([jnp.tile(cos, (1, reps)), jnp.ones((n_ctx, LANE), F32)], axis=0)
    sin = jnp.concatenate([jnp.tile(sin, (1, reps)), jnp.zeros((n_ctx, LANE), F32)], axis=0)
    return cos, sin, n_freq


def kernel(x, c, ctx, c_ctx, mod_w, mod_b, norm1_g, norm2_g, hyb_w_in, hyb_w_out, a_q_norm_g, a_k_norm_g, b_conv_w, b_conv_b, b_gate_a_w, b_gate_a_b, b_gate_x_w, b_gate_x_b, b_lru_lambda, c_w_in, c_w_out, c_lambda_q1, c_lambda_k1, c_lambda_q2, c_lambda_k2, c_subln_g, router_w, router_b, exp_w_gu, exp_b_gu, exp_w_d, exp_b_d, final_g):
    batch, n_lat, d = x.shape
    n_ctx = ctx.shape[1]
    rpb = n_lat + n_ctx
    t = batch * rpb
    depth = mod_w.shape[0]
    geo = dict(rpb=rpb, n_lat=n_lat, batch=batch)

    xs = jnp.concatenate([x, ctx], axis=1).reshape(t, d)
    mod_rows = -(-(batch + 1) // SUBLANE) * SUBLANE
    cvec = jnp.zeros((mod_rows, d), F32).at[:batch].set(c).at[batch].set(c_ctx)
    mod = _modmat(cvec, mod_w, mod_b)

    rope_a = _rope_tables(n_lat, n_ctx, A_HEAD_DIM)
    rope_c = _rope_tables(n_lat, n_ctx, C_HEAD_DIM)
    q_w = A_GROUP * A_HEAD_DIM * (hyb_w_in.shape[2] - 2 * (d // 2)) // ((A_GROUP + 2) * A_HEAD_DIM)
    kv_w = q_w // A_GROUP
    b_w = d // 2
    nh = b_w // B_BLOCK

    for l in range(depth):
        i = l // 2
        mod3 = mod[l].reshape(mod_rows, 6, d)
        h = _modulate(xs, norm1_g[l], mod3, k=0, **geo)
        if l % 2 == 0:
            w_in = hyb_w_in[i].astype(BF16)
            o = 0
            q = _proj(h, w_in[:, o:o + q_w], rpb=rpb, out_dtype=BF16, norm_g=a_q_norm_g[i], rope=rope_a,
                      q_cols=q_w, q_scale=A_HEAD_DIM ** -0.5 * LOG2E)
            o += q_w
            kk = _proj(h, w_in[:, o:o + kv_w], rpb=rpb, out_dtype=BF16, norm_g=a_k_norm_g[i], rope=rope_a)
            o += kv_w
            v = _proj(h, w_in[:, o:o + kv_w], rpb=rpb, out_dtype=BF16)
            o += kv_w
            xb = _proj(h, w_in[:, o:o + b_w], rpb=rpb, out_dtype=F32)
            o += b_w
            yb = _proj(h, w_in[:, o:o + b_w], rpb=rpb, out_dtype=F32)
            att = jnp.zeros((t, q_w), BF16)
            att = _gqa(q, kk, v, att, ctx_queries=False, **geo)
            att = _gqa(q, kk, v, att, ctx_queries=True, **geo)
            gw = jnp.concatenate([b_gate_a_w[i, 0], b_gate_x_w[i, 0], b_gate_a_w[i, 1], b_gate_x_w[i, 1]],
                                 axis=-1).astype(BF16)
            gb = jnp.concatenate([b_gate_a_b[i, 0].reshape(nh, 1, B_BLOCK), b_gate_x_b[i, 0].reshape(nh, 1, B_BLOCK),
                                  b_gate_a_b[i, 1].reshape(nh, 1, B_BLOCK), b_gate_x_b[i, 1].reshape(nh, 1, B_BLOCK)],
                                 axis=-1)
            ry = _rglru(xb, yb, b_conv_w[i], b_conv_b[i], gw, gb, b_lru_lambda[i], **geo)
            w_out = hyb_w_out[i].astype(BF16)
            pairs = [(att, w_out[:q_w]), (ry, w_out[q_w:])]
        else:
            lam_init = 0.8 - 0.6 * math.exp(-0.3 * l)
            w_in = c_w_in[i].astype(BF16)
            qk_w = 2 * (w_in.shape[1] // 3)
            qk = _proj(h, w_in[:, :qk_w], rpb=rpb, out_dtype=BF16, rope=rope_c,
                       q_cols=qk_w // 2, q_scale=C_HEAD_DIM ** -0.5 * LOG2E)
            v = _proj(h, w_in[:, qk_w:], rpb=rpb, out_dtype=BF16)
            lamvec = jnp.stack([c_lambda_q1[i], c_lambda_k1[i], c_lambda_q2[i], c_lambda_k2[i]]).astype(F32)
            att = jnp.zeros((t, v.shape[1]), BF16)
            att = _diff_attn(qk, v, lamvec, c_subln_g[i], att, lam_init=lam_init, ctx_queries=False, **geo)
            att = _diff_attn(qk, v, lamvec, c_subln_g[i], att, lam_init=lam_init, ctx_queries=True, **geo)
            pairs = [(att, c_w_out[i].astype(BF16))]
        xs = _resid_proj(xs, mod3, pairs, k=2, **geo)
        xs = _moe(xs, norm2_g[l], mod3, router_w[l], router_b[l], exp_w_gu, exp_b_gu, exp_w_d, exp_b_d,
                  layer=l, **geo)
    return _final_norm(xs.reshape(batch, rpb, d), final_g, n_lat)
```

```python
import functools
import math

import jax
import jax.numpy as jnp
from jax import lax
from jax.experimental import pallas as pl
from jax.experimental.pallas import tpu as pltpu

F32 = jnp.float32
BF16 = jnp.bfloat16

GRID_W = 64
ROPE_THETA = 10000.0
NORM_EPS = 1e-6
A_HEAD_DIM = 128
A_GROUP = 4
B_BLOCK = 128
LRU_C = 8.0
C_HEAD_DIM = 64
TOP_K = 4
SWIGLU_LIMIT = 7.0
SWIGLU_ALPHA = 1.702
LOG2E = math.log2(math.e)

LANE = 128
SUBLANE = 8
VMEM_LIMIT_BYTES = 56 * 1024 * 1024
ATTN_Q_ROWS = 512
ATTN_KEY_CHUNK = 256
MOE_ROW_TILE = 512
MOE_TOKEN_SPLITS = 2


def _params():
    return pltpu.CompilerParams(vmem_limit_bytes=VMEM_LIMIT_BYTES)


def _tile(n, target, mult=SUBLANE):
    best = None
    for t in range(mult, min(n, target) + 1, mult):
        if n % t == 0:
            best = t
    assert best is not None, (n, target, mult)
    return best


def _ctx_rows(i, bm, rpb, n_lat):
    pos0 = lax.rem(i * bm, rpb)
    row = lax.broadcasted_iota(jnp.int32, (bm, 1), 0)
    return (pos0 + row) >= n_lat


def _mod_row(modb_ref, modc_ref, k, is_ctx):
    return jnp.where(is_ctx, modc_ref[0, k:k + 1, :], modb_ref[0, k:k + 1, :])


def _rms(x, g):
    return x * lax.rsqrt(jnp.mean(x * x, axis=-1, keepdims=True) + NORM_EPS) * g


def _dot(a, b):
    return jnp.dot(a, b, preferred_element_type=F32)


def _dot_nt(a, b):
    return lax.dot_general(a, b, (((1,), (1,)), ((), ())), preferred_element_type=F32)


def _modmat_kernel(c_ref, w_ref, b_ref, o_ref):
    c = c_ref[...]
    s = (c * jax.nn.sigmoid(c)).astype(BF16)
    o_ref[0] = _dot(s, w_ref[0].astype(BF16)) + b_ref[0]


def _modmat(cvec, mod_w, mod_b):
    depth, d, n = mod_w.shape
    r = cvec.shape[0]
    bn = _tile(n, 512, LANE)
    return pl.pallas_call(
        _modmat_kernel,
        grid=(depth, n // bn),
        in_specs=[
            pl.BlockSpec((r, d), lambda l, j: (0, 0)),
            pl.BlockSpec((1, d, bn), lambda l, j: (l, 0, j)),
            pl.BlockSpec((1, 1, bn), lambda l, j: (l, 0, j)),
        ],
        out_specs=pl.BlockSpec((1, r, bn), lambda l, j: (l, 0, j)),
        out_shape=jax.ShapeDtypeStruct((depth, r, n), F32),
        compiler_params=_params(),
        name="modmat",
    )(cvec, mod_w, mod_b.reshape(depth, 1, n))


def _modulate_kernel(x_ref, g_ref, modb_ref, modc_ref, h_ref, *, k, bm, rpb, n_lat):
    is_ctx = _ctx_rows(pl.program_id(0), bm, rpb, n_lat)
    y = _rms(x_ref[...], g_ref[...])
    h = y * (1.0 + _mod_row(modb_ref, modc_ref, k + 1, is_ctx)) + _mod_row(modb_ref, modc_ref, k, is_ctx)
    h_ref[...] = h.astype(BF16)


def _mod_specs(bm, rpb, batch, d):
    return [
        pl.BlockSpec((1, 6, d), lambda i: ((i * bm) // rpb, 0, 0)),
        pl.BlockSpec((1, 6, d), lambda i: (batch, 0, 0)),
    ]


def _modulate(xs, g, mod3, *, k, rpb, n_lat, batch):
    t, d = xs.shape
    bm = _tile(rpb, 256)
    return pl.pallas_call(
        functools.partial(_modulate_kernel, k=k, bm=bm, rpb=rpb, n_lat=n_lat),
        grid=(t // bm,),
        in_specs=[pl.BlockSpec((bm, d), lambda i: (i, 0)), pl.BlockSpec((1, d), lambda i: (0, 0))]
        + _mod_specs(bm, rpb, batch, d),
        out_specs=pl.BlockSpec((bm, d), lambda i: (i, 0)),
        out_shape=jax.ShapeDtypeStruct((t, d), BF16),
        compiler_params=_params(),
        name="modulate",
    )(xs, g.reshape(1, d), mod3, mod3)


def _router_kernel(x_ref, g_ref, modb_ref, modc_ref, rwh_ref, rwl_ref, rb_ref, h_ref, idx_ref, wt_ref,
                   *, k, bm, rpb, n_lat):
    is_ctx = _ctx_rows(pl.program_id(0), bm, rpb, n_lat)
    y = _rms(x_ref[...], g_ref[...])
    h = y * (1.0 + _mod_row(modb_ref, modc_ref, k + 1, is_ctx)) + _mod_row(modb_ref, modc_ref, k, is_ctx)
    h_hi = h.astype(BF16)
    h_lo = (h - h_hi.astype(F32)).astype(BF16)
    h_ref[...] = h_hi
    logits = _dot(h_hi, rwh_ref[...]) + (_dot(h_lo, rwh_ref[...]) + _dot(h_hi, rwl_ref[...])) + rb_ref[...]
    lane = lax.broadcasted_iota(jnp.int32, logits.shape, 1)
    vals, idxs = [], []
    for _ in range(TOP_K):
        m = jnp.max(logits, axis=-1, keepdims=True)
        ik = jnp.min(jnp.where(logits == m, lane, LANE), axis=-1, keepdims=True)
        vals.append(m)
        idxs.append(ik)
        logits = jnp.where(lane == ik, -jnp.inf, logits)
    es = [jnp.exp(v - vals[0]) for v in vals]
    den = es[0] + es[1] + es[2] + es[3]
    idx_out = jnp.zeros(lane.shape, jnp.int32)
    wt_out = jnp.zeros(lane.shape, F32)
    for j in range(TOP_K):
        idx_out = jnp.where(lane == j, idxs[j], idx_out)
        wt_out = jnp.where(lane == j, es[j] / den, wt_out)
    idx_ref[...] = idx_out
    wt_ref[...] = wt_out


def _modulate_route(xs, g, mod3, router_w, router_b, *, k, rpb, n_lat, batch):
    t, d = xs.shape
    e = router_w.shape[1]
    bm = _tile(rpb, 256)
    rw = jnp.zeros((d, LANE), F32).at[:, :e].set(router_w)
    rw_hi = rw.astype(BF16)
    rw_lo = (rw - rw_hi.astype(F32)).astype(BF16)
    rb = jnp.full((1, LANE), -1e30, F32).at[0, :e].set(router_b)
    return pl.pallas_call(
        functools.partial(_router_kernel, k=k, bm=bm, rpb=rpb, n_lat=n_lat),
        grid=(t // bm,),
        in_specs=[pl.BlockSpec((bm, d), lambda i: (i, 0)), pl.BlockSpec((1, d), lambda i: (0, 0))]
        + _mod_specs(bm, rpb, batch, d)
        + [pl.BlockSpec((d, LANE), lambda i: (0, 0))] * 2 + [pl.BlockSpec((1, LANE), lambda i: (0, 0))],
        out_specs=[pl.BlockSpec((bm, d), lambda i: (i, 0)),
                   pl.BlockSpec((bm, LANE), lambda i: (i, 0)),
                   pl.BlockSpec((bm, LANE), lambda i: (i, 0))],
        out_shape=[jax.ShapeDtypeStruct((t, d), BF16),
                   jax.ShapeDtypeStruct((t, LANE), jnp.int32),
                   jax.ShapeDtypeStruct((t, LANE), F32)],
        compiler_params=_params(),
        name="modulate_route",
    )(xs, g.reshape(1, d), mod3, mod3, rw_hi, rw_lo, rb)


def _gelu(x):
    return 0.5 * x * (1.0 + jnp.tanh(math.sqrt(2.0 / math.pi) * (x + 0.044715 * (x * x * x))))


def _proj_kernel(*refs, norm, rope_shift, q_cols, q_scale, bn):
    a_ref, w_ref = refs[0], refs[1]
    o_ref = refs[-1]
    pos = 2
    g_ref = cos_ref = sin_ref = None
    if norm:
        g_ref = refs[pos]
        pos += 1
    if rope_shift:
        cos_ref, sin_ref = refs[pos], refs[pos + 1]
    acc = _dot(a_ref[...], w_ref[...])
    if not norm and not rope_shift:
        o_ref[...] = acc.astype(o_ref.dtype)
        return
    col_scale = jnp.where(pl.program_id(0) * bn < q_cols, q_scale, 1.0)
    lane = lax.broadcasted_iota(jnp.int32, (acc.shape[0], LANE), 1)
    first_half = jnp.bitwise_and(lane, 2 * rope_shift - 1) < rope_shift
    for c in range(bn // LANE):
        y = acc[:, c * LANE:(c + 1) * LANE]
        if norm:
            y = _rms(y, g_ref[...])
        partner = jnp.where(first_half, pltpu.roll(y, LANE - rope_shift, 1), pltpu.roll(y, rope_shift, 1))
        y = (y * cos_ref[...] + partner * sin_ref[...]) * col_scale
        o_ref[:, c * LANE:(c + 1) * LANE] = y.astype(o_ref.dtype)


def _proj(a, w, *, rpb, out_dtype, norm_g=None, rope=None, q_cols=0, q_scale=1.0):
    t, kdim = a.shape
    n = w.shape[1]
    bm = _tile(rpb, 1152, 16)
    bn = _tile(math.gcd(n, q_cols) if q_cols else n, 1024, LANE)
    tiles_per_sample = rpb // bm
    in_specs = [pl.BlockSpec((bm, kdim), lambda j, i: (i, 0)), pl.BlockSpec((kdim, bn), lambda j, i: (0, j))]
    args = [a, w]
    if norm_g is not None:
        in_specs.append(pl.BlockSpec((1, LANE), lambda j, i: (0, 0)))
        args.append(norm_g.reshape(1, LANE).astype(F32))
    rope_shift = 0
    if rope is not None:
        cos, sin, rope_shift = rope
        in_specs += [pl.BlockSpec((bm, LANE), lambda j, i: (i % tiles_per_sample, 0))] * 2
        args += [cos, sin]
    return pl.pallas_call(
        functools.partial(_proj_kernel, norm=norm_g is not None, rope_shift=rope_shift, q_cols=q_cols,
                          q_scale=q_scale, bn=bn),
        grid=(n // bn, t // bm),
        in_specs=in_specs,
        out_specs=pl.BlockSpec((bm, bn), lambda j, i: (i, j)),
        out_shape=jax.ShapeDtypeStruct((t, n), out_dtype),
        compiler_params=_params(),
        name="proj",
    )(*args)


def _resid_kernel(*refs, n_pairs, k, bm, rpb, n_lat):
    x_ref, modb_ref, modc_ref = refs[0], refs[1], refs[2]
    o_ref = refs[-1]
    acc = None
    for p in range(n_pairs):
        part = _dot(refs[3 + 2 * p][...], refs[4 + 2 * p][...])
        acc = part if acc is None else acc + part
    is_ctx = _ctx_rows(pl.program_id(1), bm, rpb, n_lat)
    o_ref[...] = x_ref[...] + _mod_row(modb_ref, modc_ref, k, is_ctx) * acc


def _resid_proj(xs, mod3, pairs, *, k, rpb, n_lat, batch):
    t, d = xs.shape
    bm = _tile(rpb, 768, 16)
    bn = _tile(d, 1024, LANE)
    in_specs = [
        pl.BlockSpec((bm, bn), lambda j, i: (i, j)),
        pl.BlockSpec((1, 6, bn), lambda j, i: ((i * bm) // rpb, 0, j)),
        pl.BlockSpec((1, 6, bn), lambda j, i: (batch, 0, j)),
    ]
    args = [xs, mod3, mod3]
    for a, w in pairs:
        kdim = a.shape[1]
        in_specs += [pl.BlockSpec((bm, kdim), lambda j, i: (i, 0)), pl.BlockSpec((kdim, bn), lambda j, i: (0, j))]
        args += [a, w]
    return pl.pallas_call(
        functools.partial(_resid_kernel, n_pairs=len(pairs), k=k, bm=bm, rpb=rpb, n_lat=n_lat),
        grid=(d // bn, t // bm),
        in_specs=in_specs,
        out_specs=pl.BlockSpec((bm, bn), lambda j, i: (i, j)),
        out_shape=jax.ShapeDtypeStruct((t, d), F32),
        compiler_params=_params(),
        name="resid_proj",
    )(*args)


def _fill_v_ext(v_ref, vext_ref):
    width = v_ref.shape[1]
    vext_ref[:, :width] = v_ref[...]
    vext_ref[:, width:] = jnp.ones((v_ref.shape[0], vext_ref.shape[1] - width), vext_ref.dtype)


def _attend(qs, k_ref, vext_ref, s_refs, p_refs):
    tq = qs[0].shape[0]
    lk = k_ref.shape[0]
    kc = _tile(lk, ATTN_KEY_CHUNK, LANE)

    def scores(q, c, s_ref, lane_max):
        s = _dot_nt(q, k_ref[c * kc:(c + 1) * kc])
        s_ref[:, c * kc:(c + 1) * kc] = s
        for j in range(kc // LANE):
            lane_max = jnp.maximum(lane_max, s[:, j * LANE:(j + 1) * LANE])
        return lane_max

    def exponentials(s_ref, p_ref, c, row_max):
        p_ref[:, c * kc:(c + 1) * kc] = jnp.exp2(s_ref[:, c * kc:(c + 1) * kc] - row_max).astype(BF16)

    lowest = jnp.full((tq, LANE), -jnp.inf, F32)
    lane_max = lowest
    for c in range(lk // kc):
        lane_max = scores(qs[0], c, s_refs[0], lane_max)
    row_max = jnp.max(lane_max, axis=-1, keepdims=True)
    for g in range(len(qs)):
        cur, nxt = g % 2, (g + 1) % 2
        lane_max = lowest
        for c in range(lk // kc):
            if g + 1 < len(qs):
                lane_max = scores(qs[g + 1], c, s_refs[nxt], lane_max)
            exponentials(s_refs[cur], p_refs[cur], c, row_max)
        if g + 1 < len(qs):
            row_max = jnp.max(lane_max, axis=-1, keepdims=True)
        yield _dot(p_refs[cur][...], vext_ref[...])


def _attn_geometry(rpb, n_lat, ctx_queries):
    n_ctx = rpb - n_lat
    if ctx_queries:
        assert n_lat % n_ctx == 0
        return n_ctx, 1, n_ctx, (lambda i: n_lat // n_ctx), n_lat // n_ctx
    tq = _tile(n_lat, ATTN_Q_ROWS)
    return tq, n_lat // tq, rpb, (lambda i: i), 0


def _attn_scratch(tq, lk, width):
    return [pltpu.VMEM((lk, 2 * width), BF16),
            pltpu.VMEM((tq, lk), F32), pltpu.VMEM((tq, lk), F32),
            pltpu.VMEM((tq, lk), BF16), pltpu.VMEM((tq, lk), BF16)]


def _gqa_kernel(q_ref, k_ref, v_ref, prev_ref, o_ref, vext_ref, s0_ref, s1_ref, p0_ref, p1_ref, *, group):
    del prev_ref

    @pl.when(pl.program_id(2) == 0)
    def _():
        _fill_v_ext(v_ref, vext_ref)

    qs = [q_ref[:, g * A_HEAD_DIM:(g + 1) * A_HEAD_DIM] for g in range(group)]
    for g, oe in enumerate(_attend(qs, k_ref, vext_ref, (s0_ref, s1_ref), (p0_ref, p1_ref))):
        o_ref[:, g * A_HEAD_DIM:(g + 1) * A_HEAD_DIM] = (oe[:, :A_HEAD_DIM] / oe[:, A_HEAD_DIM:]).astype(o_ref.dtype)


def _gqa(q, k, v, att, *, rpb, n_lat, batch, ctx_queries):
    t = q.shape[0]
    kvh = k.shape[1] // A_HEAD_DIM
    gw = A_GROUP * A_HEAD_DIM
    tq, nq, lk, q_blk, k_blk = _attn_geometry(rpb, n_lat, ctx_queries)
    per_sample = lambda a: a.reshape(batch, rpb, a.shape[1])
    kernel = functools.partial(_gqa_kernel, group=A_GROUP)
    return pl.pallas_call(
        kernel,
        grid=(batch, kvh, nq),
        in_specs=[
            pl.BlockSpec((None, tq, gw), lambda b, h, i: (b, q_blk(i), h)),
            pl.BlockSpec((None, lk, A_HEAD_DIM), lambda b, h, i: (b, k_blk, h)),
            pl.BlockSpec((None, lk, A_HEAD_DIM), lambda b, h, i: (b, k_blk, h)),
            pl.BlockSpec(memory_space=pl.ANY),
        ],
        out_specs=pl.BlockSpec((None, tq, gw), lambda b, h, i: (b, q_blk(i), h)),
        out_shape=jax.ShapeDtypeStruct((batch, rpb, q.shape[1]), BF16),
        scratch_shapes=_attn_scratch(tq, lk, A_HEAD_DIM),
        input_output_aliases={3: 0},
        compiler_params=_params(),
        name="gqa_ctx" if ctx_queries else "gqa_lat",
    )(per_sample(q), per_sample(k), per_sample(v), per_sample(att)).reshape(t, q.shape[1])


def _diff_kernel(lam_ref, g_ref, q_ref, k_ref, v_ref, prev_ref, o_ref, vext_ref, s0_ref, s1_ref, p0_ref, p1_ref,
                 *, lam_init):
    del prev_ref

    @pl.when(pl.program_id(2) == 0)
    def _():
        _fill_v_ext(v_ref, vext_ref)

    lv = lam_ref[...]
    lam = (jnp.exp(jnp.sum(lv[0:1] * lv[1:2], axis=-1, keepdims=True))
           - jnp.exp(jnp.sum(lv[2:3] * lv[3:4], axis=-1, keepdims=True)) + lam_init)
    q = q_ref[...]
    hw = q.shape[1]
    lane = lax.broadcasted_iota(jnp.int32, q.shape, 1)
    zero = jnp.zeros_like(q)
    qs = [jnp.where(lane < C_HEAD_DIM, q, zero), jnp.where(lane >= C_HEAD_DIM, q, zero)]
    oe1, oe2 = _attend(qs, k_ref, vext_ref, (s0_ref, s1_ref), (p0_ref, p1_ref))
    o = oe1[:, :hw] / oe1[:, hw:] - lam * (oe2[:, :hw] / oe2[:, hw:])
    o_ref[...] = (_rms(o, g_ref[...]) * (1.0 - lam_init)).astype(o_ref.dtype)


def _diff_attn(qk, v, lamvec, subln_g, out, *, lam_init, rpb, n_lat, batch, ctx_queries):
    t = qk.shape[0]
    heads = v.shape[1] // (2 * C_HEAD_DIM)
    hw = 2 * C_HEAD_DIM
    tq, nq, lk, q_blk, k_blk = _attn_geometry(rpb, n_lat, ctx_queries)
    per_sample = lambda a: a.reshape(batch, rpb, a.shape[1])
    kernel = functools.partial(_diff_kernel, lam_init=lam_init)
    return pl.pallas_call(
        kernel,
        grid=(batch, heads, nq),
        in_specs=[
            pl.BlockSpec(lamvec.shape, lambda b, h, i: (0, 0)),
            pl.BlockSpec((1, hw), lambda b, h, i: (0, 0)),
            pl.BlockSpec((None, tq, hw), lambda b, h, i: (b, q_blk(i), h)),
            pl.BlockSpec((None, lk, hw), lambda b, h, i: (b, k_blk, heads + h)),
            pl.BlockSpec((None, lk, hw), lambda b, h, i: (b, k_blk, h)),
            pl.BlockSpec(memory_space=pl.ANY),
        ],
        out_specs=pl.BlockSpec((None, tq, hw), lambda b, h, i: (b, q_blk(i), h)),
        out_shape=jax.ShapeDtypeStruct((batch, rpb, v.shape[1]), BF16),
        scratch_shapes=_attn_scratch(tq, lk, hw),
        input_output_aliases={5: 0},
        compiler_params=_params(),
        name="diff_ctx" if ctx_queries else "diff_lat",
    )(lamvec, subln_g.reshape(1, hw).astype(F32), per_sample(qk), per_sample(qk), per_sample(v),
      per_sample(out)).reshape(t, v.shape[1])


CONV_PAD = SUBLANE


def _block_scan(a, u, carry, reverse):
    row = lax.broadcasted_iota(jnp.int32, a.shape, 0)
    for s in (1, 2, 4):
        shift = SUBLANE - s if reverse else s
        valid = (row < SUBLANE - s) if reverse else (row >= s)
        a_prev = pltpu.roll(a, shift, 0)
        u_prev = pltpu.roll(u, shift, 0)
        u = jnp.where(valid, a * u_prev + u, u)
        a = jnp.where(valid, a * a_prev, a)
    h = u + a * carry
    last = h[0:1] if reverse else h[SUBLANE - 1:SUBLANE]
    return h, jnp.broadcast_to(last, h.shape)


def _rglru_kernel(xb_ref, yb_ref, cw_ref, cb_ref, gw_ref, gb_ref, lam_ref, o_ref,
                  xpad_ref, xc_ref, a_ref, u_ref, hf_ref, hb_ref, *, n_lat, n_ctx):
    cw = cw_ref[...]
    taps = cw.shape[0]
    zpad = jnp.zeros((CONV_PAD, B_BLOCK), F32)

    def conv_segment(r0, n):
        xpad_ref[0:CONV_PAD] = zpad
        xpad_ref[CONV_PAD:CONV_PAD + n] = xb_ref[r0:r0 + n]
        xpad_ref[CONV_PAD + n:2 * CONV_PAD + n] = zpad
        acc = jnp.broadcast_to(cb_ref[...], (n, B_BLOCK))
        for j in range(taps):
            off = CONV_PAD + j - taps // 2
            acc = acc + cw[j:j + 1] * xpad_ref[off:off + n]
        xc_ref[r0:r0 + n] = acc

    conv_segment(0, n_lat)
    conv_segment(n_lat, n_ctx)

    xc = xc_ref[...]
    gates = _dot(xc.astype(BF16), gw_ref[0]) + gb_ref[0]
    lam = lam_ref[...]
    softplus_neg = jnp.maximum(-lam, 0.0) + jnp.log(1.0 + jnp.exp(-jnp.abs(lam)))
    for d in range(2):
        r = jax.nn.sigmoid(gates[:, (2 * d) * B_BLOCK:(2 * d + 1) * B_BLOCK])
        i = jax.nn.sigmoid(gates[:, (2 * d + 1) * B_BLOCK:(2 * d + 2) * B_BLOCK])
        log_a = -LRU_C * r * softplus_neg[d:d + 1]
        a_ref[d] = jnp.exp(log_a)
        u_ref[d] = jnp.sqrt(1.0 - jnp.exp(2.0 * log_a)) * i * xc

    def scan_segment(r0, n, carry):
        nblk = n // SUBLANE

        def body(b, c):
            cf, cb = c
            rf = pl.multiple_of(r0 + b * SUBLANE, SUBLANE)
            rb = pl.multiple_of(r0 + (nblk - 1 - b) * SUBLANE, SUBLANE)
            hf, cf = _block_scan(a_ref[0, pl.ds(rf, SUBLANE), :], u_ref[0, pl.ds(rf, SUBLANE), :], cf, False)
            hb, cb = _block_scan(a_ref[1, pl.ds(rb, SUBLANE), :], u_ref[1, pl.ds(rb, SUBLANE), :], cb, True)
            hf_ref[pl.ds(rf, SUBLANE), :] = hf
            hb_ref[pl.ds(rb, SUBLANE), :] = hb
            return cf, cb

        return lax.fori_loop(0, nblk, body, carry)

    zero = jnp.zeros((SUBLANE, B_BLOCK), F32)
    carry = scan_segment(n_lat, n_ctx, (zero, zero))
    scan_segment(0, n_lat, carry)
    o_ref[...] = ((hf_ref[...] + hb_ref[...]) * _gelu(yb_ref[...])).astype(o_ref.dtype)


def _rglru(xb, yb, conv_w, conv_b, gw, gb, lam, *, rpb, n_lat, batch):
    t, width = xb.shape
    nh = width // B_BLOCK
    n_ctx = rpb - n_lat
    taps = conv_w.shape[0]
    return pl.pallas_call(
        functools.partial(_rglru_kernel, n_lat=n_lat, n_ctx=n_ctx),
        grid=(batch, nh),
        in_specs=[
            pl.BlockSpec((rpb, B_BLOCK), lambda b, h: (b, h)),
            pl.BlockSpec((rpb, B_BLOCK), lambda b, h: (b, h)),
            pl.BlockSpec((taps, B_BLOCK), lambda b, h: (0, h)),
            pl.BlockSpec((1, B_BLOCK), lambda b, h: (0, h)),
            pl.BlockSpec((1, B_BLOCK, 4 * B_BLOCK), lambda b, h: (h, 0, 0)),
            pl.BlockSpec((1, 1, 4 * B_BLOCK), lambda b, h: (h, 0, 0)),
            pl.BlockSpec((2, B_BLOCK), lambda b, h: (0, h)),
        ],
        out_specs=pl.BlockSpec((rpb, B_BLOCK), lambda b, h: (b, h)),
        out_shape=jax.ShapeDtypeStruct((t, width), BF16),
        scratch_shapes=[
            pltpu.VMEM((n_lat + 2 * CONV_PAD, B_BLOCK), F32),
            pltpu.VMEM((rpb, B_BLOCK), F32),
            pltpu.VMEM((2, rpb, B_BLOCK), F32),
            pltpu.VMEM((2, rpb, B_BLOCK), F32),
            pltpu.VMEM((rpb, B_BLOCK), F32),
            pltpu.VMEM((rpb, B_BLOCK), F32),
        ],
        compiler_params=_params(),
        name="rglru",
    )(xb, yb, conv_w, conv_b.reshape(1, width), gw, gb, lam)


CAST_ROWS = 256


def _expert_kernel(te_ref, nu_ref, x_ref, wgu_ref, bgu_ref, wd_ref, bd_ref, o_ref, wgu_bf, wd_bf, *, f):
    j = pl.program_id(0)

    @pl.when((j == 0) | (te_ref[j] != te_ref[jnp.maximum(j - 1, 0)]))
    def _():
        for r in range(0, wgu_bf.shape[0], CAST_ROWS):
            wgu_bf[r:r + CAST_ROWS] = wgu_ref[0, 0, r:r + CAST_ROWS].astype(BF16)
        for r in range(0, wd_bf.shape[0], CAST_ROWS):
            wd_bf[r:r + CAST_ROWS] = wd_ref[0, 0, r:r + CAST_ROWS].astype(BF16)

    @pl.when(j < nu_ref[0])
    def _():
        gu = _dot(x_ref[...], wgu_bf[...]) + bgu_ref[0, 0]
        gate = jnp.minimum(gu[:, :f], SWIGLU_LIMIT)
        up = jnp.clip(gu[:, f:], -SWIGLU_LIMIT, SWIGLU_LIMIT)
        act = (up + 1.0) * gate * jax.nn.sigmoid(gate * SWIGLU_ALPHA)
        o_ref[...] = (_dot(act.astype(BF16), wd_bf[...]) + bd_ref[0, 0]).astype(o_ref.dtype)

    @pl.when(j >= nu_ref[0])
    def _():
        o_ref[...] = jnp.zeros(o_ref.shape, o_ref.dtype)


def _experts(x_sorted, tile_expert, n_used, w_gu, b_gu, w_d, b_d, *, layer, tm):
    r, d = x_sorted.shape
    depth, e, _, f2 = w_gu.shape
    f = f2 // 2
    assert d % CAST_ROWS == 0 and f % CAST_ROWS == 0
    grid_spec = pltpu.PrefetchScalarGridSpec(
        num_scalar_prefetch=2,
        grid=(r // tm,),
        in_specs=[
            pl.BlockSpec((tm, d), lambda j, te, nu: (j, 0)),
            pl.BlockSpec((1, 1, d, f2), lambda j, te, nu: (layer, te[j], 0, 0)),
            pl.BlockSpec((1, 1, 1, f2), lambda j, te, nu: (layer, te[j], 0, 0)),
            pl.BlockSpec((1, 1, f, d), lambda j, te, nu: (layer, te[j], 0, 0)),
            pl.BlockSpec((1, 1, 1, d), lambda j, te, nu: (layer, te[j], 0, 0)),
        ],
        out_specs=pl.BlockSpec((tm, d), lambda j, te, nu: (j, 0)),
        scratch_shapes=[pltpu.VMEM((d, f2), BF16), pltpu.VMEM((f, d), BF16)],
    )
    return pl.pallas_call(
        functools.partial(_expert_kernel, f=f),
        grid_spec=grid_spec,
        out_shape=jax.ShapeDtypeStruct((r, d), BF16),
        compiler_params=_params(),
        name="experts",
    )(tile_expert, n_used, x_sorted, w_gu, b_gu.reshape(depth, e, 1, f2), w_d, b_d.reshape(depth, e, 1, d))


def _combine_kernel(y_ref, wt_ref, x_ref, modb_ref, modc_ref, o_ref, *, k, bm, rpb, n_lat, tile0):
    is_ctx = _ctx_rows(pl.program_id(0) + tile0, bm, rpb, n_lat)
    wt = wt_ref[...]
    y = wt[:, 0:1] * y_ref[0].astype(F32)
    for j in range(1, TOP_K):
        y = y + wt[:, j:j + 1] * y_ref[j].astype(F32)
    o_ref[...] = x_ref[...] + _mod_row(modb_ref, modc_ref, k, is_ctx) * y


def _combine(y4, wt, xs, mod3, *, row0, k, rpb, n_lat, batch):
    t, d = xs.shape
    rows = y4.shape[1]
    bm = _tile(rpb, 256)
    assert row0 % bm == 0 and rows % bm == 0
    tile0 = row0 // bm
    return pl.pallas_call(
        functools.partial(_combine_kernel, k=k, bm=bm, rpb=rpb, n_lat=n_lat, tile0=tile0),
        grid=(rows // bm,),
        in_specs=[
            pl.BlockSpec((TOP_K, bm, d), lambda i: (0, i, 0)),
            pl.BlockSpec((bm, LANE), lambda i: (i + tile0, 0)),
            pl.BlockSpec((bm, d), lambda i: (i + tile0, 0)),
            pl.BlockSpec((1, 6, d), lambda i: (((i + tile0) * bm) // rpb, 0, 0)),
            pl.BlockSpec((1, 6, d), lambda i: (batch, 0, 0)),
        ],
        out_specs=pl.BlockSpec((bm, d), lambda i: (i + tile0, 0)),
        out_shape=jax.ShapeDtypeStruct((t, d), F32),
        input_output_aliases={2: 0},
        compiler_params=_params(),
        name="combine",
    )(y4, wt, xs, mod3, mod3)


def _dispatch_plan(top_idx, n_experts, tm):
    t = top_idx.shape[0]
    chosen = (top_idx[:, :, None] == jnp.arange(n_experts, dtype=jnp.int32)[None, None, :]).any(axis=1)
    chosen = chosen.astype(jnp.int32)
    before = jnp.cumsum(chosen, axis=0) - chosen
    counts = before[-1] + chosen[-1]
    padded = ((counts + tm - 1) // tm) * tm
    group_end = jnp.cumsum(padded)
    group_start = group_end - padded
    dest = group_start[top_idx] + jnp.take_along_axis(before, top_idx, axis=1)
    n_rows = TOP_K * t + n_experts * tm
    n_tiles = n_rows // tm
    tile_expert = jnp.minimum(
        jnp.sum(jnp.arange(n_tiles, dtype=jnp.int32)[:, None] >= (group_end // tm)[None, :], axis=1), n_experts - 1
    ).astype(jnp.int32)
    n_used = (group_end[-1] // tm).astype(jnp.int32).reshape(1)
    row_token = (jnp.arange(n_rows, dtype=jnp.int32) % t).at[dest.reshape(-1)].set(
        jnp.repeat(jnp.arange(t, dtype=jnp.int32), TOP_K), unique_indices=True)
    return dest, row_token, tile_expert, n_used


def _moe(xs, g, mod3, router_w, router_b, w_gu, b_gu, w_d, b_d, *, layer, rpb, n_lat, batch):
    t, d = xs.shape
    e = router_w.shape[1]
    splits = MOE_TOKEN_SPLITS if batch % MOE_TOKEN_SPLITS == 0 else 1
    ts = t // splits
    h2, idx, wt = _modulate_route(xs, g, mod3, router_w, router_b, k=3, rpb=rpb, n_lat=n_lat, batch=batch)
    plans = [_dispatch_plan(idx[s * ts:(s + 1) * ts, :TOP_K], e, MOE_ROW_TILE) for s in range(splits)]
    sorted_x = [jnp.take(h2, row_token + s * ts, axis=0, mode="clip") for s, (_, row_token, _, _) in enumerate(plans)]
    for s, (dest, _, tile_expert, n_used) in enumerate(plans):
        rows = _experts(sorted_x[s], tile_expert, n_used, w_gu, b_gu, w_d, b_d, layer=layer, tm=MOE_ROW_TILE)
        y4 = jnp.take(rows, dest.T.reshape(-1), axis=0, mode="clip").reshape(TOP_K, ts, d)
        xs = _combine(y4, wt, xs, mod3, row0=s * ts, k=5, rpb=rpb, n_lat=n_lat, batch=batch)
    return xs


def _final_kernel(x_ref, g_ref, o_ref):
    o_ref[0] = _rms(x_ref[0], g_ref[...])


def _final_norm(xs3, g, n_lat):
    batch, rpb, d = xs3.shape
    bm = _tile(n_lat, 256)
    return pl.pallas_call(
        _final_kernel,
        grid=(batch, n_lat // bm),
        in_specs=[pl.BlockSpec((1, bm, d), lambda b, i: (b, i, 0)), pl.BlockSpec((1, d), lambda b, i: (0, 0))],
        out_specs=pl.BlockSpec((1, bm, d), lambda b, i: (b, i, 0)),
        out_shape=jax.ShapeDtypeStruct((batch, n_lat, d), F32),
        compiler_params=_params(),
        name="final_norm",
    )(xs3, g.reshape(1, d))


def _rope_tables(n_lat, n_ctx, head_dim):
    n_freq = head_dim // 4
    inv_freq = ROPE_THETA ** (-jnp.arange(n_freq, dtype=F32) / n_freq)
    tok = jnp.arange(n_lat, dtype=jnp.int32)
    ang_r = (tok // GRID_W).astype(F32)[:, None] * inv_freq
    ang_c = (tok % GRID_W).astype(F32)[:, None] * inv_freq
    cos = jnp.concatenate([jnp.cos(ang_r), jnp.cos(ang_r), jnp.cos(ang_c), jnp.cos(ang_c)], axis=-1)
    sin = jnp.concatenate([-jnp.sin(ang_r), jnp.sin(ang_r), -jnp.sin(ang_c), jnp.sin(ang_c)], axis=-1)
    reps = LANE // head_dim
    cos = jnp.concatenate([jnp.tile(cos, (1, reps)), jnp.ones((n_ctx, LANE), F32)], axis=0)
    sin = jnp.concatenate([jnp.tile(sin, (1, reps)), jnp.zeros((n_ctx, LANE), F32)], axis=0)
    return cos, sin, n_freq


def kernel(x, c, ctx, c_ctx, mod_w, mod_b, norm1_g, norm2_g, hyb_w_in, hyb_w_out, a_q_norm_g, a_k_norm_g, b_conv_w, b_conv_b, b_gate_a_w, b_gate_a_b, b_gate_x_w, b_gate_x_b, b_lru_lambda, c_w_in, c_w_out, c_lambda_q1, c_lambda_k1, c_lambda_q2, c_lambda_k2, c_subln_g, router_w, router_b, exp_w_gu, exp_b_gu, exp_w_d, exp_b_d, final_g):
    batch, n_lat, d = x.shape
    n_ctx = ctx.shape[1]
    rpb = n_lat + n_ctx
    t = batch * rpb
    depth = mod_w.shape[0]
    geo = dict(rpb=rpb, n_lat=n_lat, batch=batch)

    xs = jnp.concatenate([x, ctx], axis=1).reshape(t, d)
    mod_rows = -(-(batch + 1) // SUBLANE) * SUBLANE
    cvec = jnp.zeros((mod_rows, d), F32).at[:batch].set(c).at[batch].set(c_ctx)
    mod = _modmat(cvec, mod_w, mod_b)

    rope_a = _rope_tables(n_lat, n_ctx, A_HEAD_DIM)
    rope_c = _rope_tables(n_lat, n_ctx, C_HEAD_DIM)
    q_w = A_GROUP * A_HEAD_DIM * (hyb_w_in.shape[2] - 2 * (d // 2)) // ((A_GROUP + 2) * A_HEAD_DIM)
    kv_w = q_w // A_GROUP
    b_w = d // 2
    nh = b_w // B_BLOCK

    for l in range(depth):
        i = l // 2
        mod3 = mod[l].reshape(mod_rows, 6, d)
        h = _modulate(xs, norm1_g[l], mod3, k=0, **geo)
        if l % 2 == 0:
            w_in = hyb_w_in[i].astype(BF16)
            o = 0
            q = _proj(h, w_in[:, o:o + q_w], rpb=rpb, out_dtype=BF16, norm_g=a_q_norm_g[i], rope=rope_a,
                      q_cols=q_w, q_scale=A_HEAD_DIM ** -0.5 * LOG2E)
            o += q_w
            kk = _proj(h, w_in[:, o:o + kv_w], rpb=rpb, out_dtype=BF16, norm_g=a_k_norm_g[i], rope=rope_a)
            o += kv_w
            v = _proj(h, w_in[:, o:o + kv_w], rpb=rpb, out_dtype=BF16)
            o += kv_w
            xb = _proj(h, w_in[:, o:o + b_w], rpb=rpb, out_dtype=F32)
            o += b_w
            yb = _proj(h, w_in[:, o:o + b_w], rpb=rpb, out_dtype=F32)
            att = jnp.zeros((t, q_w), BF16)
            att = _gqa(q, kk, v, att, ctx_queries=False, **geo)
            att = _gqa(q, kk, v, att, ctx_queries=True, **geo)
            gw = jnp.concatenate([b_gate_a_w[i, 0], b_gate_x_w[i, 0], b_gate_a_w[i, 1], b_gate_x_w[i, 1]],
                                 axis=-1).astype(BF16)
            gb = jnp.concatenate([b_gate_a_b[i, 0].reshape(nh, 1, B_BLOCK), b_gate_x_b[i, 0].reshape(nh, 1, B_BLOCK),
                                  b_gate_a_b[i, 1].reshape(nh, 1, B_BLOCK), b_gate_x_b[i, 1].reshape(nh, 1, B_BLOCK)],
                                 axis=-1)
            ry = _rglru(xb, yb, b_conv_w[i], b_conv_b[i], gw, gb, b_lru_lambda[i], **geo)
            w_out = hyb_w_out[i].astype(BF16)
            pairs = [(att, w_out[:q_w]), (ry, w_out[q_w:])]
        else:
            lam_init = 0.8 - 0.6 * math.exp(-0.3 * l)
            w_in = c_w_in[i].astype(BF16)
            qk_w = 2 * (w_in.shape[1] // 3)
            qk = _proj(h, w_in[:, :qk_w], rpb=rpb, out_dtype=BF16, rope=rope_c,
                       q_cols=qk_w // 2, q_scale=C_HEAD_DIM ** -0.5 * LOG2E)
            v = _proj(h, w_in[:, qk_w:], rpb=rpb, out_dtype=BF16)
            lamvec = jnp.stack([c_lambda_q1[i], c_lambda_k1[i], c_lambda_q2[i], c_lambda_k2[i]]).astype(F32)
            att = jnp.zeros((t, v.shape[1]), BF16)
            att = _diff_attn(qk, v, lamvec, c_subln_g[i], att, lam_init=lam_init, ctx_queries=False, **geo)
            att = _diff_attn(qk, v, lamvec, c_subln_g[i], att, lam_init=lam_init, ctx_queries=True, **geo)
            pairs = [(att, c_w_out[i].astype(BF16))]
        xs = _resid_proj(xs, mod3, pairs, k=2, **geo)
        xs = _moe(xs, norm2_g[l], mod3, router_w[l], router_b[l], exp_w_gu, exp_b_gu, exp_w_d, exp_b_d,
                  layer=l, **geo)
    return _final_norm(xs.reshape(batch, rpb, d), final_g, n_lat)
```

```python
import functools
import math

import jax
import jax.numpy as jnp
from jax import lax
from jax.experimental import pallas as pl
from jax.experimental.pallas import tpu as pltpu

F32 = jnp.float32
BF16 = jnp.bfloat16

GRID_W = 64
ROPE_THETA = 10000.0
NORM_EPS = 1e-6
A_HEAD_DIM = 128
A_GROUP = 4
B_BLOCK = 128
LRU_C = 8.0
C_HEAD_DIM = 64
TOP_K = 4
SWIGLU_LIMIT = 7.0
SWIGLU_ALPHA = 1.702
LOG2E = math.log2(math.e)

LANE = 128
SUBLANE = 8
VMEM_LIMIT_BYTES = 56 * 1024 * 1024
ATTN_Q_ROWS = 1024
ATTN_KEY_CHUNK = 256
MOE_ROW_TILE = 512
MOE_TOKEN_SPLITS = 2


def _params():
    return pltpu.CompilerParams(vmem_limit_bytes=VMEM_LIMIT_BYTES)


def _tile(n, target, mult=SUBLANE):
    best = None
    for t in range(mult, min(n, target) + 1, mult):
        if n % t == 0:
            best = t
    assert best is not None, (n, target, mult)
    return best


def _ctx_rows(i, bm, rpb, n_lat):
    pos0 = lax.rem(i * bm, rpb)
    row = lax.broadcasted_iota(jnp.int32, (bm, 1), 0)
    return (pos0 + row) >= n_lat


def _mod_row(modb_ref, modc_ref, k, is_ctx):
    return jnp.where(is_ctx, modc_ref[0, k:k + 1, :], modb_ref[0, k:k + 1, :])


def _rms(x, g):
    return x * lax.rsqrt(jnp.mean(x * x, axis=-1, keepdims=True) + NORM_EPS) * g


def _dot(a, b):
    return jnp.dot(a, b, preferred_element_type=F32)


def _dot_nt(a, b):
    return lax.dot_general(a, b, (((1,), (1,)), ((), ())), preferred_element_type=F32)


def _modmat_kernel(c_ref, w_ref, b_ref, o_ref):
    c = c_ref[...]
    s = (c * jax.nn.sigmoid(c)).astype(BF16)
    o_ref[0] = _dot(s, w_ref[0].astype(BF16)) + b_ref[0]


def _modmat(cvec, mod_w, mod_b):
    depth, d, n = mod_w.shape
    r = cvec.shape[0]
    bn = _tile(n, 512, LANE)
    return pl.pallas_call(
        _modmat_kernel,
        grid=(depth, n // bn),
        in_specs=[
            pl.BlockSpec((r, d), lambda l, j: (0, 0)),
            pl.BlockSpec((1, d, bn), lambda l, j: (l, 0, j)),
            pl.BlockSpec((1, 1, bn), lambda l, j: (l, 0, j)),
        ],
        out_specs=pl.BlockSpec((1, r, bn), lambda l, j: (l, 0, j)),
        out_shape=jax.ShapeDtypeStruct((depth, r, n), F32),
        compiler_params=_params(),
        name="modmat",
    )(cvec, mod_w, mod_b.reshape(depth, 1, n))


def _modulate_kernel(x_ref, g_ref, modb_ref, modc_ref, h_ref, *, k, bm, rpb, n_lat):
    is_ctx = _ctx_rows(pl.program_id(0), bm, rpb, n_lat)
    y = _rms(x_ref[...], g_ref[...])
    h = y * (1.0 + _mod_row(modb_ref, modc_ref, k + 1, is_ctx)) + _mod_row(modb_ref, modc_ref, k, is_ctx)
    h_ref[...] = h.astype(BF16)


def _mod_specs(bm, rpb, batch, d):
    return [
        pl.BlockSpec((1, 6, d), lambda i: ((i * bm) // rpb, 0, 0)),
        pl.BlockSpec((1, 6, d), lambda i: (batch, 0, 0)),
    ]


def _modulate(xs, g, mod3, *, k, rpb, n_lat, batch):
    t, d = xs.shape
    bm = _tile(rpb, 256)
    return pl.pallas_call(
        functools.partial(_modulate_kernel, k=k, bm=bm, rpb=rpb, n_lat=n_lat),
        grid=(t // bm,),
        in_specs=[pl.BlockSpec((bm, d), lambda i: (i, 0)), pl.BlockSpec((1, d), lambda i: (0, 0))]
        + _mod_specs(bm, rpb, batch, d),
        out_specs=pl.BlockSpec((bm, d), lambda i: (i, 0)),
        out_shape=jax.ShapeDtypeStruct((t, d), BF16),
        compiler_params=_params(),
        name="modulate",
    )(xs, g.reshape(1, d), mod3, mod3)


def _router_kernel(x_ref, g_ref, modb_ref, modc_ref, rwh_ref, rwl_ref, rb_ref, h_ref, idx_ref, wt_ref,
                   *, k, bm, rpb, n_lat):
    is_ctx = _ctx_rows(pl.program_id(0), bm, rpb, n_lat)
    y = _rms(x_ref[...], g_ref[...])
    h = y * (1.0 + _mod_row(modb_ref, modc_ref, k + 1, is_ctx)) + _mod_row(modb_ref, modc_ref, k, is_ctx)
    h_hi = h.astype(BF16)
    h_lo = (h - h_hi.astype(F32)).astype(BF16)
    h_ref[...] = h_hi
    logits = _dot(h_hi, rwh_ref[...]) + (_dot(h_lo, rwh_ref[...]) + _dot(h_hi, rwl_ref[...])) + rb_ref[...]
    lane = lax.broadcasted_iota(jnp.int32, logits.shape, 1)
    vals, idxs = [], []
    for _ in range(TOP_K):
        m = jnp.max(logits, axis=-1, keepdims=True)
        ik = jnp.min(jnp.where(logits == m, lane, LANE), axis=-1, keepdims=True)
        vals.append(m)
        idxs.append(ik)
        logits = jnp.where(lane == ik, -jnp.inf, logits)
    es = [jnp.exp(v - vals[0]) for v in vals]
    den = es[0] + es[1] + es[2] + es[3]
    idx_out = jnp.zeros(lane.shape, jnp.int32)
    wt_out = jnp.zeros(lane.shape, F32)
    for j in range(TOP_K):
        idx_out = jnp.where(lane == j, idxs[j], idx_out)
        wt_out = jnp.where(lane == j, es[j] / den, wt_out)
    idx_ref[...] = idx_out
    wt_ref[...] = wt_out


def _modulate_route(xs, g, mod3, router_w, router_b, *, k, rpb, n_lat, batch):
    t, d = xs.shape
    e = router_w.shape[1]
    bm = _tile(rpb, 256)
    rw = jnp.zeros((d, LANE), F32).at[:, :e].set(router_w)
    rw_hi = rw.astype(BF16)
    rw_lo = (rw - rw_hi.astype(F32)).astype(BF16)
    rb = jnp.full((1, LANE), -1e30, F32).at[0, :e].set(router_b)
    return pl.pallas_call(
        functools.partial(_router_kernel, k=k, bm=bm, rpb=rpb, n_lat=n_lat),
        grid=(t // bm,),
        in_specs=[pl.BlockSpec((bm, d), lambda i: (i, 0)), pl.BlockSpec((1, d), lambda i: (0, 0))]
        + _mod_specs(bm, rpb, batch, d)
        + [pl.BlockSpec((d, LANE), lambda i: (0, 0))] * 2 + [pl.BlockSpec((1, LANE), lambda i: (0, 0))],
        out_specs=[pl.BlockSpec((bm, d), lambda i: (i, 0)),
                   pl.BlockSpec((bm, LANE), lambda i: (i, 0)),
                   pl.BlockSpec((bm, LANE), lambda i: (i, 0))],
        out_shape=[jax.ShapeDtypeStruct((t, d), BF16),
                   jax.ShapeDtypeStruct((t, LANE), jnp.int32),
                   jax.ShapeDtypeStruct((t, LANE), F32)],
        compiler_params=_params(),
        name="modulate_route",
    )(xs, g.reshape(1, d), mod3, mod3, rw_hi, rw_lo, rb)


def _gelu(x):
    return 0.5 * x * (1.0 + jnp.tanh(math.sqrt(2.0 / math.pi) * (x + 0.044715 * (x * x * x))))


def _proj_kernel(*refs, norm, rope_shift, q_cols, q_scale, bn):
    a_ref, w_ref = refs[0], refs[1]
    o_ref = refs[-1]
    pos = 2
    g_ref = cos_ref = sin_ref = None
    if norm:
        g_ref = refs[pos]
        pos += 1
    if rope_shift:
        cos_ref, sin_ref = refs[pos], refs[pos + 1]
    acc = _dot(a_ref[...], w_ref[...])
    if not norm and not rope_shift:
        o_ref[...] = acc.astype(o_ref.dtype)
        return
    col_scale = jnp.where(pl.program_id(0) * bn < q_cols, q_scale, 1.0)
    lane = lax.broadcasted_iota(jnp.int32, (acc.shape[0], LANE), 1)
    first_half = jnp.bitwise_and(lane, 2 * rope_shift - 1) < rope_shift
    for c in range(bn // LANE):
        y = acc[:, c * LANE:(c + 1) * LANE]
        if norm:
            y = _rms(y, g_ref[...])
        partner = jnp.where(first_half, pltpu.roll(y, LANE - rope_shift, 1), pltpu.roll(y, rope_shift, 1))
        y = (y * cos_ref[...] + partner * sin_ref[...]) * col_scale
        o_ref[:, c * LANE:(c + 1) * LANE] = y.astype(o_ref.dtype)


def _proj(a, w, *, rpb, out_dtype, norm_g=None, rope=None, q_cols=0, q_scale=1.0):
    t, kdim = a.shape
    n = w.shape[1]
    bm = _tile(rpb, 1152, 16)
    bn = _tile(math.gcd(n, q_cols) if q_cols else n, 1024, LANE)
    tiles_per_sample = rpb // bm
    in_specs = [pl.BlockSpec((bm, kdim), lambda j, i: (i, 0)), pl.BlockSpec((kdim, bn), lambda j, i: (0, j))]
    args = [a, w]
    if norm_g is not None:
        in_specs.append(pl.BlockSpec((1, LANE), lambda j, i: (0, 0)))
        args.append(norm_g.reshape(1, LANE).astype(F32))
    rope_shift = 0
    if rope is not None:
        cos, sin, rope_shift = rope
        in_specs += [pl.BlockSpec((bm, LANE), lambda j, i: (i % tiles_per_sample, 0))] * 2
        args += [cos, sin]
    return pl.pallas_call(
        functools.partial(_proj_kernel, norm=norm_g is not None, rope_shift=rope_shift, q_cols=q_cols,
                          q_scale=q_scale, bn=bn),
        grid=(n // bn, t // bm),
        in_specs=in_specs,
        out_specs=pl.BlockSpec((bm, bn), lambda j, i: (i, j)),
        out_shape=jax.ShapeDtypeStruct((t, n), out_dtype),
        compiler_params=_params(),
        name="proj",
    )(*args)


def _resid_kernel(*refs, n_pairs, k, bm, rpb, n_lat):
    x_ref, modb_ref, modc_ref = refs[0], refs[1], refs[2]
    o_ref = refs[-1]
    acc = None
    for p in range(n_pairs):
        part = _dot(refs[3 + 2 * p][...], refs[4 + 2 * p][...])
        acc = part if acc is None else acc + part
    is_ctx = _ctx_rows(pl.program_id(1), bm, rpb, n_lat)
    o_ref[...] = x_ref[...] + _mod_row(modb_ref, modc_ref, k, is_ctx) * acc


def _resid_proj(xs, mod3, pairs, *, k, rpb, n_lat, batch):
    t, d = xs.shape
    bm = _tile(rpb, 768, 16)
    bn = _tile(d, 1024, LANE)
    in_specs = [
        pl.BlockSpec((bm, bn), lambda j, i: (i, j)),
        pl.BlockSpec((1, 6, bn), lambda j, i: ((i * bm) // rpb, 0, j)),
        pl.BlockSpec((1, 6, bn), lambda j, i: (batch, 0, j)),
    ]
    args = [xs, mod3, mod3]
    for a, w in pairs:
        kdim = a.shape[1]
        in_specs += [pl.BlockSpec((bm, kdim), lambda j, i: (i, 0)), pl.BlockSpec((kdim, bn), lambda j, i: (0, j))]
        args += [a, w]
    return pl.pallas_call(
        functools.partial(_resid_kernel, n_pairs=len(pairs), k=k, bm=bm, rpb=rpb, n_lat=n_lat),
        grid=(d // bn, t // bm),
        in_specs=in_specs,
        out_specs=pl.BlockSpec((bm, bn), lambda j, i: (i, j)),
        out_shape=jax.ShapeDtypeStruct((t, d), F32),
        compiler_params=_params(),
        name="resid_proj",
    )(*args)


def _fill_v_ext(v_ref, vext_ref):
    width = v_ref.shape[1]
    vext_ref[:, :width] = v_ref[...]
    vext_ref[:, width:] = jnp.ones((v_ref.shape[0], vext_ref.shape[1] - width), vext_ref.dtype)


def _attend(qs, k_ref, vext_ref, s_refs, p_refs):
    tq = qs[0].shape[0]
    lk = k_ref.shape[0]
    kc = _tile(lk, ATTN_KEY_CHUNK, LANE)

    def scores(q, c, s_ref, lane_max):
        s = _dot_nt(q, k_ref[c * kc:(c + 1) * kc])
        s_ref[:, c * kc:(c + 1) * kc] = s
        for j in range(kc // LANE):
            lane_max = jnp.maximum(lane_max, s[:, j * LANE:(j + 1) * LANE])
        return lane_max

    def exponentials(s_ref, p_ref, c, row_max):
        p_ref[:, c * kc:(c + 1) * kc] = jnp.exp2(s_ref[:, c * kc:(c + 1) * kc] - row_max).astype(BF16)

    lowest = jnp.full((tq, LANE), -jnp.inf, F32)
    lane_max = lowest
    for c in range(lk // kc):
        lane_max = scores(qs[0], c, s_refs[0], lane_max)
    row_max = jnp.max(lane_max, axis=-1, keepdims=True)
    for g in range(len(qs)):
        cur, nxt = g % 2, (g + 1) % 2
        lane_max = lowest
        for c in range(lk // kc):
            if g + 1 < len(qs):
                lane_max = scores(qs[g + 1], c, s_refs[nxt], lane_max)
            exponentials(s_refs[cur], p_refs[cur], c, row_max)
        if g + 1 < len(qs):
            row_max = jnp.max(lane_max, axis=-1, keepdims=True)
        yield _dot(p_refs[cur][...], vext_ref[...])


def _attn_geometry(rpb, n_lat, ctx_queries):
    n_ctx = rpb - n_lat
    if ctx_queries:
        assert n_lat % n_ctx == 0
        return n_ctx, 1, n_ctx, (lambda i: n_lat // n_ctx), n_lat // n_ctx
    tq = _tile(n_lat, ATTN_Q_ROWS)
    return tq, n_lat // tq, rpb, (lambda i: i), 0


def _attn_scratch(tq, lk, width):
    return [pltpu.VMEM((lk, 2 * width), BF16),
            pltpu.VMEM((tq, lk), F32), pltpu.VMEM((tq, lk), F32),
            pltpu.VMEM((tq, lk), BF16), pltpu.VMEM((tq, lk), BF16)]


def _gqa_kernel(q_ref, k_ref, v_ref, prev_ref, o_ref, vext_ref, s0_ref, s1_ref, p0_ref, p1_ref, *, group):
    del prev_ref

    @pl.when(pl.program_id(2) == 0)
    def _():
        _fill_v_ext(v_ref, vext_ref)

    qs = [q_ref[:, g * A_HEAD_DIM:(g + 1) * A_HEAD_DIM] for g in range(group)]
    for g, oe in enumerate(_attend(qs, k_ref, vext_ref, (s0_ref, s1_ref), (p0_ref, p1_ref))):
        o_ref[:, g * A_HEAD_DIM:(g + 1) * A_HEAD_DIM] = (oe[:, :A_HEAD_DIM] / oe[:, A_HEAD_DIM:]).astype(o_ref.dtype)


def _gqa(q, k, v, att, *, rpb, n_lat, batch, ctx_queries):
    t = q.shape[0]
    kvh = k.shape[1] // A_HEAD_DIM
    gw = A_GROUP * A_HEAD_DIM
    tq, nq, lk, q_blk, k_blk = _attn_geometry(rpb, n_lat, ctx_queries)
    per_sample = lambda a: a.reshape(batch, rpb, a.shape[1])
    kernel = functools.partial(_gqa_kernel, group=A_GROUP)
    return pl.pallas_call(
        kernel,
        grid=(batch, kvh, nq),
        in_specs=[
            pl.BlockSpec((None, tq, gw), lambda b, h, i: (b, q_blk(i), h)),
            pl.BlockSpec((None, lk, A_HEAD_DIM), lambda b, h, i: (b, k_blk, h)),
            pl.BlockSpec((None, lk, A_HEAD_DIM), lambda b, h, i: (b, k_blk, h)),
            pl.BlockSpec(memory_space=pl.ANY),
        ],
        out_specs=pl.BlockSpec((None, tq, gw), lambda b, h, i: (b, q_blk(i), h)),
        out_shape=jax.ShapeDtypeStruct((batch, rpb, q.shape[1]), BF16),
        scratch_shapes=_attn_scratch(tq, lk, A_HEAD_DIM),
        input_output_aliases={3: 0},
        compiler_params=_params(),
        name="gqa_ctx" if ctx_queries else "gqa_lat",
    )(per_sample(q), per_sample(k), per_sample(v), per_sample(att)).reshape(t, q.shape[1])


def _diff_kernel(lam_ref, g_ref, q_ref, k_ref, v_ref, prev_ref, o_ref, vext_ref, s0_ref, s1_ref, p0_ref, p1_ref,
                 *, lam_init):
    del prev_ref

    @pl.when(pl.program_id(2) == 0)
    def _():
        _fill_v_ext(v_ref, vext_ref)

    lv = lam_ref[...]
    lam = (jnp.exp(jnp.sum(lv[0:1] * lv[1:2], axis=-1, keepdims=True))
           - jnp.exp(jnp.sum(lv[2:3] * lv[3:4], axis=-1, keepdims=True)) + lam_init)
    q = q_ref[...]
    hw = q.shape[1]
    lane = lax.broadcasted_iota(jnp.int32, q.shape, 1)
    zero = jnp.zeros_like(q)
    qs = [jnp.where(lane < C_HEAD_DIM, q, zero), jnp.where(lane >= C_HEAD_DIM, q, zero)]
    oe1, oe2 = _attend(qs, k_ref, vext_ref, (s0_ref, s1_ref), (p0_ref, p1_ref))
    o = oe1[:, :hw] / oe1[:, hw:] - lam * (oe2[:, :hw] / oe2[:, hw:])
    o_ref[...] = (_rms(o, g_ref[...]) * (1.0 - lam_init)).astype(o_ref.dtype)


def _diff_attn(qk, v, lamvec, subln_g, out, *, lam_init, rpb, n_lat, batch, ctx_queries):
    t = qk.shape[0]
    heads = v.shape[1] // (2 * C_HEAD_DIM)
    hw = 2 * C_HEAD_DIM
    tq, nq, lk, q_blk, k_blk = _attn_geometry(rpb, n_lat, ctx_queries)
    per_sample = lambda a: a.reshape(batch, rpb, a.shape[1])
    kernel = functools.partial(_diff_kernel, lam_init=lam_init)
    return pl.pallas_call(
        kernel,
        grid=(batch, heads, nq),
        in_specs=[
            pl.BlockSpec(lamvec.shape, lambda b, h, i: (0, 0)),
            pl.BlockSpec((1, hw), lambda b, h, i: (0, 0)),
            pl.BlockSpec((None, tq, hw), lambda b, h, i: (b, q_blk(i), h)),
            pl.BlockSpec((None, lk, hw), lambda b, h, i: (b, k_blk, heads + h)),
            pl.BlockSpec((None, lk, hw), lambda b, h, i: (b, k_blk, h)),
            pl.BlockSpec(memory_space=pl.ANY),
        ],
        out_specs=pl.BlockSpec((None, tq, hw), lambda b, h, i: (b, q_blk(i), h)),
        out_shape=jax.ShapeDtypeStruct((batch, rpb, v.shape[1]), BF16),
        scratch_shapes=_attn_scratch(tq, lk, hw),
        input_output_aliases={5: 0},
        compiler_params=_params(),
        name="diff_ctx" if ctx_queries else "diff_lat",
    )(lamvec, subln_g.reshape(1, hw).astype(F32), per_sample(qk), per_sample(qk), per_sample(v),
      per_sample(out)).reshape(t, v.shape[1])


CONV_PAD = SUBLANE
SCAN_UNROLL = 8


def _block_scan(a, u, carry, reverse):
    row = lax.broadcasted_iota(jnp.int32, a.shape, 0)
    for s in (1, 2, 4):
        shift = SUBLANE - s if reverse else s
        valid = (row < SUBLANE - s) if reverse else (row >= s)
        a_prev = pltpu.roll(a, shift, 0)
        u_prev = pltpu.roll(u, shift, 0)
        u = jnp.where(valid, a * u_prev + u, u)
        a = jnp.where(valid, a * a_prev, a)
    h = u + a * carry
    last = h[0:1] if reverse else h[SUBLANE - 1:SUBLANE]
    return h, jnp.broadcast_to(last, h.shape)


def _rglru_kernel(xb_ref, yb_ref, cw_ref, cb_ref, gw_ref, gb_ref, lam_ref, o_ref,
                  xpad_ref, xc_ref, a_ref, u_ref, hf_ref, hb_ref, *, n_lat, n_ctx):
    cw = cw_ref[...]
    taps = cw.shape[0]
    zpad = jnp.zeros((CONV_PAD, B_BLOCK), F32)

    def conv_segment(r0, n):
        xpad_ref[0:CONV_PAD] = zpad
        xpad_ref[CONV_PAD:CONV_PAD + n] = xb_ref[r0:r0 + n]
        xpad_ref[CONV_PAD + n:2 * CONV_PAD + n] = zpad
        acc = jnp.broadcast_to(cb_ref[...], (n, B_BLOCK))
        for j in range(taps):
            off = CONV_PAD + j - taps // 2
            acc = acc + cw[j:j + 1] * xpad_ref[off:off + n]
        xc_ref[r0:r0 + n] = acc

    conv_segment(0, n_lat)
    conv_segment(n_lat, n_ctx)

    xc = xc_ref[...]
    gates = _dot(xc.astype(BF16), gw_ref[0]) + gb_ref[0]
    lam = lam_ref[...]
    softplus_neg = jnp.maximum(-lam, 0.0) + jnp.log(1.0 + jnp.exp(-jnp.abs(lam)))
    for d in range(2):
        r = jax.nn.sigmoid(gates[:, (2 * d) * B_BLOCK:(2 * d + 1) * B_BLOCK])
        i = jax.nn.sigmoid(gates[:, (2 * d + 1) * B_BLOCK:(2 * d + 2) * B_BLOCK])
        log_a = -LRU_C * r * softplus_neg[d:d + 1]
        a_ref[d] = jnp.exp(log_a)
        u_ref[d] = jnp.sqrt(1.0 - jnp.exp(2.0 * log_a)) * i * xc

    def scan_segment(r0, n, carry):
        nblk = n // SUBLANE

        def body(b, c):
            cf, cb = c
            rf = pl.multiple_of(r0 + b * SUBLANE, SUBLANE)
            rb = pl.multiple_of(r0 + (nblk - 1 - b) * SUBLANE, SUBLANE)
            hf, cf = _block_scan(a_ref[0, pl.ds(rf, SUBLANE), :], u_ref[0, pl.ds(rf, SUBLANE), :], cf, False)
            hb, cb = _block_scan(a_ref[1, pl.ds(rb, SUBLANE), :], u_ref[1, pl.ds(rb, SUBLANE), :], cb, True)
            hf_ref[pl.ds(rf, SUBLANE), :] = hf
            hb_ref[pl.ds(rb, SUBLANE), :] = hb
            return cf, cb

        return lax.fori_loop(0, nblk, body, carry, unroll=math.gcd(nblk, SCAN_UNROLL))

    zero = jnp.zeros((SUBLANE, B_BLOCK), F32)
    carry = scan_segment(n_lat, n_ctx, (zero, zero))
    scan_segment(0, n_lat, carry)
    o_ref[...] = ((hf_ref[...] + hb_ref[...]) * _gelu(yb_ref[...])).astype(o_ref.dtype)


def _rglru(xb, yb, conv_w, conv_b, gw, gb, lam, *, rpb, n_lat, batch):
    t, width = xb.shape
    nh = width // B_BLOCK
    n_ctx = rpb - n_lat
    taps = conv_w.shape[0]
    return pl.pallas_call(
        functools.partial(_rglru_kernel, n_lat=n_lat, n_ctx=n_ctx),
        grid=(batch, nh),
        in_specs=[
            pl.BlockSpec((rpb, B_BLOCK), lambda b, h: (b, h)),
            pl.BlockSpec((rpb, B_BLOCK), lambda b, h: (b, h)),
            pl.BlockSpec((taps, B_BLOCK), lambda b, h: (0, h)),
            pl.BlockSpec((1, B_BLOCK), lambda b, h: (0, h)),
            pl.BlockSpec((1, B_BLOCK, 4 * B_BLOCK), lambda b, h: (h, 0, 0)),
            pl.BlockSpec((1, 1, 4 * B_BLOCK), lambda b, h: (h, 0, 0)),
            pl.BlockSpec((2, B_BLOCK), lambda b, h: (0, h)),
        ],
        out_specs=pl.BlockSpec((rpb, B_BLOCK), lambda b, h: (b, h)),
        out_shape=jax.ShapeDtypeStruct((t, width), BF16),
        scratch_shapes=[
            pltpu.VMEM((n_lat + 2 * CONV_PAD, B_BLOCK), F32),
            pltpu.VMEM((rpb, B_BLOCK), F32),
            pltpu.VMEM((2, rpb, B_BLOCK), F32),
            pltpu.VMEM((2, rpb, B_BLOCK), F32),
            pltpu.VMEM((rpb, B_BLOCK), F32),
            pltpu.VMEM((rpb, B_BLOCK), F32),
        ],
        compiler_params=_params(),
        name="rglru",
    )(xb, yb, conv_w, conv_b.reshape(1, width), gw, gb, lam)


CAST_ROWS = 256


def _expert_kernel(te_ref, nu_ref, x_ref, wgu_ref, bgu_ref, wd_ref, bd_ref, o_ref, wgu_bf, wd_bf, *, f):
    j = pl.program_id(0)

    @pl.when((j == 0) | (te_ref[j] != te_ref[jnp.maximum(j - 1, 0)]))
    def _():
        for r in range(0, wgu_bf.shape[0], CAST_ROWS):
            wgu_bf[r:r + CAST_ROWS] = wgu_ref[0, 0, r:r + CAST_ROWS].astype(BF16)
        for r in range(0, wd_bf.shape[0], CAST_ROWS):
            wd_bf[r:r + CAST_ROWS] = wd_ref[0, 0, r:r + CAST_ROWS].astype(BF16)

    @pl.when(j < nu_ref[0])
    def _():
        gu = _dot(x_ref[...], wgu_bf[...]) + bgu_ref[0, 0]
        gate = jnp.minimum(gu[:, :f], SWIGLU_LIMIT)
        up = jnp.clip(gu[:, f:], -SWIGLU_LIMIT, SWIGLU_LIMIT)
        act = (up + 1.0) * gate * jax.nn.sigmoid(gate * SWIGLU_ALPHA)
        o_ref[...] = (_dot(act.astype(BF16), wd_bf[...]) + bd_ref[0, 0]).astype(o_ref.dtype)

    @pl.when(j >= nu_ref[0])
    def _():
        o_ref[...] = jnp.zeros(o_ref.shape, o_ref.dtype)


def _experts(x_sorted, tile_expert, n_used, w_gu, b_gu, w_d, b_d, *, layer, tm):
    r, d = x_sorted.shape
    depth, e, _, f2 = w_gu.shape
    f = f2 // 2
    assert d % CAST_ROWS == 0 and f % CAST_ROWS == 0
    grid_spec = pltpu.PrefetchScalarGridSpec(
        num_scalar_prefetch=2,
        grid=(r // tm,),
        in_specs=[
            pl.BlockSpec((tm, d), lambda j, te, nu: (j, 0)),
            pl.BlockSpec((1, 1, d, f2), lambda j, te, nu: (layer, te[j], 0, 0)),
            pl.BlockSpec((1, 1, 1, f2), lambda j, te, nu: (layer, te[j], 0, 0)),
            pl.BlockSpec((1, 1, f, d), lambda j, te, nu: (layer, te[j], 0, 0)),
            pl.BlockSpec((1, 1, 1, d), lambda j, te, nu: (layer, te[j], 0, 0)),
        ],
        out_specs=pl.BlockSpec((tm, d), lambda j, te, nu: (j, 0)),
        scratch_shapes=[pltpu.VMEM((d, f2), BF16), pltpu.VMEM((f, d), BF16)],
    )
    return pl.pallas_call(
        functools.partial(_expert_kernel, f=f),
        grid_spec=grid_spec,
        out_shape=jax.ShapeDtypeStruct((r, d), BF16),
        compiler_params=_params(),
        name="experts",
    )(tile_expert, n_used, x_sorted, w_gu, b_gu.reshape(depth, e, 1, f2), w_d, b_d.reshape(depth, e, 1, d))


def _combine_kernel(y_ref, wt_ref, x_ref, modb_ref, modc_ref, *rest, k, bm, rpb, n_lat, tile0, with_next):
    is_ctx = _ctx_rows(pl.program_id(0) + tile0, bm, rpb, n_lat)
    wt = wt_ref[...]
    y = wt[:, 0:1] * y_ref[0].astype(F32)
    for j in range(1, TOP_K):
        y = y + wt[:, j:j + 1] * y_ref[j].astype(F32)
    x_new = x_ref[...] + _mod_row(modb_ref, modc_ref, k, is_ctx) * y
    if not with_next:
        (o_ref,) = rest
        o_ref[...] = x_new
        return
    g_ref, nmodb_ref, nmodc_ref, _, o_ref, h_ref = rest
    o_ref[...] = x_new
    h = _rms(x_new, g_ref[...]) * (1.0 + _mod_row(nmodb_ref, nmodc_ref, 1, is_ctx)) + _mod_row(nmodb_ref, nmodc_ref, 0, is_ctx)
    h_ref[...] = h.astype(BF16)


def _combine(y4, wt, xs, mod3, *, row0, k, rpb, n_lat, batch, nxt=None):
    t, d = xs.shape
    rows = y4.shape[1]
    bm = _tile(rpb, 256)
    assert row0 % bm == 0 and rows % bm == 0
    tile0 = row0 // bm
    row_spec = lambda width: pl.BlockSpec((bm, width), lambda i: (i + tile0, 0))
    mod_specs = [pl.BlockSpec((1, 6, d), lambda i: (((i + tile0) * bm) // rpb, 0, 0)),
                 pl.BlockSpec((1, 6, d), lambda i: (batch, 0, 0))]
    in_specs = [pl.BlockSpec((TOP_K, bm, d), lambda i: (0, i, 0)), row_spec(LANE), row_spec(d)] + mod_specs
    args = [y4, wt, xs, mod3, mod3]
    out_specs, out_shape, aliases = [row_spec(d)], [jax.ShapeDtypeStruct((t, d), F32)], {2: 0}
    if nxt is not None:
        g_next, mod3_next, h = nxt
        in_specs += [pl.BlockSpec((1, d), lambda i: (0, 0))] + mod_specs + [pl.BlockSpec(memory_space=pl.ANY)]
        args += [g_next.reshape(1, d), mod3_next, mod3_next, h]
        out_specs.append(row_spec(d))
        out_shape.append(jax.ShapeDtypeStruct((t, d), BF16))
        aliases[len(args) - 1] = 1
    out = pl.pallas_call(
        functools.partial(_combine_kernel, k=k, bm=bm, rpb=rpb, n_lat=n_lat, tile0=tile0, with_next=nxt is not None),
        grid=(rows // bm,),
        in_specs=in_specs,
        out_specs=out_specs,
        out_shape=out_shape,
        input_output_aliases=aliases,
        compiler_params=_params(),
        name="combine",
    )(*args)
    return (out[0], out[1]) if nxt is not None else (out[0], None)


def _dispatch_plan(top_idx, n_experts, tm):
    t = top_idx.shape[0]
    chosen = (top_idx[:, :, None] == jnp.arange(n_experts, dtype=jnp.int32)[None, None, :]).any(axis=1)
    chosen = chosen.astype(jnp.int32)
    before = jnp.cumsum(chosen, axis=0) - chosen
    counts = before[-1] + chosen[-1]
    padded = ((counts + tm - 1) // tm) * tm
    group_end = jnp.cumsum(padded)
    group_start = group_end - padded
    dest = group_start[top_idx] + jnp.take_along_axis(before, top_idx, axis=1)
    n_rows = TOP_K * t + n_experts * tm
    n_tiles = n_rows // tm
    tile_expert = jnp.minimum(
        jnp.sum(jnp.arange(n_tiles, dtype=jnp.int32)[:, None] >= (group_end // tm)[None, :], axis=1), n_experts - 1
    ).astype(jnp.int32)
    n_used = (group_end[-1] // tm).astype(jnp.int32).reshape(1)
    pad = jnp.arange(tm, dtype=jnp.int32)[None, :]
    pad_row = (group_start + counts)[:, None] + pad
    pad_key = jnp.where(pad_row < group_end[:, None], pad_row, n_rows + pad_row).reshape(-1)
    keys = jnp.concatenate([dest.reshape(-1), pad_key])
    vals = jnp.concatenate([jnp.repeat(jnp.arange(t, dtype=jnp.int32), TOP_K),
                            jnp.arange(n_experts * tm, dtype=jnp.int32) % t])
    _, row_token = lax.sort((keys, vals), num_keys=1)
    return dest, row_token, tile_expert, n_used


def _moe(xs, g, mod3, router_w, router_b, w_gu, b_gu, w_d, b_d, *, layer, rpb, n_lat, batch, nxt=None):
    t, d = xs.shape
    e = router_w.shape[1]
    splits = MOE_TOKEN_SPLITS if batch % MOE_TOKEN_SPLITS == 0 else 1
    ts = t // splits
    h2, idx, wt = _modulate_route(xs, g, mod3, router_w, router_b, k=3, rpb=rpb, n_lat=n_lat, batch=batch)
    h = None if nxt is None else jnp.zeros((t, d), BF16)
    plans = [_dispatch_plan(idx[s * ts:(s + 1) * ts, :TOP_K], e, MOE_ROW_TILE) for s in range(splits)]
    sorted_x = [jnp.take(h2, row_token + s * ts, axis=0, mode="clip") for s, (_, row_token, _, _) in enumerate(plans)]
    for s, (dest, _, tile_expert, n_used) in enumerate(plans):
        rows = _experts(sorted_x[s], tile_expert, n_used, w_gu, b_gu, w_d, b_d, layer=layer, tm=MOE_ROW_TILE)
        y4 = jnp.take(rows, dest.T.reshape(-1), axis=0, mode="clip").reshape(TOP_K, ts, d)
        xs, h = _combine(y4, wt, xs, mod3, row0=s * ts, k=5, rpb=rpb, n_lat=n_lat, batch=batch,
                         nxt=None if nxt is None else (*nxt, h))
    return xs, h


def _final_kernel(x_ref, g_ref, o_ref):
    o_ref[0] = _rms(x_ref[0], g_ref[...])


def _final_norm(xs3, g, n_lat):
    batch, rpb, d = xs3.shape
    bm = _tile(n_lat, 256)
    return pl.pallas_call(
        _final_kernel,
        grid=(batch, n_lat // bm),
        in_specs=[pl.BlockSpec((1, bm, d), lambda b, i: (b, i, 0)), pl.BlockSpec((1, d), lambda b, i: (0, 0))],
        out_specs=pl.BlockSpec((1, bm, d), lambda b, i: (b, i, 0)),
        out_shape=jax.ShapeDtypeStruct((batch, n_lat, d), F32),
        compiler_params=_params(),
        name="final_norm",
    )(xs3, g.reshape(1, d))


def _rope_tables(n_lat, n_ctx, head_dim):
    n_freq = head_dim // 4
    inv_freq = ROPE_THETA ** (-jnp.arange(n_freq, dtype=F32) / n_freq)
    tok = jnp.arange(n_lat, dtype=jnp.int32)
    ang_r = (tok // GRID_W).astype(F32)[:, None] * inv_freq
    ang_c = (tok % GRID_W).astype(F32)[:, None] * inv_freq
    cos = jnp.concatenate([jnp.cos(ang_r), jnp.cos(ang_r), jnp.cos(ang_c), jnp.cos(ang_c)], axis=-1)
    sin = jnp.concatenate([-jnp.sin(ang_r), jnp.sin(ang_r), -jnp.sin(ang_c), jnp.sin(ang_c)], axis=-1)
    reps = LANE // head_dim
    cos = jnp.concatenate([jnp.tile(cos, (1, reps)), jnp.ones((n_ctx, LANE), F32)], axis=0)
    sin = jnp.concatenate([jnp.tile(sin, (1, reps)), jnp.zeros((n_ctx, LANE), F32)], axis=0)
    return cos, sin, n_freq


def kernel(x, c, ctx, c_ctx, mod_w, mod_b, norm1_g, norm2_g, hyb_w_in, hyb_w_out, a_q_norm_g, a_k_norm_g, b_conv_w, b_conv_b, b_gate_a_w, b_gate_a_b, b_gate_x_w, b_gate_x_b, b_lru_lambda, c_w_in, c_w_out, c_lambda_q1, c_lambda_k1, c_lambda_q2, c_lambda_k2, c_subln_g, router_w, router_b, exp_w_gu, exp_b_gu, exp_w_d, exp_b_d, final_g):
    batch, n_lat, d = x.shape
    n_ctx = ctx.shape[1]
    rpb = n_lat + n_ctx
    t = batch * rpb
    depth = mod_w.shape[0]
    geo = dict(rpb=rpb, n_lat=n_lat, batch=batch)

    xs = jnp.concatenate([x, ctx], axis=1).reshape(t, d)
    mod_rows = -(-(batch + 1) // SUBLANE) * SUBLANE
    cvec = jnp.zeros((mod_rows, d), F32).at[:batch].set(c).at[batch].set(c_ctx)
    mod = _modmat(cvec, mod_w, mod_b)

    rope_a = _rope_tables(n_lat, n_ctx, A_HEAD_DIM)
    rope_c = _rope_tables(n_lat, n_ctx, C_HEAD_DIM)
    q_w = A_GROUP * A_HEAD_DIM * (hyb_w_in.shape[2] - 2 * (d // 2)) // ((A_GROUP + 2) * A_HEAD_DIM)
    kv_w = q_w // A_GROUP
    b_w = d // 2
    nh = b_w // B_BLOCK

    mod3s = [mod[l].reshape(mod_rows, 6, d) for l in range(depth)]
    h = _modulate(xs, norm1_g[0], mod3s[0], k=0, **geo)
    for l in range(depth):
        i = l // 2
        mod3 = mod3s[l]
        if l % 2 == 0:
            w_in = hyb_w_in[i].astype(BF16)
            o = 0
            q = _proj(h, w_in[:, o:o + q_w], rpb=rpb, out_dtype=BF16, norm_g=a_q_norm_g[i], rope=rope_a,
                      q_cols=q_w, q_scale=A_HEAD_DIM ** -0.5 * LOG2E)
            o += q_w
            kk = _proj(h, w_in[:, o:o + kv_w], rpb=rpb, out_dtype=BF16, norm_g=a_k_norm_g[i], rope=rope_a)
            o += kv_w
            v = _proj(h, w_in[:, o:o + kv_w], rpb=rpb, out_dtype=BF16)
            o += kv_w
            xb = _proj(h, w_in[:, o:o + b_w], rpb=rpb, out_dtype=F32)
            o += b_w
            yb = _proj(h, w_in[:, o:o + b_w], rpb=rpb, out_dtype=F32)
            att = jnp.zeros((t, q_w), BF16)
            att = _gqa(q, kk, v, att, ctx_queries=False, **geo)
            att = _gqa(q, kk, v, att, ctx_queries=True, **geo)
            gw = jnp.concatenate([b_gate_a_w[i, 0], b_gate_x_w[i, 0], b_gate_a_w[i, 1], b_gate_x_w[i, 1]],
                                 axis=-1).astype(BF16)
            gb = jnp.concatenate([b_gate_a_b[i, 0].reshape(nh, 1, B_BLOCK), b_gate_x_b[i, 0].reshape(nh, 1, B_BLOCK),
                                  b_gate_a_b[i, 1].reshape(nh, 1, B_BLOCK), b_gate_x_b[i, 1].reshape(nh, 1, B_BLOCK)],
                                 axis=-1)
            ry = _rglru(xb, yb, b_conv_w[i], b_conv_b[i], gw, gb, b_lru_lambda[i], **geo)
            w_out = hyb_w_out[i].astype(BF16)
            pairs = [(att, w_out[:q_w]), (ry, w_out[q_w:])]
        else:
            lam_init = 0.8 - 0.6 * math.exp(-0.3 * l)
            w_in = c_w_in[i].astype(BF16)
            qk_w = 2 * (w_in.shape[1] // 3)
            qk = _proj(h, w_in[:, :qk_w], rpb=rpb, out_dtype=BF16, rope=rope_c,
                       q_cols=qk_w // 2, q_scale=C_HEAD_DIM ** -0.5 * LOG2E)
            v = _proj(h, w_in[:, qk_w:], rpb=rpb, out_dtype=BF16)
            lamvec = jnp.stack([c_lambda_q1[i], c_lambda_k1[i], c_lambda_q2[i], c_lambda_k2[i]]).astype(F32)
            att = jnp.zeros((t, v.shape[1]), BF16)
            att = _diff_attn(qk, v, lamvec, c_subln_g[i], att, lam_init=lam_init, ctx_queries=False, **geo)
            att = _diff_attn(qk, v, lamvec, c_subln_g[i], att, lam_init=lam_init, ctx_queries=True, **geo)
            pairs = [(att, c_w_out[i].astype(BF16))]
        xs = _resid_proj(xs, mod3, pairs, k=2, **geo)
        xs, h = _moe(xs, norm2_g[l], mod3, router_w[l], router_b[l], exp_w_gu, exp_b_gu, exp_w_d, exp_b_d,
                     layer=l, nxt=(norm1_g[l + 1], mod3s[l + 1]) if l + 1 < depth else None, **geo)
    return _final_norm(xs.reshape(batch, rpb, d), final_g, n_lat)
```

```python
import functools
import math

import jax
import jax.numpy as jnp
from jax import lax
from jax.experimental import pallas as pl
from jax.experimental.pallas import tpu as pltpu

F32 = jnp.float32
BF16 = jnp.bfloat16

GRID_W = 64
ROPE_THETA = 10000.0
NORM_EPS = 1e-6
A_HEAD_DIM = 128
A_GROUP = 4
B_BLOCK = 128
LRU_C = 8.0
C_HEAD_DIM = 64
TOP_K = 4
SWIGLU_LIMIT = 7.0
SWIGLU_ALPHA = 1.702
LOG2E = math.log2(math.e)

LANE = 128
SUBLANE = 8
VMEM_LIMIT_BYTES = 56 * 1024 * 1024
ATTN_Q_ROWS = 1024
ATTN_KEY_CHUNK = 256
MOE_ROW_TILE = 512
MOE_TOKEN_SPLITS = 2


def _params():
    return pltpu.CompilerParams(vmem_limit_bytes=VMEM_LIMIT_BYTES)


def _tile(n, target, mult=SUBLANE):
    best = None
    for t in range(mult, min(n, target) + 1, mult):
        if n % t == 0:
            best = t
    assert best is not None, (n, target, mult)
    return best


def _ctx_rows(i, bm, rpb, n_lat):
    pos0 = lax.rem(i * bm, rpb)
    row = lax.broadcasted_iota(jnp.int32, (bm, 1), 0)
    return (pos0 + row) >= n_lat


def _mod_row(modb_ref, modc_ref, k, is_ctx):
    return jnp.where(is_ctx, modc_ref[0, k:k + 1, :], modb_ref[0, k:k + 1, :])


def _rms(x, g):
    return x * lax.rsqrt(jnp.mean(x * x, axis=-1, keepdims=True) + NORM_EPS) * g


def _dot(a, b):
    return jnp.dot(a, b, preferred_element_type=F32)


def _dot_nt(a, b):
    return lax.dot_general(a, b, (((1,), (1,)), ((), ())), preferred_element_type=F32)


def _modmat_kernel(c_ref, w_ref, b_ref, o_ref):
    c = c_ref[...]
    s = (c * jax.nn.sigmoid(c)).astype(BF16)
    o_ref[0] = _dot(s, w_ref[0].astype(BF16)) + b_ref[0]


def _modmat(cvec, mod_w, mod_b):
    depth, d, n = mod_w.shape
    r = cvec.shape[0]
    bn = _tile(n, 512, LANE)
    return pl.pallas_call(
        _modmat_kernel,
        grid=(depth, n // bn),
        in_specs=[
            pl.BlockSpec((r, d), lambda l, j: (0, 0)),
            pl.BlockSpec((1, d, bn), lambda l, j: (l, 0, j)),
            pl.BlockSpec((1, 1, bn), lambda l, j: (l, 0, j)),
        ],
        out_specs=pl.BlockSpec((1, r, bn), lambda l, j: (l, 0, j)),
        out_shape=jax.ShapeDtypeStruct((depth, r, n), F32),
        compiler_params=_params(),
        name="modmat",
    )(cvec, mod_w, mod_b.reshape(depth, 1, n))


def _modulate_kernel(x_ref, g_ref, modb_ref, modc_ref, h_ref, *, k, bm, rpb, n_lat):
    is_ctx = _ctx_rows(pl.program_id(0), bm, rpb, n_lat)
    y = _rms(x_ref[...], g_ref[...])
    h = y * (1.0 + _mod_row(modb_ref, modc_ref, k + 1, is_ctx)) + _mod_row(modb_ref, modc_ref, k, is_ctx)
    h_ref[...] = h.astype(BF16)


def _mod_specs(bm, rpb, batch, d):
    return [
        pl.BlockSpec((1, 6, d), lambda i: ((i * bm) // rpb, 0, 0)),
        pl.BlockSpec((1, 6, d), lambda i: (batch, 0, 0)),
    ]


def _modulate(xs, g, mod3, *, k, rpb, n_lat, batch):
    t, d = xs.shape
    bm = _tile(rpb, 256)
    return pl.pallas_call(
        functools.partial(_modulate_kernel, k=k, bm=bm, rpb=rpb, n_lat=n_lat),
        grid=(t // bm,),
        in_specs=[pl.BlockSpec((bm, d), lambda i: (i, 0)), pl.BlockSpec((1, d), lambda i: (0, 0))]
        + _mod_specs(bm, rpb, batch, d),
        out_specs=pl.BlockSpec((bm, d), lambda i: (i, 0)),
        out_shape=jax.ShapeDtypeStruct((t, d), BF16),
        compiler_params=_params(),
        name="modulate",
    )(xs, g.reshape(1, d), mod3, mod3)


def _router_kernel(x_ref, g_ref, modb_ref, modc_ref, rwh_ref, rwl_ref, rb_ref, h_ref, idx_ref, wt_ref,
                   *, k, bm, rpb, n_lat):
    is_ctx = _ctx_rows(pl.program_id(0), bm, rpb, n_lat)
    y = _rms(x_ref[...], g_ref[...])
    h = y * (1.0 + _mod_row(modb_ref, modc_ref, k + 1, is_ctx)) + _mod_row(modb_ref, modc_ref, k, is_ctx)
    h_hi = h.astype(BF16)
    h_lo = (h - h_hi.astype(F32)).astype(BF16)
    h_ref[...] = h_hi
    logits = _dot(h_hi, rwh_ref[...]) + (_dot(h_lo, rwh_ref[...]) + _dot(h_hi, rwl_ref[...])) + rb_ref[...]
    lane = lax.broadcasted_iota(jnp.int32, logits.shape, 1)
    vals, idxs = [], []
    for _ in range(TOP_K):
        m = jnp.max(logits, axis=-1, keepdims=True)
        ik = jnp.min(jnp.where(logits == m, lane, LANE), axis=-1, keepdims=True)
        vals.append(m)
        idxs.append(ik)
        logits = jnp.where(lane == ik, -jnp.inf, logits)
    es = [jnp.exp(v - vals[0]) for v in vals]
    den = es[0] + es[1] + es[2] + es[3]
    idx_out = jnp.zeros(lane.shape, jnp.int32)
    wt_out = jnp.zeros(lane.shape, F32)
    for j in range(TOP_K):
        idx_out = jnp.where(lane == j, idxs[j], idx_out)
        wt_out = jnp.where(lane == j, es[j] / den, wt_out)
    idx_ref[...] = idx_out
    wt_ref[...] = wt_out


def _modulate_route(xs, g, mod3, router_w, router_b, *, k, rpb, n_lat, batch):
    t, d = xs.shape
    e = router_w.shape[1]
    bm = _tile(rpb, 256)
    rw = jnp.zeros((d, LANE), F32).at[:, :e].set(router_w)
    rw_hi = rw.astype(BF16)
    rw_lo = (rw - rw_hi.astype(F32)).astype(BF16)
    rb = jnp.full((1, LANE), -1e30, F32).at[0, :e].set(router_b)
    return pl.pallas_call(
        functools.partial(_router_kernel, k=k, bm=bm, rpb=rpb, n_lat=n_lat),
        grid=(t // bm,),
        in_specs=[pl.BlockSpec((bm, d), lambda i: (i, 0)), pl.BlockSpec((1, d), lambda i: (0, 0))]
        + _mod_specs(bm, rpb, batch, d)
        + [pl.BlockSpec((d, LANE), lambda i: (0, 0))] * 2 + [pl.BlockSpec((1, LANE), lambda i: (0, 0))],
        out_specs=[pl.BlockSpec((bm, d), lambda i: (i, 0)),
                   pl.BlockSpec((bm, LANE), lambda i: (i, 0)),
                   pl.BlockSpec((bm, LANE), lambda i: (i, 0))],
        out_shape=[jax.ShapeDtypeStruct((t, d), BF16),
                   jax.ShapeDtypeStruct((t, LANE), jnp.int32),
                   jax.ShapeDtypeStruct((t, LANE), F32)],
        compiler_params=_params(),
        name="modulate_route",
    )(xs, g.reshape(1, d), mod3, mod3, rw_hi, rw_lo, rb)


def _gelu(x):
    return 0.5 * x * (1.0 + jnp.tanh(math.sqrt(2.0 / math.pi) * (x + 0.044715 * (x * x * x))))


def _proj_kernel(*refs, norm, rope_shift, q_cols, q_scale, bn):
    a_ref, w_ref = refs[0], refs[1]
    o_ref = refs[-1]
    pos = 2
    g_ref = cos_ref = sin_ref = None
    if norm:
        g_ref = refs[pos]
        pos += 1
    if rope_shift:
        cos_ref, sin_ref = refs[pos], refs[pos + 1]
    acc = _dot(a_ref[...], w_ref[...])
    if not norm and not rope_shift:
        o_ref[...] = acc.astype(o_ref.dtype)
        return
    col_scale = jnp.where(pl.program_id(0) * bn < q_cols, q_scale, 1.0)
    lane = lax.broadcasted_iota(jnp.int32, (acc.shape[0], LANE), 1)
    first_half = jnp.bitwise_and(lane, 2 * rope_shift - 1) < rope_shift
    for c in range(bn // LANE):
        y = acc[:, c * LANE:(c + 1) * LANE]
        if norm:
            y = _rms(y, g_ref[...])
        partner = jnp.where(first_half, pltpu.roll(y, LANE - rope_shift, 1), pltpu.roll(y, rope_shift, 1))
        y = (y * cos_ref[...] + partner * sin_ref[...]) * col_scale
        o_ref[:, c * LANE:(c + 1) * LANE] = y.astype(o_ref.dtype)


def _proj(a, w, *, rpb, out_dtype, norm_g=None, rope=None, q_cols=0, q_scale=1.0):
    t, kdim = a.shape
    n = w.shape[1]
    bm = _tile(rpb, 1152, 16)
    bn = _tile(math.gcd(n, q_cols) if q_cols else n, 1024, LANE)
    tiles_per_sample = rpb // bm
    in_specs = [pl.BlockSpec((bm, kdim), lambda j, i: (i, 0)), pl.BlockSpec((kdim, bn), lambda j, i: (0, j))]
    args = [a, w]
    if norm_g is not None:
        in_specs.append(pl.BlockSpec((1, LANE), lambda j, i: (0, 0)))
        args.append(norm_g.reshape(1, LANE).astype(F32))
    rope_shift = 0
    if rope is not None:
        cos, sin, rope_shift = rope
        in_specs += [pl.BlockSpec((bm, LANE), lambda j, i: (i % tiles_per_sample, 0))] * 2
        args += [cos, sin]
    return pl.pallas_call(
        functools.partial(_proj_kernel, norm=norm_g is not None, rope_shift=rope_shift, q_cols=q_cols,
                          q_scale=q_scale, bn=bn),
        grid=(n // bn, t // bm),
        in_specs=in_specs,
        out_specs=pl.BlockSpec((bm, bn), lambda j, i: (i, j)),
        out_shape=jax.ShapeDtypeStruct((t, n), out_dtype),
        compiler_params=_params(),
        name="proj",
    )(*args)


def _resid_kernel(*refs, n_pairs, k, bm, rpb, n_lat):
    x_ref, modb_ref, modc_ref = refs[0], refs[1], refs[2]
    o_ref = refs[-1]
    acc = None
    for p in range(n_pairs):
        part = _dot(refs[3 + 2 * p][...], refs[4 + 2 * p][...])
        acc = part if acc is None else acc + part
    is_ctx = _ctx_rows(pl.program_id(1), bm, rpb, n_lat)
    o_ref[...] = x_ref[...] + _mod_row(modb_ref, modc_ref, k, is_ctx) * acc


def _resid_proj(xs, mod3, pairs, *, k, rpb, n_lat, batch):
    t, d = xs.shape
    bm = _tile(rpb, 768, 16)
    bn = _tile(d, 1024, LANE)
    in_specs = [
        pl.BlockSpec((bm, bn), lambda j, i: (i, j)),
        pl.BlockSpec((1, 6, bn), lambda j, i: ((i * bm) // rpb, 0, j)),
        pl.BlockSpec((1, 6, bn), lambda j, i: (batch, 0, j)),
    ]
    args = [xs, mod3, mod3]
    for a, w in pairs:
        kdim = a.shape[1]
        in_specs += [pl.BlockSpec((bm, kdim), lambda j, i: (i, 0)), pl.BlockSpec((kdim, bn), lambda j, i: (0, j))]
        args += [a, w]
    return pl.pallas_call(
        functools.partial(_resid_kernel, n_pairs=len(pairs), k=k, bm=bm, rpb=rpb, n_lat=n_lat),
        grid=(d // bn, t // bm),
        in_specs=in_specs,
        out_specs=pl.BlockSpec((bm, bn), lambda j, i: (i, j)),
        out_shape=jax.ShapeDtypeStruct((t, d), F32),
        compiler_params=_params(),
        name="resid_proj",
    )(*args)


def _fill_v_ext(v_ref, vext_ref):
    width = v_ref.shape[1]
    vext_ref[:, :width] = v_ref[...]
    vext_ref[:, width:] = jnp.ones((v_ref.shape[0], vext_ref.shape[1] - width), vext_ref.dtype)


def _attend(qs, k_ref, vext_ref, s_refs, p_refs):
    tq = qs[0].shape[0]
    lk = k_ref.shape[0]
    kc = _tile(lk, ATTN_KEY_CHUNK, LANE)

    def scores(q, c, s_ref, lane_max):
        s = _dot_nt(q, k_ref[c * kc:(c + 1) * kc])
        s_ref[:, c * kc:(c + 1) * kc] = s
        for j in range(kc // LANE):
            lane_max = jnp.maximum(lane_max, s[:, j * LANE:(j + 1) * LANE])
        return lane_max

    def exponentials(s_ref, p_ref, c, row_max):
        p_ref[:, c * kc:(c + 1) * kc] = jnp.exp2(s_ref[:, c * kc:(c + 1) * kc] - row_max).astype(BF16)

    lowest = jnp.full((tq, LANE), -jnp.inf, F32)
    lane_max = lowest
    for c in range(lk // kc):
        lane_max = scores(qs[0], c, s_refs[0], lane_max)
    row_max = jnp.max(lane_max, axis=-1, keepdims=True)
    for g in range(len(qs)):
        cur, nxt = g % 2, (g + 1) % 2
        lane_max = lowest
        for c in range(lk // kc):
            if g + 1 < len(qs):
                lane_max = scores(qs[g + 1], c, s_refs[nxt], lane_max)
            exponentials(s_refs[cur], p_refs[cur], c, row_max)
        if g + 1 < len(qs):
            row_max = jnp.max(lane_max, axis=-1, keepdims=True)
        yield _dot(p_refs[cur][...], vext_ref[...])


def _attn_geometry(rpb, n_lat, ctx_queries):
    n_ctx = rpb - n_lat
    if ctx_queries:
        assert n_lat % n_ctx == 0
        return n_ctx, 1, n_ctx, (lambda i: n_lat // n_ctx), n_lat // n_ctx
    tq = _tile(n_lat, ATTN_Q_ROWS)
    return tq, n_lat // tq, rpb, (lambda i: i), 0


def _attn_scratch(tq, lk, width):
    return [pltpu.VMEM((lk, 2 * width), BF16),
            pltpu.VMEM((tq, lk), F32), pltpu.VMEM((tq, lk), F32),
            pltpu.VMEM((tq, lk), BF16), pltpu.VMEM((tq, lk), BF16)]


def _gqa_kernel(q_ref, k_ref, v_ref, prev_ref, o_ref, vext_ref, s0_ref, s1_ref, p0_ref, p1_ref, *, group):
    del prev_ref

    @pl.when(pl.program_id(2) == 0)
    def _():
        _fill_v_ext(v_ref, vext_ref)

    qs = [q_ref[:, g * A_HEAD_DIM:(g + 1) * A_HEAD_DIM] for g in range(group)]
    for g, oe in enumerate(_attend(qs, k_ref, vext_ref, (s0_ref, s1_ref), (p0_ref, p1_ref))):
        o_ref[:, g * A_HEAD_DIM:(g + 1) * A_HEAD_DIM] = (oe[:, :A_HEAD_DIM] / oe[:, A_HEAD_DIM:]).astype(o_ref.dtype)


def _gqa(q, k, v, out, *, rpb, n_lat, batch, ctx_queries):
    t = q.shape[0]
    kvh = k.shape[1] // A_HEAD_DIM
    gw = A_GROUP * A_HEAD_DIM
    tq, nq, lk, q_blk, k_blk = _attn_geometry(rpb, n_lat, ctx_queries)
    per_sample = lambda a: a.reshape(batch, rpb, a.shape[1])
    kernel = functools.partial(_gqa_kernel, group=A_GROUP)
    return pl.pallas_call(
        kernel,
        grid=(batch, kvh, nq),
        in_specs=[
            pl.BlockSpec((None, tq, gw), lambda b, h, i: (b, q_blk(i), h)),
            pl.BlockSpec((None, lk, A_HEAD_DIM), lambda b, h, i: (b, k_blk, h)),
            pl.BlockSpec((None, lk, A_HEAD_DIM), lambda b, h, i: (b, k_blk, h)),
            pl.BlockSpec(memory_space=pl.ANY),
        ],
        out_specs=pl.BlockSpec((None, tq, gw), lambda b, h, i: (b, q_blk(i), h)),
        out_shape=jax.ShapeDtypeStruct((batch, rpb, out.shape[1]), BF16),
        scratch_shapes=_attn_scratch(tq, lk, A_HEAD_DIM),
        input_output_aliases={3: 0},
        compiler_params=_params(),
        name="gqa_ctx" if ctx_queries else "gqa_lat",
    )(per_sample(q), per_sample(k), per_sample(v), per_sample(out)).reshape(t, out.shape[1])


def _diff_kernel(lam_ref, g_ref, q_ref, k_ref, v_ref, prev_ref, o_ref, vext_ref, s0_ref, s1_ref, p0_ref, p1_ref,
                 *, lam_init):
    del prev_ref

    @pl.when(pl.program_id(2) == 0)
    def _():
        _fill_v_ext(v_ref, vext_ref)

    lv = lam_ref[...]
    lam = (jnp.exp(jnp.sum(lv[0:1] * lv[1:2], axis=-1, keepdims=True))
           - jnp.exp(jnp.sum(lv[2:3] * lv[3:4], axis=-1, keepdims=True)) + lam_init)
    q = q_ref[...]
    hw = q.shape[1]
    lane = lax.broadcasted_iota(jnp.int32, q.shape, 1)
    zero = jnp.zeros_like(q)
    qs = [jnp.where(lane < C_HEAD_DIM, q, zero), jnp.where(lane >= C_HEAD_DIM, q, zero)]
    oe1, oe2 = _attend(qs, k_ref, vext_ref, (s0_ref, s1_ref), (p0_ref, p1_ref))
    o = oe1[:, :hw] / oe1[:, hw:] - lam * (oe2[:, :hw] / oe2[:, hw:])
    o_ref[...] = (_rms(o, g_ref[...]) * (1.0 - lam_init)).astype(o_ref.dtype)


def _diff_attn(qk, v, lamvec, subln_g, out, *, lam_init, rpb, n_lat, batch, ctx_queries):
    t = qk.shape[0]
    heads = v.shape[1] // (2 * C_HEAD_DIM)
    hw = 2 * C_HEAD_DIM
    tq, nq, lk, q_blk, k_blk = _attn_geometry(rpb, n_lat, ctx_queries)
    per_sample = lambda a: a.reshape(batch, rpb, a.shape[1])
    kernel = functools.partial(_diff_kernel, lam_init=lam_init)
    return pl.pallas_call(
        kernel,
        grid=(batch, heads, nq),
        in_specs=[
            pl.BlockSpec(lamvec.shape, lambda b, h, i: (0, 0)),
            pl.BlockSpec((1, hw), lambda b, h, i: (0, 0)),
            pl.BlockSpec((None, tq, hw), lambda b, h, i: (b, q_blk(i), h)),
            pl.BlockSpec((None, lk, hw), lambda b, h, i: (b, k_blk, heads + h)),
            pl.BlockSpec((None, lk, hw), lambda b, h, i: (b, k_blk, h)),
            pl.BlockSpec(memory_space=pl.ANY),
        ],
        out_specs=pl.BlockSpec((None, tq, hw), lambda b, h, i: (b, q_blk(i), h)),
        out_shape=jax.ShapeDtypeStruct((batch, rpb, v.shape[1]), BF16),
        scratch_shapes=_attn_scratch(tq, lk, hw),
        input_output_aliases={5: 0},
        compiler_params=_params(),
        name="diff_ctx" if ctx_queries else "diff_lat",
    )(lamvec, subln_g.reshape(1, hw).astype(F32), per_sample(qk), per_sample(qk), per_sample(v),
      per_sample(out)).reshape(t, v.shape[1])


CONV_PAD = SUBLANE
SCAN_UNROLL = 8


def _block_scan(a, u, carry, reverse):
    row = lax.broadcasted_iota(jnp.int32, a.shape, 0)
    for s in (1, 2, 4):
        shift = SUBLANE - s if reverse else s
        valid = (row < SUBLANE - s) if reverse else (row >= s)
        a_prev = pltpu.roll(a, shift, 0)
        u_prev = pltpu.roll(u, shift, 0)
        u = jnp.where(valid, a * u_prev + u, u)
        a = jnp.where(valid, a * a_prev, a)
    h = u + a * carry
    last = h[0:1] if reverse else h[SUBLANE - 1:SUBLANE]
    return h, jnp.broadcast_to(last, h.shape)


def _rglru_kernel(xb_ref, yb_ref, cw_ref, cb_ref, gw_ref, gb_ref, lam_ref, buf_ref, o_ref,
                  xpad_ref, xc_ref, a_ref, u_ref, hf_ref, hb_ref, *, n_lat, n_ctx):
    del buf_ref
    cw = cw_ref[...]
    taps = cw.shape[0]
    zpad = jnp.zeros((CONV_PAD, B_BLOCK), F32)

    def conv_segment(r0, n):
        xpad_ref[0:CONV_PAD] = zpad
        xpad_ref[CONV_PAD:CONV_PAD + n] = xb_ref[r0:r0 + n]
        xpad_ref[CONV_PAD + n:2 * CONV_PAD + n] = zpad
        acc = jnp.broadcast_to(cb_ref[...], (n, B_BLOCK))
        for j in range(taps):
            off = CONV_PAD + j - taps // 2
            acc = acc + cw[j:j + 1] * xpad_ref[off:off + n]
        xc_ref[r0:r0 + n] = acc

    conv_segment(0, n_lat)
    conv_segment(n_lat, n_ctx)

    xc = xc_ref[...]
    gates = _dot(xc.astype(BF16), gw_ref[0]) + gb_ref[0]
    lam = lam_ref[...]
    softplus_neg = jnp.maximum(-lam, 0.0) + jnp.log(1.0 + jnp.exp(-jnp.abs(lam)))
    for d in range(2):
        r = jax.nn.sigmoid(gates[:, (2 * d) * B_BLOCK:(2 * d + 1) * B_BLOCK])
        i = jax.nn.sigmoid(gates[:, (2 * d + 1) * B_BLOCK:(2 * d + 2) * B_BLOCK])
        log_a = -LRU_C * r * softplus_neg[d:d + 1]
        a_ref[d] = jnp.exp(log_a)
        u_ref[d] = jnp.sqrt(1.0 - jnp.exp(2.0 * log_a)) * i * xc

    def scan_segment(r0, n, carry):
        nblk = n // SUBLANE

        def body(b, c):
            cf, cb = c
            rf = pl.multiple_of(r0 + b * SUBLANE, SUBLANE)
            rb = pl.multiple_of(r0 + (nblk - 1 - b) * SUBLANE, SUBLANE)
            hf, cf = _block_scan(a_ref[0, pl.ds(rf, SUBLANE), :], u_ref[0, pl.ds(rf, SUBLANE), :], cf, False)
            hb, cb = _block_scan(a_ref[1, pl.ds(rb, SUBLANE), :], u_ref[1, pl.ds(rb, SUBLANE), :], cb, True)
            hf_ref[pl.ds(rf, SUBLANE), :] = hf
            hb_ref[pl.ds(rb, SUBLANE), :] = hb
            return cf, cb

        return lax.fori_loop(0, nblk, body, carry, unroll=math.gcd(nblk, SCAN_UNROLL))

    zero = jnp.zeros((SUBLANE, B_BLOCK), F32)
    carry = scan_segment(n_lat, n_ctx, (zero, zero))
    scan_segment(0, n_lat, carry)
    o_ref[...] = ((hf_ref[...] + hb_ref[...]) * _gelu(yb_ref[...])).astype(o_ref.dtype)


def _rglru(xb, yb, conv_w, conv_b, gw, gb, lam, out, col0, *, rpb, n_lat, batch):
    t, width = xb.shape
    assert col0 % B_BLOCK == 0
    nh = width // B_BLOCK
    n_ctx = rpb - n_lat
    taps = conv_w.shape[0]
    return pl.pallas_call(
        functools.partial(_rglru_kernel, n_lat=n_lat, n_ctx=n_ctx),
        grid=(batch, nh),
        in_specs=[
            pl.BlockSpec((rpb, B_BLOCK), lambda b, h: (b, h)),
            pl.BlockSpec((rpb, B_BLOCK), lambda b, h: (b, h)),
            pl.BlockSpec((taps, B_BLOCK), lambda b, h: (0, h)),
            pl.BlockSpec((1, B_BLOCK), lambda b, h: (0, h)),
            pl.BlockSpec((1, B_BLOCK, 4 * B_BLOCK), lambda b, h: (h, 0, 0)),
            pl.BlockSpec((1, 1, 4 * B_BLOCK), lambda b, h: (h, 0, 0)),
            pl.BlockSpec((2, B_BLOCK), lambda b, h: (0, h)),
            pl.BlockSpec(memory_space=pl.ANY),
        ],
        out_specs=pl.BlockSpec((rpb, B_BLOCK), lambda b, h: (b, col0 // B_BLOCK + h)),
        out_shape=jax.ShapeDtypeStruct(out.shape, BF16),
        input_output_aliases={7: 0},
        scratch_shapes=[
            pltpu.VMEM((n_lat + 2 * CONV_PAD, B_BLOCK), F32),
            pltpu.VMEM((rpb, B_BLOCK), F32),
            pltpu.VMEM((2, rpb, B_BLOCK), F32),
            pltpu.VMEM((2, rpb, B_BLOCK), F32),
            pltpu.VMEM((rpb, B_BLOCK), F32),
            pltpu.VMEM((rpb, B_BLOCK), F32),
        ],
        compiler_params=_params(),
        name="rglru",
    )(xb, yb, conv_w, conv_b.reshape(1, width), gw, gb, lam, out)


CAST_ROWS = 256


def _expert_weight_copies(wgu_hbm, wd_hbm, wgu_f32, wd_f32, sem, layer, expert, slot):
    return (pltpu.make_async_copy(wgu_hbm.at[layer, expert], wgu_f32.at[slot], sem.at[slot, 0]),
            pltpu.make_async_copy(wd_hbm.at[layer, expert], wd_f32.at[slot], sem.at[slot, 1]))


def _expert_kernel(te_ref, nx_ref, sl_ref, nu_ref, x_ref, wgu_hbm, bgu_ref, wd_hbm, bd_ref, o_ref,
                   wgu_f32, wd_f32, wgu_bf, wd_bf, sem, *, f, layer):
    j = pl.program_id(0)
    expert = te_ref[j]
    slot = sl_ref[j]
    copies = functools.partial(_expert_weight_copies, wgu_hbm, wd_hbm, wgu_f32, wd_f32, sem, layer)

    @pl.when(j == 0)
    def _():
        for c in copies(expert, slot):
            c.start()

    @pl.when((j == 0) | (expert != te_ref[jnp.maximum(j - 1, 0)]))
    def _():
        for c in copies(expert, slot):
            c.wait()
        nxt = nx_ref[j]

        @pl.when(nxt >= 0)
        def _():
            for c in copies(nxt, 1 - slot):
                c.start()

        for r in range(0, wgu_bf.shape[0], CAST_ROWS):
            wgu_bf[r:r + CAST_ROWS] = wgu_f32[slot, r:r + CAST_ROWS].astype(BF16)
        for r in range(0, wd_bf.shape[0], CAST_ROWS):
            wd_bf[r:r + CAST_ROWS] = wd_f32[slot, r:r + CAST_ROWS].astype(BF16)

    @pl.when(j < nu_ref[0])
    def _():
        gu = _dot(x_ref[...], wgu_bf[...]) + bgu_ref[0, 0]
        gate = jnp.minimum(gu[:, :f], SWIGLU_LIMIT)
        up = jnp.clip(gu[:, f:], -SWIGLU_LIMIT, SWIGLU_LIMIT)
        act = (up + 1.0) * gate * jax.nn.sigmoid(gate * SWIGLU_ALPHA)
        o_ref[...] = (_dot(act.astype(BF16), wd_bf[...]) + bd_ref[0, 0]).astype(o_ref.dtype)

    @pl.when(j >= nu_ref[0])
    def _():
        o_ref[...] = jnp.zeros(o_ref.shape, o_ref.dtype)


def _experts(x_sorted, tile_expert, n_used, w_gu, b_gu, w_d, b_d, *, layer, tm):
    r, d = x_sorted.shape
    depth, e, _, f2 = w_gu.shape
    f = f2 // 2
    assert d % CAST_ROWS == 0 and f % CAST_ROWS == 0
    n_tiles = tile_expert.shape[0]
    first = jnp.concatenate([jnp.ones((1,), jnp.bool_), tile_expert[1:] != tile_expert[:-1]])
    slot = ((jnp.cumsum(first.astype(jnp.int32)) - 1) % 2).astype(jnp.int32)
    later = jnp.where(tile_expert[None, :] > tile_expert[:, None], tile_expert[None, :], e)
    next_expert = jnp.min(later, axis=1)
    next_expert = jnp.where(next_expert < e, next_expert, -1).astype(jnp.int32)
    grid_spec = pltpu.PrefetchScalarGridSpec(
        num_scalar_prefetch=4,
        grid=(n_tiles,),
        in_specs=[
            pl.BlockSpec((tm, d), lambda j, *_: (j, 0)),
            pl.BlockSpec(memory_space=pl.ANY),
            pl.BlockSpec((1, 1, 1, f2), lambda j, te, *_: (layer, te[j], 0, 0)),
            pl.BlockSpec(memory_space=pl.ANY),
            pl.BlockSpec((1, 1, 1, d), lambda j, te, *_: (layer, te[j], 0, 0)),
        ],
        out_specs=pl.BlockSpec((tm, d), lambda j, *_: (j, 0)),
        scratch_shapes=[pltpu.VMEM((2, d, f2), F32), pltpu.VMEM((2, f, d), F32),
                        pltpu.VMEM((d, f2), BF16), pltpu.VMEM((f, d), BF16),
                        pltpu.SemaphoreType.DMA((2, 2))],
    )
    assert r == n_tiles * tm
    return pl.pallas_call(
        functools.partial(_expert_kernel, f=f, layer=layer),
        grid_spec=grid_spec,
        out_shape=jax.ShapeDtypeStruct((r, d), BF16),
        compiler_params=_params(),
        name="experts",
    )(tile_expert, next_expert, slot, n_used, x_sorted, w_gu, b_gu.reshape(depth, e, 1, f2), w_d,
      b_d.reshape(depth, e, 1, d))


def _combine_kernel(y_ref, wt_ref, x_ref, modb_ref, modc_ref, *rest, k, bm, rpb, n_lat, tile0, with_next):
    is_ctx = _ctx_rows(pl.program_id(0) + tile0, bm, rpb, n_lat)
    wt = wt_ref[...]
    y = wt[:, 0:1] * y_ref[0].astype(F32)
    for j in range(1, TOP_K):
        y = y + wt[:, j:j + 1] * y_ref[j].astype(F32)
    x_new = x_ref[...] + _mod_row(modb_ref, modc_ref, k, is_ctx) * y
    if not with_next:
        (o_ref,) = rest
        o_ref[...] = x_new
        return
    g_ref, nmodb_ref, nmodc_ref, _, o_ref, h_ref = rest
    o_ref[...] = x_new
    h = _rms(x_new, g_ref[...]) * (1.0 + _mod_row(nmodb_ref, nmodc_ref, 1, is_ctx)) + _mod_row(nmodb_ref, nmodc_ref, 0, is_ctx)
    h_ref[...] = h.astype(BF16)


def _combine(y4, wt, xs, mod3, *, row0, k, rpb, n_lat, batch, nxt=None):
    t, d = xs.shape
    rows = y4.shape[1]
    bm = _tile(rpb, 256)
    assert row0 % bm == 0 and rows % bm == 0
    tile0 = row0 // bm
    row_spec = lambda width: pl.BlockSpec((bm, width), lambda i: (i + tile0, 0))
    mod_specs = [pl.BlockSpec((1, 6, d), lambda i: (((i + tile0) * bm) // rpb, 0, 0)),
                 pl.BlockSpec((1, 6, d), lambda i: (batch, 0, 0))]
    in_specs = [pl.BlockSpec((TOP_K, bm, d), lambda i: (0, i, 0)), row_spec(LANE), row_spec(d)] + mod_specs
    args = [y4, wt, xs, mod3, mod3]
    out_specs, out_shape, aliases = [row_spec(d)], [jax.ShapeDtypeStruct((t, d), F32)], {2: 0}
    if nxt is not None:
        g_next, mod3_next, h = nxt
        in_specs += [pl.BlockSpec((1, d), lambda i: (0, 0))] + mod_specs + [pl.BlockSpec(memory_space=pl.ANY)]
        args += [g_next.reshape(1, d), mod3_next, mod3_next, h]
        out_specs.append(row_spec(d))
        out_shape.append(jax.ShapeDtypeStruct((t, d), BF16))
        aliases[len(args) - 1] = 1
    out = pl.pallas_call(
        functools.partial(_combine_kernel, k=k, bm=bm, rpb=rpb, n_lat=n_lat, tile0=tile0, with_next=nxt is not None),
        grid=(rows // bm,),
        in_specs=in_specs,
        out_specs=out_specs,
        out_shape=out_shape,
        input_output_aliases=aliases,
        compiler_params=_params(),
        name="combine",
    )(*args)
    return (out[0], out[1]) if nxt is not None else (out[0], None)


def _dispatch_plan(top_idx, n_experts, tm):
    t = top_idx.shape[0]
    chosen = (top_idx[:, :, None] == jnp.arange(n_experts, dtype=jnp.int32)[None, None, :]).any(axis=1)
    chosen = chosen.astype(jnp.int32)
    before = jnp.cumsum(chosen, axis=0) - chosen
    counts = before[-1] + chosen[-1]
    padded = ((counts + tm - 1) // tm) * tm
    group_end = jnp.cumsum(padded)
    group_start = group_end - padded
    dest = group_start[top_idx] + jnp.take_along_axis(before, top_idx, axis=1)
    n_rows = TOP_K * t + n_experts * tm
    n_tiles = n_rows // tm
    tile_expert = jnp.minimum(
        jnp.sum(jnp.arange(n_tiles, dtype=jnp.int32)[:, None] >= (group_end // tm)[None, :], axis=1), n_experts - 1
    ).astype(jnp.int32)
    n_used = (group_end[-1] // tm).astype(jnp.int32).reshape(1)
    pad = jnp.arange(tm, dtype=jnp.int32)[None, :]
    pad_row = (group_start + counts)[:, None] + pad
    pad_key = jnp.where(pad_row < group_end[:, None], pad_row, n_rows + pad_row).reshape(-1)
    keys = jnp.concatenate([dest.reshape(-1), pad_key])
    vals = jnp.concatenate([jnp.repeat(jnp.arange(t, dtype=jnp.int32), TOP_K),
                            jnp.arange(n_experts * tm, dtype=jnp.int32) % t])
    _, row_token = lax.sort((keys, vals), num_keys=1)
    return dest, row_token, tile_expert, n_used


def _moe(xs, g, mod3, router_w, router_b, w_gu, b_gu, w_d, b_d, *, layer, rpb, n_lat, batch, nxt=None):
    t, d = xs.shape
    e = router_w.shape[1]
    splits = MOE_TOKEN_SPLITS if batch % MOE_TOKEN_SPLITS == 0 else 1
    ts = t // splits
    h2, idx, wt = _modulate_route(xs, g, mod3, router_w, router_b, k=3, rpb=rpb, n_lat=n_lat, batch=batch)
    h = h2
    plans = [_dispatch_plan(idx[s * ts:(s + 1) * ts, :TOP_K], e, MOE_ROW_TILE) for s in range(splits)]
    sorted_x = [jnp.take(h2, row_token + s * ts, axis=0, mode="clip") for s, (_, row_token, _, _) in enumerate(plans)]
    for s, (dest, _, tile_expert, n_used) in enumerate(plans):
        rows = _experts(sorted_x[s], tile_expert, n_used, w_gu, b_gu, w_d, b_d, layer=layer, tm=MOE_ROW_TILE)
        y4 = jnp.take(rows, dest.T.reshape(-1), axis=0, mode="clip").reshape(TOP_K, ts, d)
        xs, h = _combine(y4, wt, xs, mod3, row0=s * ts, k=5, rpb=rpb, n_lat=n_lat, batch=batch,
                         nxt=None if nxt is None else (*nxt, h))
    return xs, h


def _final_kernel(x_ref, g_ref, o_ref):
    o_ref[0] = _rms(x_ref[0], g_ref[...])


def _final_norm(xs3, g, n_lat):
    batch, rpb, d = xs3.shape
    bm = _tile(n_lat, 256)
    return pl.pallas_call(
        _final_kernel,
        grid=(batch, n_lat // bm),
        in_specs=[pl.BlockSpec((1, bm, d), lambda b, i: (b, i, 0)), pl.BlockSpec((1, d), lambda b, i: (0, 0))],
        out_specs=pl.BlockSpec((1, bm, d), lambda b, i: (b, i, 0)),
        out_shape=jax.ShapeDtypeStruct((batch, n_lat, d), F32),
        compiler_params=_params(),
        name="final_norm",
    )(xs3, g.reshape(1, d))


def _rope_tables(n_lat, n_ctx, head_dim):
    n_freq = head_dim // 4
    inv_freq = ROPE_THETA ** (-jnp.arange(n_freq, dtype=F32) / n_freq)
    tok = jnp.arange(n_lat, dtype=jnp.int32)
    ang_r = (tok // GRID_W).astype(F32)[:, None] * inv_freq
    ang_c = (tok % GRID_W).astype(F32)[:, None] * inv_freq
    cos = jnp.concatenate([jnp.cos(ang_r), jnp.cos(ang_r), jnp.cos(ang_c), jnp.cos(ang_c)], axis=-1)
    sin = jnp.concatenate([-jnp.sin(ang_r), jnp.sin(ang_r), -jnp.sin(ang_c), jnp.sin(ang_c)], axis=-1)
    reps = LANE // head_dim
    cos = jnp.concatenate([jnp.tile(cos, (1, reps)), jnp.ones((n_ctx, LANE), F32)], axis=0)
    sin = jnp.concatenate([jnp.tile(sin, (1, reps)), jnp.zeros((n_ctx, LANE), F32)], axis=0)
    return cos, sin, n_freq


def kernel(x, c, ctx, c_ctx, mod_w, mod_b, norm1_g, norm2_g, hyb_w_in, hyb_w_out, a_q_norm_g, a_k_norm_g, b_conv_w, b_conv_b, b_gate_a_w, b_gate_a_b, b_gate_x_w, b_gate_x_b, b_lru_lambda, c_w_in, c_w_out, c_lambda_q1, c_lambda_k1, c_lambda_q2, c_lambda_k2, c_subln_g, router_w, router_b, exp_w_gu, exp_b_gu, exp_w_d, exp_b_d, final_g):
    batch, n_lat, d = x.shape
    n_ctx = ctx.shape[1]
    rpb = n_lat + n_ctx
    t = batch * rpb
    depth = mod_w.shape[0]
    geo = dict(rpb=rpb, n_lat=n_lat, batch=batch)

    xs = jnp.concatenate([x, ctx], axis=1).reshape(t, d)
    mod_rows = -(-(batch + 1) // SUBLANE) * SUBLANE
    cvec = jnp.zeros((mod_rows, d), F32).at[:batch].set(c).at[batch].set(c_ctx)
    mod = _modmat(cvec, mod_w, mod_b)

    rope_a = _rope_tables(n_lat, n_ctx, A_HEAD_DIM)
    rope_c = _rope_tables(n_lat, n_ctx, C_HEAD_DIM)
    q_w = A_GROUP * A_HEAD_DIM * (hyb_w_in.shape[2] - 2 * (d // 2)) // ((A_GROUP + 2) * A_HEAD_DIM)
    kv_w = q_w // A_GROUP
    b_w = d // 2
    nh = b_w // B_BLOCK

    mod3s = [mod[l].reshape(mod_rows, 6, d) for l in range(depth)]
    h = _modulate(xs, norm1_g[0], mod3s[0], k=0, **geo)
    for l in range(depth):
        i = l // 2
        mod3 = mod3s[l]
        if l % 2 == 0:
            w_in = hyb_w_in[i].astype(BF16)
            o = 0
            q = _proj(h, w_in[:, o:o + q_w], rpb=rpb, out_dtype=BF16, norm_g=a_q_norm_g[i], rope=rope_a,
                      q_cols=q_w, q_scale=A_HEAD_DIM ** -0.5 * LOG2E)
            o += q_w
            kk = _proj(h, w_in[:, o:o + kv_w], rpb=rpb, out_dtype=BF16, norm_g=a_k_norm_g[i], rope=rope_a)
            o += kv_w
            v = _proj(h, w_in[:, o:o + kv_w], rpb=rpb, out_dtype=BF16)
            o += kv_w
            xb = _proj(h, w_in[:, o:o + b_w], rpb=rpb, out_dtype=F32)
            o += b_w
            yb = _proj(h, w_in[:, o:o + b_w], rpb=rpb, out_dtype=F32)
            mix = _gqa(q, kk, v, h, ctx_queries=False, **geo)
            mix = _gqa(q, kk, v, mix, ctx_queries=True, **geo)
            gw = jnp.concatenate([b_gate_a_w[i, 0], b_gate_x_w[i, 0], b_gate_a_w[i, 1], b_gate_x_w[i, 1]],
                                 axis=-1).astype(BF16)
            gb = jnp.concatenate([b_gate_a_b[i, 0].reshape(nh, 1, B_BLOCK), b_gate_x_b[i, 0].reshape(nh, 1, B_BLOCK),
                                  b_gate_a_b[i, 1].reshape(nh, 1, B_BLOCK), b_gate_x_b[i, 1].reshape(nh, 1, B_BLOCK)],
                                 axis=-1)
            mix = _rglru(xb, yb, b_conv_w[i], b_conv_b[i], gw, gb, b_lru_lambda[i], mix, q_w, **geo)
            pairs = [(mix, hyb_w_out[i].astype(BF16))]
        else:
            lam_init = 0.8 - 0.6 * math.exp(-0.3 * l)
            w_in = c_w_in[i].astype(BF16)
            qk_w = 2 * (w_in.shape[1] // 3)
            qk = _proj(h, w_in[:, :qk_w], rpb=rpb, out_dtype=BF16, rope=rope_c,
                       q_cols=qk_w // 2, q_scale=C_HEAD_DIM ** -0.5 * LOG2E)
            v = _proj(h, w_in[:, qk_w:], rpb=rpb, out_dtype=BF16)
            lamvec = jnp.stack([c_lambda_q1[i], c_lambda_k1[i], c_lambda_q2[i], c_lambda_k2[i]]).astype(F32)
            att = _diff_attn(qk, v, lamvec, c_subln_g[i], h, lam_init=lam_init, ctx_queries=False, **geo)
            att = _diff_attn(qk, v, lamvec, c_subln_g[i], att, lam_init=lam_init, ctx_queries=True, **geo)
            pairs = [(att, c_w_out[i].astype(BF16))]
        xs = _resid_proj(xs, mod3, pairs, k=2, **geo)
        xs, h = _moe(xs, norm2_g[l], mod3, router_w[l], router_b[l], exp_w_gu, exp_b_gu, exp_w_d, exp_b_d,
                     layer=l, nxt=(norm1_g[l + 1], mod3s[l + 1]) if l + 1 < depth else None, **geo)
    return _final_norm(xs.reshape(batch, rpb, d), final_g, n_lat)
```

```python
import functools
import math

import jax
import jax.numpy as jnp
from jax import lax
from jax.experimental import pallas as pl
from jax.experimental.pallas import tpu as pltpu

F32 = jnp.float32
BF16 = jnp.bfloat16

GRID_W = 64
ROPE_THETA = 10000.0
NORM_EPS = 1e-6
A_HEAD_DIM = 128
A_GROUP = 4
B_BLOCK = 128
LRU_C = 8.0
C_HEAD_DIM = 64
TOP_K = 4
SWIGLU_LIMIT = 7.0
SWIGLU_ALPHA = 1.702
LOG2E = math.log2(math.e)

LANE = 128
SUBLANE = 8
VMEM_LIMIT_BYTES = 56 * 1024 * 1024
ATTN_Q_ROWS = 1024
ATTN_KEY_CHUNK = 256
PROJ_COLS = 2048
PROJ_TILE_ELEMS = 1152 * 1024
PROJ_EPILOGUE_COLS = 256
MOE_ROW_TILE = 512
MOE_TOKEN_SPLITS = 2


def _params():
    return pltpu.CompilerParams(vmem_limit_bytes=VMEM_LIMIT_BYTES)


def _tile(n, target, mult=SUBLANE):
    best = None
    for t in range(mult, min(n, target) + 1, mult):
        if n % t == 0:
            best = t
    assert best is not None, (n, target, mult)
    return best


def _ctx_rows(i, bm, rpb, n_lat):
    pos0 = lax.rem(i * bm, rpb)
    row = lax.broadcasted_iota(jnp.int32, (bm, 1), 0)
    return (pos0 + row) >= n_lat


def _mod_row(modb_ref, modc_ref, k, is_ctx):
    return jnp.where(is_ctx, modc_ref[0, k:k + 1, :], modb_ref[0, k:k + 1, :])


def _rms(x, g):
    return x * lax.rsqrt(jnp.mean(x * x, axis=-1, keepdims=True) + NORM_EPS) * g


def _dot(a, b):
    return jnp.dot(a, b, preferred_element_type=F32)


def _dot_nt(a, b):
    return lax.dot_general(a, b, (((1,), (1,)), ((), ())), preferred_element_type=F32)


def _modmat_kernel(c_ref, w_ref, b_ref, o_ref):
    c = c_ref[...]
    s = (c * jax.nn.sigmoid(c)).astype(BF16)
    o_ref[0] = _dot(s, w_ref[0].astype(BF16)) + b_ref[0]


def _modmat(cvec, mod_w, mod_b):
    depth, d, n = mod_w.shape
    r = cvec.shape[0]
    bn = _tile(n, 512, LANE)
    return pl.pallas_call(
        _modmat_kernel,
        grid=(depth, n // bn),
        in_specs=[
            pl.BlockSpec((r, d), lambda l, j: (0, 0)),
            pl.BlockSpec((1, d, bn), lambda l, j: (l, 0, j)),
            pl.BlockSpec((1, 1, bn), lambda l, j: (l, 0, j)),
        ],
        out_specs=pl.BlockSpec((1, r, bn), lambda l, j: (l, 0, j)),
        out_shape=jax.ShapeDtypeStruct((depth, r, n), F32),
        compiler_params=_params(),
        name="modmat",
    )(cvec, mod_w, mod_b.reshape(depth, 1, n))


def _modulate_kernel(x_ref, g_ref, modb_ref, modc_ref, h_ref, *, k, bm, rpb, n_lat):
    is_ctx = _ctx_rows(pl.program_id(0), bm, rpb, n_lat)
    y = _rms(x_ref[...], g_ref[...])
    h = y * (1.0 + _mod_row(modb_ref, modc_ref, k + 1, is_ctx)) + _mod_row(modb_ref, modc_ref, k, is_ctx)
    h_ref[...] = h.astype(BF16)


def _mod_specs(bm, rpb, batch, d):
    return [
        pl.BlockSpec((1, 6, d), lambda i: ((i * bm) // rpb, 0, 0)),
        pl.BlockSpec((1, 6, d), lambda i: (batch, 0, 0)),
    ]


def _modulate(xs, g, mod3, *, k, rpb, n_lat, batch):
    t, d = xs.shape
    bm = _tile(rpb, 256)
    return pl.pallas_call(
        functools.partial(_modulate_kernel, k=k, bm=bm, rpb=rpb, n_lat=n_lat),
        grid=(t // bm,),
        in_specs=[pl.BlockSpec((bm, d), lambda i: (i, 0)), pl.BlockSpec((1, d), lambda i: (0, 0))]
        + _mod_specs(bm, rpb, batch, d),
        out_specs=pl.BlockSpec((bm, d), lambda i: (i, 0)),
        out_shape=jax.ShapeDtypeStruct((t, d), BF16),
        compiler_params=_params(),
        name="modulate",
    )(xs, g.reshape(1, d), mod3, mod3)


def _router_kernel(x_ref, g_ref, modb_ref, modc_ref, rwh_ref, rwl_ref, rb_ref, h_ref, idx_ref, wt_ref,
                   *, k, bm, rpb, n_lat):
    is_ctx = _ctx_rows(pl.program_id(0), bm, rpb, n_lat)
    y = _rms(x_ref[...], g_ref[...])
    h = y * (1.0 + _mod_row(modb_ref, modc_ref, k + 1, is_ctx)) + _mod_row(modb_ref, modc_ref, k, is_ctx)
    h_hi = h.astype(BF16)
    h_lo = (h - h_hi.astype(F32)).astype(BF16)
    h_ref[...] = h_hi
    logits = _dot(h_hi, rwh_ref[...]) + (_dot(h_lo, rwh_ref[...]) + _dot(h_hi, rwl_ref[...])) + rb_ref[...]
    lane = lax.broadcasted_iota(jnp.int32, logits.shape, 1)
    vals, idxs = [], []
    for _ in range(TOP_K):
        m = jnp.max(logits, axis=-1, keepdims=True)
        ik = jnp.min(jnp.where(logits == m, lane, LANE), axis=-1, keepdims=True)
        vals.append(m)
        idxs.append(ik)
        logits = jnp.where(lane == ik, -jnp.inf, logits)
    es = [jnp.exp(v - vals[0]) for v in vals]
    den = es[0] + es[1] + es[2] + es[3]
    idx_out = jnp.zeros(lane.shape, jnp.int32)
    wt_out = jnp.zeros(lane.shape, F32)
    for j in range(TOP_K):
        idx_out = jnp.where(lane == j, idxs[j], idx_out)
        wt_out = jnp.where(lane == j, es[j] / den, wt_out)
    idx_ref[...] = idx_out
    wt_ref[...] = wt_out


def _modulate_route(xs, g, mod3, router_w, router_b, *, k, rpb, n_lat, batch):
    t, d = xs.shape
    e = router_w.shape[1]
    bm = _tile(rpb, 256)
    rw = jnp.zeros((d, LANE), F32).at[:, :e].set(router_w)
    rw_hi = rw.astype(BF16)
    rw_lo = (rw - rw_hi.astype(F32)).astype(BF16)
    rb = jnp.full((1, LANE), -1e30, F32).at[0, :e].set(router_b)
    return pl.pallas_call(
        functools.partial(_router_kernel, k=k, bm=bm, rpb=rpb, n_lat=n_lat),
        grid=(t // bm,),
        in_specs=[pl.BlockSpec((bm, d), lambda i: (i, 0)), pl.BlockSpec((1, d), lambda i: (0, 0))]
        + _mod_specs(bm, rpb, batch, d)
        + [pl.BlockSpec((d, LANE), lambda i: (0, 0))] * 2 + [pl.BlockSpec((1, LANE), lambda i: (0, 0))],
        out_specs=[pl.BlockSpec((bm, d), lambda i: (i, 0)),
                   pl.BlockSpec((bm, LANE), lambda i: (i, 0)),
                   pl.BlockSpec((bm, LANE), lambda i: (i, 0))],
        out_shape=[jax.ShapeDtypeStruct((t, d), BF16),
                   jax.ShapeDtypeStruct((t, LANE), jnp.int32),
                   jax.ShapeDtypeStruct((t, LANE), F32)],
        compiler_params=_params(),
        name="modulate_route",
    )(xs, g.reshape(1, d), mod3, mod3, rw_hi, rw_lo, rb)


def _gelu(x):
    return 0.5 * x * (1.0 + jnp.tanh(math.sqrt(2.0 / math.pi) * (x + 0.044715 * (x * x * x))))


def _proj_kernel(*refs, norm, rope_shift, q_cols, q_scale, bn):
    a_ref, w_ref = refs[0], refs[1]
    o_ref = refs[-1]
    pos = 2
    g_ref = cos_ref = sin_ref = None
    if norm:
        g_ref = refs[pos]
        pos += 1
    if rope_shift:
        cos_ref, sin_ref = refs[pos], refs[pos + 1]
    if not norm and not rope_shift:
        o_ref[...] = _dot(a_ref[...], w_ref[...]).astype(o_ref.dtype)
        return
    col_scale = jnp.where(pl.program_id(0) * bn < q_cols, q_scale, 1.0)
    lane = lax.broadcasted_iota(jnp.int32, (a_ref.shape[0], LANE), 1)
    first_half = jnp.bitwise_and(lane, 2 * rope_shift - 1) < rope_shift
    step = math.gcd(bn, PROJ_EPILOGUE_COLS)
    for c0 in range(0, bn, step):
        acc = _dot(a_ref[...], w_ref[:, c0:c0 + step])
        for c in range(0, step, LANE):
            y = acc[:, c:c + LANE]
            if norm:
                y = _rms(y, g_ref[...])
            partner = jnp.where(first_half, pltpu.roll(y, LANE - rope_shift, 1), pltpu.roll(y, rope_shift, 1))
            y = (y * cos_ref[...] + partner * sin_ref[...]) * col_scale
            o_ref[:, c0 + c:c0 + c + LANE] = y.astype(o_ref.dtype)


def _proj(a, w, *, rpb, out_dtype, norm_g=None, rope=None, q_cols=0, q_scale=1.0):
    t, kdim = a.shape
    n = w.shape[1]
    bn = _tile(math.gcd(n, q_cols) if q_cols else n, PROJ_COLS, LANE)
    bm = _tile(rpb, PROJ_TILE_ELEMS // bn, 16)
    tiles_per_sample = rpb // bm
    in_specs = [pl.BlockSpec((bm, kdim), lambda j, i: (i, 0)), pl.BlockSpec((kdim, bn), lambda j, i: (0, j))]
    args = [a, w]
    if norm_g is not None:
        in_specs.append(pl.BlockSpec((1, LANE), lambda j, i: (0, 0)))
        args.append(norm_g.reshape(1, LANE).astype(F32))
    rope_shift = 0
    if rope is not None:
        cos, sin, rope_shift = rope
        in_specs += [pl.BlockSpec((bm, LANE), lambda j, i: (i % tiles_per_sample, 0))] * 2
        args += [cos, sin]
    return pl.pallas_call(
        functools.partial(_proj_kernel, norm=norm_g is not None, rope_shift=rope_shift, q_cols=q_cols,
                          q_scale=q_scale, bn=bn),
        grid=(n // bn, t // bm),
        in_specs=in_specs,
        out_specs=pl.BlockSpec((bm, bn), lambda j, i: (i, j)),
        out_shape=jax.ShapeDtypeStruct((t, n), out_dtype),
        compiler_params=_params(),
        name="proj",
    )(*args)


def _resid_kernel(*refs, n_pairs, k, bm, rpb, n_lat):
    x_ref, modb_ref, modc_ref = refs[0], refs[1], refs[2]
    o_ref = refs[-1]
    acc = None
    for p in range(n_pairs):
        part = _dot(refs[3 + 2 * p][...], refs[4 + 2 * p][...])
        acc = part if acc is None else acc + part
    is_ctx = _ctx_rows(pl.program_id(1), bm, rpb, n_lat)
    o_ref[...] = x_ref[...] + _mod_row(modb_ref, modc_ref, k, is_ctx) * acc


def _resid_proj(xs, mod3, pairs, *, k, rpb, n_lat, batch):
    t, d = xs.shape
    bm = _tile(rpb, 768, 16)
    bn = _tile(d, 1024, LANE)
    in_specs = [
        pl.BlockSpec((bm, bn), lambda j, i: (i, j)),
        pl.BlockSpec((1, 6, bn), lambda j, i: ((i * bm) // rpb, 0, j)),
        pl.BlockSpec((1, 6, bn), lambda j, i: (batch, 0, j)),
    ]
    args = [xs, mod3, mod3]
    for a, w in pairs:
        kdim = a.shape[1]
        in_specs += [pl.BlockSpec((bm, kdim), lambda j, i: (i, 0)), pl.BlockSpec((kdim, bn), lambda j, i: (0, j))]
        args += [a, w]
    return pl.pallas_call(
        functools.partial(_resid_kernel, n_pairs=len(pairs), k=k, bm=bm, rpb=rpb, n_lat=n_lat),
        grid=(d // bn, t // bm),
        in_specs=in_specs,
        out_specs=pl.BlockSpec((bm, bn), lambda j, i: (i, j)),
        out_shape=jax.ShapeDtypeStruct((t, d), F32),
        compiler_params=_params(),
        name="resid_proj",
    )(*args)


def _fill_v_ext(v_ref, vext_ref):
    width = v_ref.shape[1]
    vext_ref[:, :width] = v_ref[...]
    vext_ref[:, width:] = jnp.ones((v_ref.shape[0], vext_ref.shape[1] - width), vext_ref.dtype)


def _attend(qs, k_ref, vext_ref, s_refs, p_refs):
    tq = qs[0].shape[0]
    lk = k_ref.shape[0]
    kc = _tile(lk, ATTN_KEY_CHUNK, LANE)

    def scores(q, c, s_ref, lane_max):
        s = _dot_nt(q, k_ref[c * kc:(c + 1) * kc])
        s_ref[:, c * kc:(c + 1) * kc] = s
        for j in range(kc // LANE):
            lane_max = jnp.maximum(lane_max, s[:, j * LANE:(j + 1) * LANE])
        return lane_max

    def exponentials(s_ref, p_ref, c, row_max):
        p_ref[:, c * kc:(c + 1) * kc] = jnp.exp2(s_ref[:, c * kc:(c + 1) * kc] - row_max).astype(BF16)

    lowest = jnp.full((tq, LANE), -jnp.inf, F32)
    lane_max = lowest
    for c in range(lk // kc):
        lane_max = scores(qs[0], c, s_refs[0], lane_max)
    row_max = jnp.max(lane_max, axis=-1, keepdims=True)
    for g in range(len(qs)):
        cur, nxt = g % 2, (g + 1) % 2
        lane_max = lowest
        for c in range(lk // kc):
            if g + 1 < len(qs):
                lane_max = scores(qs[g + 1], c, s_refs[nxt], lane_max)
            exponentials(s_refs[cur], p_refs[cur], c, row_max)
        if g + 1 < len(qs):
            row_max = jnp.max(lane_max, axis=-1, keepdims=True)
        yield _dot(p_refs[cur][...], vext_ref[...])


def _attn_geometry(rpb, n_lat, ctx_queries):
    n_ctx = rpb - n_lat
    if ctx_queries:
        assert n_lat % n_ctx == 0
        return n_ctx, 1, n_ctx, (lambda i: n_lat // n_ctx), n_lat // n_ctx
    tq = _tile(n_lat, ATTN_Q_ROWS)
    return tq, n_lat // tq, rpb, (lambda i: i), 0


def _attn_scratch(tq, lk, width):
    return [pltpu.VMEM((lk, 2 * width), BF16),
            pltpu.VMEM((tq, lk), F32), pltpu.VMEM((tq, lk), F32),
            pltpu.VMEM((tq, lk), BF16), pltpu.VMEM((tq, lk), BF16)]


def _gqa_kernel(q_ref, k_ref, v_ref, prev_ref, o_ref, vext_ref, s0_ref, s1_ref, p0_ref, p1_ref, *, group):
    del prev_ref

    @pl.when(pl.program_id(2) == 0)
    def _():
        _fill_v_ext(v_ref, vext_ref)

    qs = [q_ref[:, g * A_HEAD_DIM:(g + 1) * A_HEAD_DIM] for g in range(group)]
    for g, oe in enumerate(_attend(qs, k_ref, vext_ref, (s0_ref, s1_ref), (p0_ref, p1_ref))):
        o_ref[:, g * A_HEAD_DIM:(g + 1) * A_HEAD_DIM] = (oe[:, :A_HEAD_DIM] / oe[:, A_HEAD_DIM:]).astype(o_ref.dtype)


def _gqa(q, k, v, out, *, rpb, n_lat, batch, ctx_queries):
    t = q.shape[0]
    kvh = k.shape[1] // A_HEAD_DIM
    gw = A_GROUP * A_HEAD_DIM
    tq, nq, lk, q_blk, k_blk = _attn_geometry(rpb, n_lat, ctx_queries)
    per_sample = lambda a: a.reshape(batch, rpb, a.shape[1])
    kernel = functools.partial(_gqa_kernel, group=A_GROUP)
    return pl.pallas_call(
        kernel,
        grid=(batch, kvh, nq),
        in_specs=[
            pl.BlockSpec((None, tq, gw), lambda b, h, i: (b, q_blk(i), h)),
            pl.BlockSpec((None, lk, A_HEAD_DIM), lambda b, h, i: (b, k_blk, h)),
            pl.BlockSpec((None, lk, A_HEAD_DIM), lambda b, h, i: (b, k_blk, h)),
            pl.BlockSpec(memory_space=pl.ANY),
        ],
        out_specs=pl.BlockSpec((None, tq, gw), lambda b, h, i: (b, q_blk(i), h)),
        out_shape=jax.ShapeDtypeStruct((batch, rpb, out.shape[1]), BF16),
        scratch_shapes=_attn_scratch(tq, lk, A_HEAD_DIM),
        input_output_aliases={3: 0},
        compiler_params=_params(),
        name="gqa_ctx" if ctx_queries else "gqa_lat",
    )(per_sample(q), per_sample(k), per_sample(v), per_sample(out)).reshape(t, out.shape[1])


def _diff_kernel(lam_ref, g_ref, q_ref, k_ref, v_ref, prev_ref, o_ref, vext_ref, s0_ref, s1_ref, p0_ref, p1_ref,
                 *, lam_init):
    del prev_ref

    @pl.when(pl.program_id(2) == 0)
    def _():
        _fill_v_ext(v_ref, vext_ref)

    lv = lam_ref[...]
    lam = (jnp.exp(jnp.sum(lv[0:1] * lv[1:2], axis=-1, keepdims=True))
           - jnp.exp(jnp.sum(lv[2:3] * lv[3:4], axis=-1, keepdims=True)) + lam_init)
    q = q_ref[...]
    hw = q.shape[1]
    lane = lax.broadcasted_iota(jnp.int32, q.shape, 1)
    zero = jnp.zeros_like(q)
    qs = [jnp.where(lane < C_HEAD_DIM, q, zero), jnp.where(lane >= C_HEAD_DIM, q, zero)]
    oe1, oe2 = _attend(qs, k_ref, vext_ref, (s0_ref, s1_ref), (p0_ref, p1_ref))
    o = oe1[:, :hw] / oe1[:, hw:] - lam * (oe2[:, :hw] / oe2[:, hw:])
    o_ref[...] = (_rms(o, g_ref[...]) * (1.0 - lam_init)).astype(o_ref.dtype)


def _diff_attn(qk, v, lamvec, subln_g, out, *, lam_init, rpb, n_lat, batch, ctx_queries):
    t = qk.shape[0]
    heads = v.shape[1] // (2 * C_HEAD_DIM)
    hw = 2 * C_HEAD_DIM
    tq, nq, lk, q_blk, k_blk = _attn_geometry(rpb, n_lat, ctx_queries)
    per_sample = lambda a: a.reshape(batch, rpb, a.shape[1])
    kernel = functools.partial(_diff_kernel, lam_init=lam_init)
    return pl.pallas_call(
        kernel,
        grid=(batch, heads, nq),
        in_specs=[
            pl.BlockSpec(lamvec.shape, lambda b, h, i: (0, 0)),
            pl.BlockSpec((1, hw), lambda b, h, i: (0, 0)),
            pl.BlockSpec((None, tq, hw), lambda b, h, i: (b, q_blk(i), h)),
            pl.BlockSpec((None, lk, hw), lambda b, h, i: (b, k_blk, heads + h)),
            pl.BlockSpec((None, lk, hw), lambda b, h, i: (b, k_blk, h)),
            pl.BlockSpec(memory_space=pl.ANY),
        ],
        out_specs=pl.BlockSpec((None, tq, hw), lambda b, h, i: (b, q_blk(i), h)),
        out_shape=jax.ShapeDtypeStruct((batch, rpb, v.shape[1]), BF16),
        scratch_shapes=_attn_scratch(tq, lk, hw),
        input_output_aliases={5: 0},
        compiler_params=_params(),
        name="diff_ctx" if ctx_queries else "diff_lat",
    )(lamvec, subln_g.reshape(1, hw).astype(F32), per_sample(qk), per_sample(qk), per_sample(v),
      per_sample(out)).reshape(t, v.shape[1])


CONV_PAD = SUBLANE
SCAN_UNROLL = 8


def _sigmoid(x):
    return 0.5 * jnp.tanh(0.5 * x) + 0.5


def _block_scan(a, u, carry, reverse):
    row = lax.broadcasted_iota(jnp.int32, a.shape, 0)
    for s in (1, 2, 4):
        shift = SUBLANE - s if reverse else s
        valid = (row < SUBLANE - s) if reverse else (row >= s)
        a_prev = pltpu.roll(a, shift, 0)
        u_prev = pltpu.roll(u, shift, 0)
        u = jnp.where(valid, a * u_prev + u, u)
        a = jnp.where(valid, a * a_prev, a)
    h = u + a * carry
    last = h[0:1] if reverse else h[SUBLANE - 1:SUBLANE]
    return h, jnp.broadcast_to(last, h.shape)


def _rglru_kernel(xb_ref, yb_ref, cw_ref, cb_ref, gw_ref, gb_ref, lam_ref, buf_ref, o_ref,
                  xpad_ref, xc_ref, a_ref, u_ref, hf_ref, hb_ref, *, n_lat, n_ctx):
    del buf_ref
    cw = cw_ref[...]
    taps = cw.shape[0]
    zpad = jnp.zeros((CONV_PAD, B_BLOCK), F32)

    def conv_segment(r0, n):
        xpad_ref[0:CONV_PAD] = zpad
        xpad_ref[CONV_PAD:CONV_PAD + n] = xb_ref[r0:r0 + n]
        xpad_ref[CONV_PAD + n:2 * CONV_PAD + n] = zpad
        acc = jnp.broadcast_to(cb_ref[...], (n, B_BLOCK))
        for j in range(taps):
            off = CONV_PAD + j - taps // 2
            acc = acc + cw[j:j + 1] * xpad_ref[off:off + n]
        xc_ref[r0:r0 + n] = acc

    conv_segment(0, n_lat)
    conv_segment(n_lat, n_ctx)

    xc = xc_ref[...]
    gates = _dot(xc.astype(BF16), gw_ref[0]) + gb_ref[0]
    lam = lam_ref[...]
    softplus_neg = jnp.maximum(-lam, 0.0) + jnp.log(1.0 + jnp.exp(-jnp.abs(lam)))
    log2_base = (-LRU_C * LOG2E) * softplus_neg
    for d in range(2):
        r = _sigmoid(gates[:, (2 * d) * B_BLOCK:(2 * d + 1) * B_BLOCK])
        i = _sigmoid(gates[:, (2 * d + 1) * B_BLOCK:(2 * d + 2) * B_BLOCK])
        a = jnp.exp2(r * log2_base[d:d + 1])
        a_ref[d] = a
        u_ref[d] = jnp.sqrt(1.0 - a * a) * i * xc

    def scan_segment(r0, n, carry):
        nblk = n // SUBLANE

        def body(b, c):
            cf, cb = c
            rf = pl.multiple_of(r0 + b * SUBLANE, SUBLANE)
            rb = pl.multiple_of(r0 + (nblk - 1 - b) * SUBLANE, SUBLANE)
            hf, cf = _block_scan(a_ref[0, pl.ds(rf, SUBLANE), :], u_ref[0, pl.ds(rf, SUBLANE), :], cf, False)
            hb, cb = _block_scan(a_ref[1, pl.ds(rb, SUBLANE), :], u_ref[1, pl.ds(rb, SUBLANE), :], cb, True)
            hf_ref[pl.ds(rf, SUBLANE), :] = hf
            hb_ref[pl.ds(rb, SUBLANE), :] = hb
            return cf, cb

        return lax.fori_loop(0, nblk, body, carry, unroll=math.gcd(nblk, SCAN_UNROLL))

    zero = jnp.zeros((SUBLANE, B_BLOCK), F32)
    carry = scan_segment(n_lat, n_ctx, (zero, zero))
    scan_segment(0, n_lat, carry)
    o_ref[...] = ((hf_ref[...] + hb_ref[...]) * _gelu(yb_ref[...])).astype(o_ref.dtype)


def _rglru(xb, yb, conv_w, conv_b, gw, gb, lam, out, col0, *, rpb, n_lat, batch):
    t, width = xb.shape
    assert col0 % B_BLOCK == 0
    nh = width // B_BLOCK
    n_ctx = rpb - n_lat
    taps = conv_w.shape[0]
    return pl.pallas_call(
        functools.partial(_rglru_kernel, n_lat=n_lat, n_ctx=n_ctx),
        grid=(batch, nh),
        in_specs=[
            pl.BlockSpec((rpb, B_BLOCK), lambda b, h: (b, h)),
            pl.BlockSpec((rpb, B_BLOCK), lambda b, h: (b, h)),
            pl.BlockSpec((taps, B_BLOCK), lambda b, h: (0, h)),
            pl.BlockSpec((1, B_BLOCK), lambda b, h: (0, h)),
            pl.BlockSpec((1, B_BLOCK, 4 * B_BLOCK), lambda b, h: (h, 0, 0)),
            pl.BlockSpec((1, 1, 4 * B_BLOCK), lambda b, h: (h, 0, 0)),
            pl.BlockSpec((2, B_BLOCK), lambda b, h: (0, h)),
            pl.BlockSpec(memory_space=pl.ANY),
        ],
        out_specs=pl.BlockSpec((rpb, B_BLOCK), lambda b, h: (b, col0 // B_BLOCK + h)),
        out_shape=jax.ShapeDtypeStruct(out.shape, BF16),
        input_output_aliases={7: 0},
        scratch_shapes=[
            pltpu.VMEM((n_lat + 2 * CONV_PAD, B_BLOCK), F32),
            pltpu.VMEM((rpb, B_BLOCK), F32),
            pltpu.VMEM((2, rpb, B_BLOCK), F32),
            pltpu.VMEM((2, rpb, B_BLOCK), F32),
            pltpu.VMEM((rpb, B_BLOCK), F32),
            pltpu.VMEM((rpb, B_BLOCK), F32),
        ],
        compiler_params=_params(),
        name="rglru",
    )(xb, yb, conv_w, conv_b.reshape(1, width), gw, gb, lam, out)


CAST_ROWS = 256


def _expert_weight_copies(wgu_hbm, wd_hbm, wgu_f32, wd_f32, sem, layer, expert, slot):
    return (pltpu.make_async_copy(wgu_hbm.at[layer, expert], wgu_f32.at[slot], sem.at[slot, 0]),
            pltpu.make_async_copy(wd_hbm.at[layer, expert], wd_f32.at[slot], sem.at[slot, 1]))


def _expert_kernel(te_ref, nx_ref, sl_ref, nu_ref, x_ref, wgu_hbm, bgu_ref, wd_hbm, bd_ref, o_ref,
                   wgu_f32, wd_f32, wgu_bf, wd_bf, sem, *, f, layer):
    j = pl.program_id(0)
    expert = te_ref[j]
    slot = sl_ref[j]
    copies = functools.partial(_expert_weight_copies, wgu_hbm, wd_hbm, wgu_f32, wd_f32, sem, layer)

    @pl.when(j == 0)
    def _():
        for c in copies(expert, slot):
            c.start()

    @pl.when((j == 0) | (expert != te_ref[jnp.maximum(j - 1, 0)]))
    def _():
        for c in copies(expert, slot):
            c.wait()
        nxt = nx_ref[j]

        @pl.when(nxt >= 0)
        def _():
            for c in copies(nxt, 1 - slot):
                c.start()

        for r in range(0, wgu_bf.shape[0], CAST_ROWS):
            wgu_bf[r:r + CAST_ROWS] = wgu_f32[slot, r:r + CAST_ROWS].astype(BF16)
        for r in range(0, wd_bf.shape[0], CAST_ROWS):
            wd_bf[r:r + CAST_ROWS] = wd_f32[slot, r:r + CAST_ROWS].astype(BF16)

    @pl.when(j < nu_ref[0])
    def _():
        gu = _dot(x_ref[...], wgu_bf[...]) + bgu_ref[0, 0]
        gate = jnp.minimum(gu[:, :f], SWIGLU_LIMIT)
        up = jnp.clip(gu[:, f:], -SWIGLU_LIMIT, SWIGLU_LIMIT)
        act = (up + 1.0) * gate * jax.nn.sigmoid(gate * SWIGLU_ALPHA)
        o_ref[...] = (_dot(act.astype(BF16), wd_bf[...]) + bd_ref[0, 0]).astype(o_ref.dtype)

    @pl.when(j >= nu_ref[0])
    def _():
        o_ref[...] = jnp.zeros(o_ref.shape, o_ref.dtype)


def _experts(x_sorted, tile_expert, n_used, w_gu, b_gu, w_d, b_d, *, layer, tm):
    r, d = x_sorted.shape
    depth, e, _, f2 = w_gu.shape
    f = f2 // 2
    assert d % CAST_ROWS == 0 and f % CAST_ROWS == 0
    n_tiles = tile_expert.shape[0]
    first = jnp.concatenate([jnp.ones((1,), jnp.bool_), tile_expert[1:] != tile_expert[:-1]])
    slot = ((jnp.cumsum(first.astype(jnp.int32)) - 1) % 2).astype(jnp.int32)
    later = jnp.where(tile_expert[None, :] > tile_expert[:, None], tile_expert[None, :], e)
    next_expert = jnp.min(later, axis=1)
    next_expert = jnp.where(next_expert < e, next_expert, -1).astype(jnp.int32)
    grid_spec = pltpu.PrefetchScalarGridSpec(
        num_scalar_prefetch=4,
        grid=(n_tiles,),
        in_specs=[
            pl.BlockSpec((tm, d), lambda j, *_: (j, 0)),
            pl.BlockSpec(memory_space=pl.ANY),
            pl.BlockSpec((1, 1, 1, f2), lambda j, te, *_: (layer, te[j], 0, 0)),
            pl.BlockSpec(memory_space=pl.ANY),
            pl.BlockSpec((1, 1, 1, d), lambda j, te, *_: (layer, te[j], 0, 0)),
        ],
        out_specs=pl.BlockSpec((tm, d), lambda j, *_: (j, 0)),
        scratch_shapes=[pltpu.VMEM((2, d, f2), F32), pltpu.VMEM((2, f, d), F32),
                        pltpu.VMEM((d, f2), BF16), pltpu.VMEM((f, d), BF16),
                        pltpu.SemaphoreType.DMA((2, 2))],
    )
    assert r == n_tiles * tm
    return pl.pallas_call(
        functools.partial(_expert_kernel, f=f, layer=layer),
        grid_spec=grid_spec,
        out_shape=jax.ShapeDtypeStruct((r, d), BF16),
        compiler_params=_params(),
        name="experts",
    )(tile_expert, next_expert, slot, n_used, x_sorted, w_gu, b_gu.reshape(depth, e, 1, f2), w_d,
      b_d.reshape(depth, e, 1, d))


def _combine_kernel(y_ref, wt_ref, x_ref, modb_ref, modc_ref, *rest, k, bm, rpb, n_lat, tile0, with_next):
    is_ctx = _ctx_rows(pl.program_id(0) + tile0, bm, rpb, n_lat)
    wt = wt_ref[...]
    y = wt[:, 0:1] * y_ref[0].astype(F32)
    for j in range(1, TOP_K):
        y = y + wt[:, j:j + 1] * y_ref[j].astype(F32)
    x_new = x_ref[...] + _mod_row(modb_ref, modc_ref, k, is_ctx) * y
    if not with_next:
        (o_ref,) = rest
        o_ref[...] = x_new
        return
    g_ref, nmodb_ref, nmodc_ref, _, o_ref, h_ref = rest
    o_ref[...] = x_new
    h = _rms(x_new, g_ref[...]) * (1.0 + _mod_row(nmodb_ref, nmodc_ref, 1, is_ctx)) + _mod_row(nmodb_ref, nmodc_ref, 0, is_ctx)
    h_ref[...] = h.astype(BF16)


def _combine(y4, wt, xs, mod3, *, row0, k, rpb, n_lat, batch, nxt=None):
    t, d = xs.shape
    rows = y4.shape[1]
    bm = _tile(rpb, 256)
    assert row0 % bm == 0 and rows % bm == 0
    tile0 = row0 // bm
    row_spec = lambda width: pl.BlockSpec((bm, width), lambda i: (i + tile0, 0))
    mod_specs = [pl.BlockSpec((1, 6, d), lambda i: (((i + tile0) * bm) // rpb, 0, 0)),
                 pl.BlockSpec((1, 6, d), lambda i: (batch, 0, 0))]
    in_specs = [pl.BlockSpec((TOP_K, bm, d), lambda i: (0, i, 0)), row_spec(LANE), row_spec(d)] + mod_specs
    args = [y4, wt, xs, mod3, mod3]
    out_specs, out_shape, aliases = [row_spec(d)], [jax.ShapeDtypeStruct((t, d), F32)], {2: 0}
    if nxt is not None:
        g_next, mod3_next, h = nxt
        in_specs += [pl.BlockSpec((1, d), lambda i: (0, 0))] + mod_specs + [pl.BlockSpec(memory_space=pl.ANY)]
        args += [g_next.reshape(1, d), mod3_next, mod3_next, h]
        out_specs.append(row_spec(d))
        out_shape.append(jax.ShapeDtypeStruct((t, d), BF16))
        aliases[len(args) - 1] = 1
    out = pl.pallas_call(
        functools.partial(_combine_kernel, k=k, bm=bm, rpb=rpb, n_lat=n_lat, tile0=tile0, with_next=nxt is not None),
        grid=(rows // bm,),
        in_specs=in_specs,
        out_specs=out_specs,
        out_shape=out_shape,
        input_output_aliases=aliases,
        compiler_params=_params(),
        name="combine",
    )(*args)
    return (out[0], out[1]) if nxt is not None else (out[0], None)


def _dispatch_plan(top_idx, n_experts, tm):
    t = top_idx.shape[0]
    chosen = (top_idx[:, :, None] == jnp.arange(n_experts, dtype=jnp.int32)[None, None, :]).any(axis=1)
    chosen = chosen.astype(jnp.int32)
    before = jnp.cumsum(chosen, axis=0) - chosen
    counts = before[-1] + chosen[-1]
    padded = ((counts + tm - 1) // tm) * tm
    group_end = jnp.cumsum(padded)
    group_start = group_end - padded
    dest = group_start[top_idx] + jnp.take_along_axis(before, top_idx, axis=1)
    n_rows = TOP_K * t + n_experts * tm
    n_tiles = n_rows // tm
    tile_expert = jnp.minimum(
        jnp.sum(jnp.arange(n_tiles, dtype=jnp.int32)[:, None] >= (group_end // tm)[None, :], axis=1), n_experts - 1
    ).astype(jnp.int32)
    n_used = (group_end[-1] // tm).astype(jnp.int32).reshape(1)
    pad = jnp.arange(tm, dtype=jnp.int32)[None, :]
    pad_row = (group_start + counts)[:, None] + pad
    pad_key = jnp.where(pad_row < group_end[:, None], pad_row, n_rows + pad_row).reshape(-1)
    keys = jnp.concatenate([dest.reshape(-1), pad_key])
    vals = jnp.concatenate([jnp.repeat(jnp.arange(t, dtype=jnp.int32), TOP_K),
                            jnp.arange(n_experts * tm, dtype=jnp.int32) % t])
    _, row_token = lax.sort((keys, vals), num_keys=1)
    return dest, row_token, tile_expert, n_used


def _moe(xs, g, mod3, router_w, router_b, w_gu, b_gu, w_d, b_d, *, layer, rpb, n_lat, batch, nxt=None):
    t, d = xs.shape
    e = router_w.shape[1]
    splits = MOE_TOKEN_SPLITS if batch % MOE_TOKEN_SPLITS == 0 else 1
    ts = t // splits
    h2, idx, wt = _modulate_route(xs, g, mod3, router_w, router_b, k=3, rpb=rpb, n_lat=n_lat, batch=batch)
    h = h2
    plans = [_dispatch_plan(idx[s * ts:(s + 1) * ts, :TOP_K], e, MOE_ROW_TILE) for s in range(splits)]
    sorted_x = [jnp.take(h2, row_token + s * ts, axis=0, mode="clip") for s, (_, row_token, _, _) in enumerate(plans)]
    for s, (dest, _, tile_expert, n_used) in enumerate(plans):
        rows = _experts(sorted_x[s], tile_expert, n_used, w_gu, b_gu, w_d, b_d, layer=layer, tm=MOE_ROW_TILE)
        y4 = jnp.take(rows, dest.T.reshape(-1), axis=0, mode="clip").reshape(TOP_K, ts, d)
        xs, h = _combine(y4, wt, xs, mod3, row0=s * ts, k=5, rpb=rpb, n_lat=n_lat, batch=batch,
                         nxt=None if nxt is None else (*nxt, h))
    return xs, h


def _final_kernel(x_ref, g_ref, o_ref):
    o_ref[0] = _rms(x_ref[0], g_ref[...])


def _final_norm(xs3, g, n_lat):
    batch, rpb, d = xs3.shape
    bm = _tile(n_lat, 256)
    return pl.pallas_call(
        _final_kernel,
        grid=(batch, n_lat // bm),
        in_specs=[pl.BlockSpec((1, bm, d), lambda b, i: (b, i, 0)), pl.BlockSpec((1, d), lambda b, i: (0, 0))],
        out_specs=pl.BlockSpec((1, bm, d), lambda b, i: (b, i, 0)),
        out_shape=jax.ShapeDtypeStruct((batch, n_lat, d), F32),
        compiler_params=_params(),
        name="final_norm",
    )(xs3, g.reshape(1, d))


def _rope_tables(n_lat, n_ctx, head_dim):
    n_freq = head_dim // 4
    inv_freq = ROPE_THETA ** (-jnp.arange(n_freq, dtype=F32) / n_freq)
    tok = jnp.arange(n_lat, dtype=jnp.int32)
    ang_r = (tok // GRID_W).astype(F32)[:, None] * inv_freq
    ang_c = (tok % GRID_W).astype(F32)[:, None] * inv_freq
    cos = jnp.concatenate([jnp.cos(ang_r), jnp.cos(ang_r), jnp.cos(ang_c), jnp.cos(ang_c)], axis=-1)
    sin = jnp.concatenate([-jnp.sin(ang_r), jnp.sin(ang_r), -jnp.sin(ang_c), jnp.sin(ang_c)], axis=-1)
    reps = LANE // head_dim
    cos = jnp.concatenate([jnp.tile(cos, (1, reps)), jnp.ones((n_ctx, LANE), F32)], axis=0)
    sin = jnp.concatenate([jnp.tile(sin, (1, reps)), jnp.zeros((n_ctx, LANE), F32)], axis=0)
    return cos, sin, n_freq


def kernel(x, c, ctx, c_ctx, mod_w, mod_b, norm1_g, norm2_g, hyb_w_in, hyb_w_out, a_q_norm_g, a_k_norm_g, b_conv_w, b_conv_b, b_gate_a_w, b_gate_a_b, b_gate_x_w, b_gate_x_b, b_lru_lambda, c_w_in, c_w_out, c_lambda_q1, c_lambda_k1, c_lambda_q2, c_lambda_k2, c_subln_g, router_w, router_b, exp_w_gu, exp_b_gu, exp_w_d, exp_b_d, final_g):
    batch, n_lat, d = x.shape
    n_ctx = ctx.shape[1]
    rpb = n_lat + n_ctx
    t = batch * rpb
    depth = mod_w.shape[0]
    geo = dict(rpb=rpb, n_lat=n_lat, batch=batch)

    xs = jnp.concatenate([x, ctx], axis=1).reshape(t, d)
    mod_rows = -(-(batch + 1) // SUBLANE) * SUBLANE
    cvec = jnp.zeros((mod_rows, d), F32).at[:batch].set(c).at[batch].set(c_ctx)
    mod = _modmat(cvec, mod_w, mod_b)

    rope_a = _rope_tables(n_lat, n_ctx, A_HEAD_DIM)
    rope_c = _rope_tables(n_lat, n_ctx, C_HEAD_DIM)
    q_w = A_GROUP * A_HEAD_DIM * (hyb_w_in.shape[2] - 2 * (d // 2)) // ((A_GROUP + 2) * A_HEAD_DIM)
    kv_w = q_w // A_GROUP
    b_w = d // 2
    nh = b_w // B_BLOCK

    mod3s = [mod[l].reshape(mod_rows, 6, d) for l in range(depth)]
    h = _modulate(xs, norm1_g[0], mod3s[0], k=0, **geo)
    for l in range(depth):
        i = l // 2
        mod3 = mod3s[l]
        if l % 2 == 0:
            w_in = hyb_w_in[i].astype(BF16)
            o = 0
            q = _proj(h, w_in[:, o:o + q_w], rpb=rpb, out_dtype=BF16, norm_g=a_q_norm_g[i], rope=rope_a,
                      q_cols=q_w, q_scale=A_HEAD_DIM ** -0.5 * LOG2E)
            o += q_w
            kk = _proj(h, w_in[:, o:o + kv_w], rpb=rpb, out_dtype=BF16, norm_g=a_k_norm_g[i], rope=rope_a)
            o += kv_w
            v = _proj(h, w_in[:, o:o + kv_w], rpb=rpb, out_dtype=BF16)
            o += kv_w
            xb = _proj(h, w_in[:, o:o + b_w], rpb=rpb, out_dtype=F32)
            o += b_w
            yb = _proj(h, w_in[:, o:o + b_w], rpb=rpb, out_dtype=F32)
            mix = _gqa(q, kk, v, h, ctx_queries=False, **geo)
            mix = _gqa(q, kk, v, mix, ctx_queries=True, **geo)
            gw = jnp.concatenate([b_gate_a_w[i, 0], b_gate_x_w[i, 0], b_gate_a_w[i, 1], b_gate_x_w[i, 1]],
                                 axis=-1).astype(BF16)
            gb = jnp.concatenate([b_gate_a_b[i, 0].reshape(nh, 1, B_BLOCK), b_gate_x_b[i, 0].reshape(nh, 1, B_BLOCK),
                                  b_gate_a_b[i, 1].reshape(nh, 1, B_BLOCK), b_gate_x_b[i, 1].reshape(nh, 1, B_BLOCK)],
                                 axis=-1)
            mix = _rglru(xb, yb, b_conv_w[i], b_conv_b[i], gw, gb, b_lru_lambda[i], mix, q_w, **geo)
            pairs = [(mix, hyb_w_out[i].astype(BF16))]
        else:
            lam_init = 0.8 - 0.6 * math.exp(-0.3 * l)
            w_in = c_w_in[i].astype(BF16)
            qk_w = 2 * (w_in.shape[1] // 3)
            qk = _proj(h, w_in[:, :qk_w], rpb=rpb, out_dtype=BF16, rope=rope_c,
                       q_cols=qk_w // 2, q_scale=C_HEAD_DIM ** -0.5 * LOG2E)
            v = _proj(h, w_in[:, qk_w:], rpb=rpb, out_dtype=BF16)
            lamvec = jnp.stack([c_lambda_q1[i], c_lambda_k1[i], c_lambda_q2[i], c_lambda_k2[i]]).astype(F32)
            att = _diff_attn(qk, v, lamvec, c_subln_g[i], h, lam_init=lam_init, ctx_queries=False, **geo)
            att = _diff_attn(qk, v, lamvec, c_subln_g[i], att, lam_init=lam_init, ctx_queries=True, **geo)
            pairs = [(att, c_w_out[i].astype(BF16))]
        xs = _resid_proj(xs, mod3, pairs, k=2, **geo)
        xs, h = _moe(xs, norm2_g[l], mod3, router_w[l], router_b[l], exp_w_gu, exp_b_gu, exp_w_d, exp_b_d,
                     layer=l, nxt=(norm1_g[l + 1], mod3s[l + 1]) if l + 1 < depth else None, **geo)
    return _final_norm(xs.reshape(batch, rpb, d), final_g, n_lat)
```

```python
import functools
import math

import jax
import jax.numpy as jnp
from jax import lax
from jax.experimental import pallas as pl
from jax.experimental.pallas import tpu as pltpu

F32 = jnp.float32
BF16 = jnp.bfloat16

GRID_W = 64
ROPE_THETA = 10000.0
NORM_EPS = 1e-6
A_HEAD_DIM = 128
A_GROUP = 4
B_BLOCK = 128
LRU_C = 8.0
C_HEAD_DIM = 64
TOP_K = 4
SWIGLU_LIMIT = 7.0
SWIGLU_ALPHA = 1.702
LOG2E = math.log2(math.e)

LANE = 128
SUBLANE = 8
VMEM_LIMIT_BYTES = 56 * 1024 * 1024
ATTN_Q_ROWS = 1024
CTX_HEADS_PER_STEP = 8
ATTN_KEY_CHUNK = 256
PROJ_COLS = 2048
PROJ_TILE_ELEMS = 1152 * 1024
PROJ_EPILOGUE_COLS = 256
MOE_ROW_TILE = 512
MOE_TOKEN_SPLITS = 2


def _params():
    return pltpu.CompilerParams(vmem_limit_bytes=VMEM_LIMIT_BYTES)


def _tile(n, target, mult=SUBLANE):
    best = None
    for t in range(mult, min(n, target) + 1, mult):
        if n % t == 0:
            best = t
    assert best is not None, (n, target, mult)
    return best


def _ctx_rows(i, bm, rpb, n_lat):
    pos0 = lax.rem(i * bm, rpb)
    row = lax.broadcasted_iota(jnp.int32, (bm, 1), 0)
    return (pos0 + row) >= n_lat


def _mod_row(modb_ref, modc_ref, k, is_ctx):
    return jnp.where(is_ctx, modc_ref[0, k:k + 1, :], modb_ref[0, k:k + 1, :])


def _rms(x, g):
    return x * lax.rsqrt(jnp.mean(x * x, axis=-1, keepdims=True) + NORM_EPS) * g


def _dot(a, b):
    return jnp.dot(a, b, preferred_element_type=F32)


def _dot_nt(a, b):
    return lax.dot_general(a, b, (((1,), (1,)), ((), ())), preferred_element_type=F32)


def _modmat_kernel(c_ref, w_ref, b_ref, o_ref):
    c = c_ref[...]
    s = (c * jax.nn.sigmoid(c)).astype(BF16)
    o_ref[0] = _dot(s, w_ref[0].astype(BF16)) + b_ref[0]


def _modmat(cvec, mod_w, mod_b):
    depth, d, n = mod_w.shape
    r = cvec.shape[0]
    bn = _tile(n, 512, LANE)
    return pl.pallas_call(
        _modmat_kernel,
        grid=(depth, n // bn),
        in_specs=[
            pl.BlockSpec((r, d), lambda l, j: (0, 0)),
            pl.BlockSpec((1, d, bn), lambda l, j: (l, 0, j)),
            pl.BlockSpec((1, 1, bn), lambda l, j: (l, 0, j)),
        ],
        out_specs=pl.BlockSpec((1, r, bn), lambda l, j: (l, 0, j)),
        out_shape=jax.ShapeDtypeStruct((depth, r, n), F32),
        compiler_params=_params(),
        name="modmat",
    )(cvec, mod_w, mod_b.reshape(depth, 1, n))


def _modulate_kernel(x_ref, g_ref, modb_ref, modc_ref, h_ref, *, k, bm, rpb, n_lat):
    is_ctx = _ctx_rows(pl.program_id(0), bm, rpb, n_lat)
    y = _rms(x_ref[...], g_ref[...])
    h = y * (1.0 + _mod_row(modb_ref, modc_ref, k + 1, is_ctx)) + _mod_row(modb_ref, modc_ref, k, is_ctx)
    h_ref[...] = h.astype(BF16)


def _mod_specs(bm, rpb, batch, d):
    return [
        pl.BlockSpec((1, 6, d), lambda i: ((i * bm) // rpb, 0, 0)),
        pl.BlockSpec((1, 6, d), lambda i: (batch, 0, 0)),
    ]


def _modulate(xs, g, mod3, *, k, rpb, n_lat, batch):
    t, d = xs.shape
    bm = _tile(rpb, 256)
    return pl.pallas_call(
        functools.partial(_modulate_kernel, k=k, bm=bm, rpb=rpb, n_lat=n_lat),
        grid=(t // bm,),
        in_specs=[pl.BlockSpec((bm, d), lambda i: (i, 0)), pl.BlockSpec((1, d), lambda i: (0, 0))]
        + _mod_specs(bm, rpb, batch, d),
        out_specs=pl.BlockSpec((bm, d), lambda i: (i, 0)),
        out_shape=jax.ShapeDtypeStruct((t, d), BF16),
        compiler_params=_params(),
        name="modulate",
    )(xs, g.reshape(1, d), mod3, mod3)


def _router_kernel(x_ref, g_ref, modb_ref, modc_ref, rwh_ref, rwl_ref, rb_ref, h_ref, idx_ref, wt_ref,
                   *, k, bm, rpb, n_lat):
    is_ctx = _ctx_rows(pl.program_id(0), bm, rpb, n_lat)
    y = _rms(x_ref[...], g_ref[...])
    h = y * (1.0 + _mod_row(modb_ref, modc_ref, k + 1, is_ctx)) + _mod_row(modb_ref, modc_ref, k, is_ctx)
    h_hi = h.astype(BF16)
    h_lo = (h - h_hi.astype(F32)).astype(BF16)
    h_ref[...] = h_hi
    logits = _dot(h_hi, rwh_ref[...]) + (_dot(h_lo, rwh_ref[...]) + _dot(h_hi, rwl_ref[...])) + rb_ref[...]
    lane = lax.broadcasted_iota(jnp.int32, logits.shape, 1)
    vals, idxs = [], []
    for _ in range(TOP_K):
        m = jnp.max(logits, axis=-1, keepdims=True)
        ik = jnp.min(jnp.where(logits == m, lane, LANE), axis=-1, keepdims=True)
        vals.append(m)
        idxs.append(ik)
        logits = jnp.where(lane == ik, -jnp.inf, logits)
    es = [jnp.exp(v - vals[0]) for v in vals]
    den = es[0] + es[1] + es[2] + es[3]
    idx_out = jnp.zeros(lane.shape, jnp.int32)
    wt_out = jnp.zeros(lane.shape, F32)
    for j in range(TOP_K):
        idx_out = jnp.where(lane == j, idxs[j], idx_out)
        wt_out = jnp.where(lane == j, es[j] / den, wt_out)
    idx_ref[...] = idx_out
    wt_ref[...] = wt_out


def _modulate_route(xs, g, mod3, router_w, router_b, *, k, rpb, n_lat, batch):
    t, d = xs.shape
    e = router_w.shape[1]
    bm = _tile(rpb, 256)
    rw = jnp.zeros((d, LANE), F32).at[:, :e].set(router_w)
    rw_hi = rw.astype(BF16)
    rw_lo = (rw - rw_hi.astype(F32)).astype(BF16)
    rb = jnp.full((1, LANE), -1e30, F32).at[0, :e].set(router_b)
    return pl.pallas_call(
        functools.partial(_router_kernel, k=k, bm=bm, rpb=rpb, n_lat=n_lat),
        grid=(t // bm,),
        in_specs=[pl.BlockSpec((bm, d), lambda i: (i, 0)), pl.BlockSpec((1, d), lambda i: (0, 0))]
        + _mod_specs(bm, rpb, batch, d)
        + [pl.BlockSpec((d, LANE), lambda i: (0, 0))] * 2 + [pl.BlockSpec((1, LANE), lambda i: (0, 0))],
        out_specs=[pl.BlockSpec((bm, d), lambda i: (i, 0)),
                   pl.BlockSpec((bm, LANE), lambda i: (i, 0)),
                   pl.BlockSpec((bm, LANE), lambda i: (i, 0))],
        out_shape=[jax.ShapeDtypeStruct((t, d), BF16),
                   jax.ShapeDtypeStruct((t, LANE), jnp.int32),
                   jax.ShapeDtypeStruct((t, LANE), F32)],
        compiler_params=_params(),
        name="modulate_route",
    )(xs, g.reshape(1, d), mod3, mod3, rw_hi, rw_lo, rb)


def _gelu(x):
    return 0.5 * x * (1.0 + jnp.tanh(math.sqrt(2.0 / math.pi) * (x + 0.044715 * (x * x * x))))


def _proj_kernel(*refs, norm, rope_shift, q_cols, q_scale, bn):
    a_ref, w_ref = refs[0], refs[1]
    o_ref = refs[-1]
    pos = 2
    g_ref = cos_ref = sin_ref = None
    if norm:
        g_ref = refs[pos]
        pos += 1
    if rope_shift:
        cos_ref, sin_ref = refs[pos], refs[pos + 1]
    if not norm and not rope_shift:
        o_ref[...] = _dot(a_ref[...], w_ref[...]).astype(o_ref.dtype)
        return
    col_scale = jnp.where(pl.program_id(0) * bn < q_cols, q_scale, 1.0)
    lane = lax.broadcasted_iota(jnp.int32, (a_ref.shape[0], LANE), 1)
    first_half = jnp.bitwise_and(lane, 2 * rope_shift - 1) < rope_shift
    step = math.gcd(bn, PROJ_EPILOGUE_COLS)
    for c0 in range(0, bn, step):
        acc = _dot(a_ref[...], w_ref[:, c0:c0 + step])
        for c in range(0, step, LANE):
            y = acc[:, c:c + LANE]
            if norm:
                y = _rms(y, g_ref[...])
            partner = jnp.where(first_half, pltpu.roll(y, LANE - rope_shift, 1), pltpu.roll(y, rope_shift, 1))
            y = (y * cos_ref[...] + partner * sin_ref[...]) * col_scale
            o_ref[:, c0 + c:c0 + c + LANE] = y.astype(o_ref.dtype)


def _proj(a, w, *, rpb, out_dtype, norm_g=None, rope=None, q_cols=0, q_scale=1.0):
    t, kdim = a.shape
    n = w.shape[1]
    bn = _tile(math.gcd(n, q_cols) if q_cols else n, PROJ_COLS, LANE)
    bm = _tile(rpb, PROJ_TILE_ELEMS // bn, 16)
    tiles_per_sample = rpb // bm
    in_specs = [pl.BlockSpec((bm, kdim), lambda j, i: (i, 0)), pl.BlockSpec((kdim, bn), lambda j, i: (0, j))]
    args = [a, w]
    if norm_g is not None:
        in_specs.append(pl.BlockSpec((1, LANE), lambda j, i: (0, 0)))
        args.append(norm_g.reshape(1, LANE).astype(F32))
    rope_shift = 0
    if rope is not None:
        cos, sin, rope_shift = rope
        in_specs += [pl.BlockSpec((bm, LANE), lambda j, i: (i % tiles_per_sample, 0))] * 2
        args += [cos, sin]
    return pl.pallas_call(
        functools.partial(_proj_kernel, norm=norm_g is not None, rope_shift=rope_shift, q_cols=q_cols,
                          q_scale=q_scale, bn=bn),
        grid=(n // bn, t // bm),
        in_specs=in_specs,
        out_specs=pl.BlockSpec((bm, bn), lambda j, i: (i, j)),
        out_shape=jax.ShapeDtypeStruct((t, n), out_dtype),
        compiler_params=_params(),
        name="proj",
    )(*args)


def _resid_kernel(*refs, n_pairs, k, bm, rpb, n_lat):
    x_ref, modb_ref, modc_ref = refs[0], refs[1], refs[2]
    o_ref = refs[-1]
    acc = None
    for p in range(n_pairs):
        part = _dot(refs[3 + 2 * p][...], refs[4 + 2 * p][...])
        acc = part if acc is None else acc + part
    is_ctx = _ctx_rows(pl.program_id(1), bm, rpb, n_lat)
    o_ref[...] = x_ref[...] + _mod_row(modb_ref, modc_ref, k, is_ctx) * acc


def _resid_proj(xs, mod3, pairs, *, k, rpb, n_lat, batch):
    t, d = xs.shape
    bm = _tile(rpb, 768, 16)
    bn = _tile(d, 1024, LANE)
    in_specs = [
        pl.BlockSpec((bm, bn), lambda j, i: (i, j)),
        pl.BlockSpec((1, 6, bn), lambda j, i: ((i * bm) // rpb, 0, j)),
        pl.BlockSpec((1, 6, bn), lambda j, i: (batch, 0, j)),
    ]
    args = [xs, mod3, mod3]
    for a, w in pairs:
        kdim = a.shape[1]
        in_specs += [pl.BlockSpec((bm, kdim), lambda j, i: (i, 0)), pl.BlockSpec((kdim, bn), lambda j, i: (0, j))]
        args += [a, w]
    return pl.pallas_call(
        functools.partial(_resid_kernel, n_pairs=len(pairs), k=k, bm=bm, rpb=rpb, n_lat=n_lat),
        grid=(d // bn, t // bm),
        in_specs=in_specs,
        out_specs=pl.BlockSpec((bm, bn), lambda j, i: (i, j)),
        out_shape=jax.ShapeDtypeStruct((t, d), F32),
        compiler_params=_params(),
        name="resid_proj",
    )(*args)


def _fill_v_ext(v_ref, vext_ref):
    width = v_ref.shape[1]
    vext_ref[:, :width] = v_ref[...]
    vext_ref[:, width:] = jnp.ones((v_ref.shape[0], vext_ref.shape[1] - width), vext_ref.dtype)


def _attend(qs, k_ref, vext_ref, s_refs, p_refs):
    tq = qs[0].shape[0]
    lk = k_ref.shape[0]
    kc = _tile(lk, ATTN_KEY_CHUNK, LANE)

    def scores(q, c, s_ref, lane_max):
        s = _dot_nt(q, k_ref[c * kc:(c + 1) * kc])
        s_ref[:, c * kc:(c + 1) * kc] = s
        for j in range(kc // LANE):
            lane_max = jnp.maximum(lane_max, s[:, j * LANE:(j + 1) * LANE])
        return lane_max

    def exponentials(s_ref, p_ref, c, row_max):
        p_ref[:, c * kc:(c + 1) * kc] = jnp.exp2(s_ref[:, c * kc:(c + 1) * kc] - row_max).astype(BF16)

    lowest = jnp.full((tq, LANE), -jnp.inf, F32)
    lane_max = lowest
    for c in range(lk // kc):
        lane_max = scores(qs[0], c, s_refs[0], lane_max)
    row_max = jnp.max(lane_max, axis=-1, keepdims=True)
    for g in range(len(qs)):
        cur, nxt = g % 2, (g + 1) % 2
        lane_max = lowest
        for c in range(lk // kc):
            if g + 1 < len(qs):
                lane_max = scores(qs[g + 1], c, s_refs[nxt], lane_max)
            exponentials(s_refs[cur], p_refs[cur], c, row_max)
        if g + 1 < len(qs):
            row_max = jnp.max(lane_max, axis=-1, keepdims=True)
        yield _dot(p_refs[cur][...], vext_ref[...])


def _attn_geometry(rpb, n_lat, ctx_queries):
    n_ctx = rpb - n_lat
    if ctx_queries:
        assert n_lat % n_ctx == 0
        return n_ctx, 1, n_ctx, (lambda i: n_lat // n_ctx), n_lat // n_ctx
    tq = _tile(n_lat, ATTN_Q_ROWS)
    return tq, n_lat // tq, rpb, (lambda i: i), 0


def _attn_scratch(tq, lk, width):
    return [pltpu.VMEM((lk, 2 * width), BF16),
            pltpu.VMEM((tq, lk), F32), pltpu.VMEM((tq, lk), F32),
            pltpu.VMEM((tq, lk), BF16), pltpu.VMEM((tq, lk), BF16)]


def _gqa_kernel(q_ref, k_ref, v_ref, prev_ref, o_ref, vext_ref, s0_ref, s1_ref, p0_ref, p1_ref, *, group):
    del prev_ref

    @pl.when(pl.program_id(2) == 0)
    def _():
        _fill_v_ext(v_ref, vext_ref)

    qs = [q_ref[:, g * A_HEAD_DIM:(g + 1) * A_HEAD_DIM] for g in range(group)]
    for g, oe in enumerate(_attend(qs, k_ref, vext_ref, (s0_ref, s1_ref), (p0_ref, p1_ref))):
        o_ref[:, g * A_HEAD_DIM:(g + 1) * A_HEAD_DIM] = (oe[:, :A_HEAD_DIM] / oe[:, A_HEAD_DIM:]).astype(o_ref.dtype)


def _gqa(q, k, v, out, *, rpb, n_lat, batch, ctx_queries):
    t = q.shape[0]
    kvh = k.shape[1] // A_HEAD_DIM
    gw = A_GROUP * A_HEAD_DIM
    tq, nq, lk, q_blk, k_blk = _attn_geometry(rpb, n_lat, ctx_queries)
    per_sample = lambda a: a.reshape(batch, rpb, a.shape[1])
    kernel = functools.partial(_gqa_kernel, group=A_GROUP)
    return pl.pallas_call(
        kernel,
        grid=(batch, kvh, nq),
        in_specs=[
            pl.BlockSpec((None, tq, gw), lambda b, h, i: (b, q_blk(i), h)),
            pl.BlockSpec((None, lk, A_HEAD_DIM), lambda b, h, i: (b, k_blk, h)),
            pl.BlockSpec((None, lk, A_HEAD_DIM), lambda b, h, i: (b, k_blk, h)),
            pl.BlockSpec(memory_space=pl.ANY),
        ],
        out_specs=pl.BlockSpec((None, tq, gw), lambda b, h, i: (b, q_blk(i), h)),
        out_shape=jax.ShapeDtypeStruct((batch, rpb, out.shape[1]), BF16),
        scratch_shapes=_attn_scratch(tq, lk, A_HEAD_DIM),
        input_output_aliases={3: 0},
        compiler_params=_params(),
        name="gqa_ctx" if ctx_queries else "gqa_lat",
    )(per_sample(q), per_sample(k), per_sample(v), per_sample(out)).reshape(t, out.shape[1])


def _diff_kernel(lam_ref, g_ref, q_ref, k_ref, v_ref, prev_ref, o_ref, vext_ref, s0_ref, s1_ref, p0_ref, p1_ref,
                 *, lam_init, heads_per_step):
    del prev_ref
    hw = 2 * C_HEAD_DIM
    lv = lam_ref[...]
    lam = (jnp.exp(jnp.sum(lv[0:1] * lv[1:2], axis=-1, keepdims=True))
           - jnp.exp(jnp.sum(lv[2:3] * lv[3:4], axis=-1, keepdims=True)) + lam_init)
    for hh in range(heads_per_step):
        cols = slice(hh * hw, (hh + 1) * hw)

        @pl.when((pl.program_id(2) == 0) | (heads_per_step > 1))
        def _():
            _fill_v_ext(v_ref.at[:, cols], vext_ref)

        q = q_ref[:, cols]
        lane = lax.broadcasted_iota(jnp.int32, q.shape, 1)
        zero = jnp.zeros_like(q)
        qs = [jnp.where(lane < C_HEAD_DIM, q, zero), jnp.where(lane >= C_HEAD_DIM, q, zero)]
        oe1, oe2 = _attend(qs, k_ref.at[:, cols], vext_ref, (s0_ref, s1_ref), (p0_ref, p1_ref))
        o = oe1[:, :hw] / oe1[:, hw:] - lam * (oe2[:, :hw] / oe2[:, hw:])
        o_ref[:, cols] = (_rms(o, g_ref[...]) * (1.0 - lam_init)).astype(o_ref.dtype)


def _diff_attn(qk, v, lamvec, subln_g, out, *, lam_init, rpb, n_lat, batch, ctx_queries):
    t = qk.shape[0]
    heads = v.shape[1] // (2 * C_HEAD_DIM)
    hw = 2 * C_HEAD_DIM
    tq, nq, lk, q_blk, k_blk = _attn_geometry(rpb, n_lat, ctx_queries)
    hs = math.gcd(heads, CTX_HEADS_PER_STEP) if ctx_queries else 1
    assert nq == 1 or hs == 1
    gw = hs * hw
    per_sample = lambda a: a.reshape(batch, rpb, a.shape[1])
    kernel = functools.partial(_diff_kernel, lam_init=lam_init, heads_per_step=hs)
    return pl.pallas_call(
        kernel,
        grid=(batch, heads // hs, nq),
        in_specs=[
            pl.BlockSpec(lamvec.shape, lambda b, h, i: (0, 0)),
            pl.BlockSpec((1, hw), lambda b, h, i: (0, 0)),
            pl.BlockSpec((None, tq, gw), lambda b, h, i: (b, q_blk(i), h)),
            pl.BlockSpec((None, lk, gw), lambda b, h, i: (b, k_blk, heads // hs + h)),
            pl.BlockSpec((None, lk, gw), lambda b, h, i: (b, k_blk, h)),
            pl.BlockSpec(memory_space=pl.ANY),
        ],
        out_specs=pl.BlockSpec((None, tq, gw), lambda b, h, i: (b, q_blk(i), h)),
        out_shape=jax.ShapeDtypeStruct((batch, rpb, v.shape[1]), BF16),
        scratch_shapes=_attn_scratch(tq, lk, hw),
        input_output_aliases={5: 0},
        compiler_params=_params(),
        name="diff_ctx" if ctx_queries else "diff_lat",
    )(lamvec, subln_g.reshape(1, hw).astype(F32), per_sample(qk), per_sample(qk), per_sample(v),
      per_sample(out)).reshape(t, v.shape[1])


CONV_PAD = SUBLANE
SCAN_UNROLL = 8


def _sigmoid(x):
    return 0.5 * jnp.tanh(0.5 * x) + 0.5


def _block_scan(a, u, carry, reverse):
    row = lax.broadcasted_iota(jnp.int32, a.shape, 0)
    for s in (1, 2, 4):
        shift = SUBLANE - s if reverse else s
        valid = (row < SUBLANE - s) if reverse else (row >= s)
        a_prev = pltpu.roll(a, shift, 0)
        u_prev = pltpu.roll(u, shift, 0)
        u = jnp.where(valid, a * u_prev + u, u)
        a = jnp.where(valid, a * a_prev, a)
    h = u + a * carry
    last = h[0:1] if reverse else h[SUBLANE - 1:SUBLANE]
    return h, jnp.broadcast_to(last, h.shape)


def _rglru_kernel(xb_ref, yb_ref, cw_ref, cb_ref, gw_ref, gb_ref, lam_ref, buf_ref, o_ref,
                  xpad_ref, xc_ref, a_ref, u_ref, hf_ref, hb_ref, *, n_lat, n_ctx):
    del buf_ref
    cw = cw_ref[...]
    taps = cw.shape[0]
    zpad = jnp.zeros((CONV_PAD, B_BLOCK), F32)

    def conv_segment(r0, n):
        xpad_ref[0:CONV_PAD] = zpad
        xpad_ref[CONV_PAD:CONV_PAD + n] = xb_ref[r0:r0 + n]
        xpad_ref[CONV_PAD + n:2 * CONV_PAD + n] = zpad
        acc = jnp.broadcast_to(cb_ref[...], (n, B_BLOCK))
        for j in range(taps):
            off = CONV_PAD + j - taps // 2
            acc = acc + cw[j:j + 1] * xpad_ref[off:off + n]
        xc_ref[r0:r0 + n] = acc

    conv_segment(0, n_lat)
    conv_segment(n_lat, n_ctx)

    xc = xc_ref[...]
    gates = _dot(xc.astype(BF16), gw_ref[0]) + gb_ref[0]
    lam = lam_ref[...]
    softplus_neg = jnp.maximum(-lam, 0.0) + jnp.log(1.0 + jnp.exp(-jnp.abs(lam)))
    log2_base = (-LRU_C * LOG2E) * softplus_neg
    for d in range(2):
        r = _sigmoid(gates[:, (2 * d) * B_BLOCK:(2 * d + 1) * B_BLOCK])
        i = _sigmoid(gates[:, (2 * d + 1) * B_BLOCK:(2 * d + 2) * B_BLOCK])
        a = jnp.exp2(r * log2_base[d:d + 1])
        a_ref[d] = a
        u_ref[d] = jnp.sqrt(1.0 - a * a) * i * xc

    def scan_segment(r0, n, carry):
        nblk = n // SUBLANE

        def body(b, c):
            cf, cb = c
            rf = pl.multiple_of(r0 + b * SUBLANE, SUBLANE)
            rb = pl.multiple_of(r0 + (nblk - 1 - b) * SUBLANE, SUBLANE)
            hf, cf = _block_scan(a_ref[0, pl.ds(rf, SUBLANE), :], u_ref[0, pl.ds(rf, SUBLANE), :], cf, False)
            hb, cb = _block_scan(a_ref[1, pl.ds(rb, SUBLANE), :], u_ref[1, pl.ds(rb, SUBLANE), :], cb, True)
            hf_ref[pl.ds(rf, SUBLANE), :] = hf
            hb_ref[pl.ds(rb, SUBLANE), :] = hb
            return cf, cb

        return lax.fori_loop(0, nblk, body, carry, unroll=math.gcd(nblk, SCAN_UNROLL))

    zero = jnp.zeros((SUBLANE, B_BLOCK), F32)
    carry = scan_segment(n_lat, n_ctx, (zero, zero))
    scan_segment(0, n_lat, carry)
    o_ref[...] = ((hf_ref[...] + hb_ref[...]) * _gelu(yb_ref[...])).astype(o_ref.dtype)


def _rglru(xb, yb, conv_w, conv_b, gw, gb, lam, out, col0, *, rpb, n_lat, batch):
    t, width = xb.shape
    assert col0 % B_BLOCK == 0
    nh = width // B_BLOCK
    n_ctx = rpb - n_lat
    taps = conv_w.shape[0]
    return pl.pallas_call(
        functools.partial(_rglru_kernel, n_lat=n_lat, n_ctx=n_ctx),
        grid=(batch, nh),
        in_specs=[
            pl.BlockSpec((rpb, B_BLOCK), lambda b, h: (b, h)),
            pl.BlockSpec((rpb, B_BLOCK), lambda b, h: (b, h)),
            pl.BlockSpec((taps, B_BLOCK), lambda b, h: (0, h)),
            pl.BlockSpec((1, B_BLOCK), lambda b, h: (0, h)),
            pl.BlockSpec((1, B_BLOCK, 4 * B_BLOCK), lambda b, h: (h, 0, 0)),
            pl.BlockSpec((1, 1, 4 * B_BLOCK), lambda b, h: (h, 0, 0)),
            pl.BlockSpec((2, B_BLOCK), lambda b, h: (0, h)),
            pl.BlockSpec(memory_space=pl.ANY),
        ],
        out_specs=pl.BlockSpec((rpb, B_BLOCK), lambda b, h: (b, col0 // B_BLOCK + h)),
        out_shape=jax.ShapeDtypeStruct(out.shape, BF16),
        input_output_aliases={7: 0},
        scratch_shapes=[
            pltpu.VMEM((n_lat + 2 * CONV_PAD, B_BLOCK), F32),
            pltpu.VMEM((rpb, B_BLOCK), F32),
            pltpu.VMEM((2, rpb, B_BLOCK), F32),
            pltpu.VMEM((2, rpb, B_BLOCK), F32),
            pltpu.VMEM((rpb, B_BLOCK), F32),
            pltpu.VMEM((rpb, B_BLOCK), F32),
        ],
        compiler_params=_params(),
        name="rglru",
    )(xb, yb, conv_w, conv_b.reshape(1, width), gw, gb, lam, out)


CAST_ROWS = 256


def _expert_weight_copies(wgu_hbm, wd_hbm, wgu_f32, wd_f32, sem, layer, expert, slot):
    return (pltpu.make_async_copy(wgu_hbm.at[layer, expert], wgu_f32.at[slot], sem.at[slot, 0]),
            pltpu.make_async_copy(wd_hbm.at[layer, expert], wd_f32.at[slot], sem.at[slot, 1]))


def _expert_kernel(te_ref, nx_ref, sl_ref, nu_ref, x_ref, wgu_hbm, bgu_ref, wd_hbm, bd_ref, o_ref,
                   wgu_f32, wd_f32, wgu_bf, wd_bf, sem, *, f, layer):
    j = pl.program_id(0)
    expert = te_ref[j]
    slot = sl_ref[j]
    copies = functools.partial(_expert_weight_copies, wgu_hbm, wd_hbm, wgu_f32, wd_f32, sem, layer)

    @pl.when(j == 0)
    def _():
        for c in copies(expert, slot):
            c.start()

    @pl.when((j == 0) | (expert != te_ref[jnp.maximum(j - 1, 0)]))
    def _():
        for c in copies(expert, slot):
            c.wait()
        nxt = nx_ref[j]

        @pl.when(nxt >= 0)
        def _():
            for c in copies(nxt, 1 - slot):
                c.start()

        for r in range(0, wgu_bf.shape[0], CAST_ROWS):
            wgu_bf[r:r + CAST_ROWS] = wgu_f32[slot, r:r + CAST_ROWS].astype(BF16)
        for r in range(0, wd_bf.shape[0], CAST_ROWS):
            wd_bf[r:r + CAST_ROWS] = wd_f32[slot, r:r + CAST_ROWS].astype(BF16)

    @pl.when(j < nu_ref[0])
    def _():
        gu = _dot(x_ref[...], wgu_bf[...]) + bgu_ref[0, 0]
        gate = jnp.minimum(gu[:, :f], SWIGLU_LIMIT)
        up = jnp.clip(gu[:, f:], -SWIGLU_LIMIT, SWIGLU_LIMIT)
        act = (up + 1.0) * gate * jax.nn.sigmoid(gate * SWIGLU_ALPHA)
        o_ref[...] = (_dot(act.astype(BF16), wd_bf[...]) + bd_ref[0, 0]).astype(o_ref.dtype)

    @pl.when(j >= nu_ref[0])
    def _():
        o_ref[...] = jnp.zeros(o_ref.shape, o_ref.dtype)


def _experts(x_sorted, tile_expert, n_used, w_gu, b_gu, w_d, b_d, *, layer, tm):
    r, d = x_sorted.shape
    depth, e, _, f2 = w_gu.shape
    f = f2 // 2
    assert d % CAST_ROWS == 0 and f % CAST_ROWS == 0
    n_tiles = tile_expert.shape[0]
    first = jnp.concatenate([jnp.ones((1,), jnp.bool_), tile_expert[1:] != tile_expert[:-1]])
    slot = ((jnp.cumsum(first.astype(jnp.int32)) - 1) % 2).astype(jnp.int32)
    later = jnp.where(tile_expert[None, :] > tile_expert[:, None], tile_expert[None, :], e)
    next_expert = jnp.min(later, axis=1)
    next_expert = jnp.where(next_expert < e, next_expert, -1).astype(jnp.int32)
    grid_spec = pltpu.PrefetchScalarGridSpec(
        num_scalar_prefetch=4,
        grid=(n_tiles,),
        in_specs=[
            pl.BlockSpec((tm, d), lambda j, *_: (j, 0)),
            pl.BlockSpec(memory_space=pl.ANY),
            pl.BlockSpec((1, 1, 1, f2), lambda j, te, *_: (layer, te[j], 0, 0)),
            pl.BlockSpec(memory_space=pl.ANY),
            pl.BlockSpec((1, 1, 1, d), lambda j, te, *_: (layer, te[j], 0, 0)),
        ],
        out_specs=pl.BlockSpec((tm, d), lambda j, *_: (j, 0)),
        scratch_shapes=[pltpu.VMEM((2, d, f2), F32), pltpu.VMEM((2, f, d), F32),
                        pltpu.VMEM((d, f2), BF16), pltpu.VMEM((f, d), BF16),
                        pltpu.SemaphoreType.DMA((2, 2))],
    )
    assert r == n_tiles * tm
    return pl.pallas_call(
        functools.partial(_expert_kernel, f=f, layer=layer),
        grid_spec=grid_spec,
        out_shape=jax.ShapeDtypeStruct((r, d), BF16),
        compiler_params=_params(),
        name="experts",
    )(tile_expert, next_expert, slot, n_used, x_sorted, w_gu, b_gu.reshape(depth, e, 1, f2), w_d,
      b_d.reshape(depth, e, 1, d))


def _combine_kernel(y_ref, wt_ref, x_ref, modb_ref, modc_ref, *rest, k, bm, rpb, n_lat, tile0, with_next):
    is_ctx = _ctx_rows(pl.program_id(0) + tile0, bm, rpb, n_lat)
    wt = wt_ref[...]
    y = wt[:, 0:1] * y_ref[0].astype(F32)
    for j in range(1, TOP_K):
        y = y + wt[:, j:j + 1] * y_ref[j].astype(F32)
    x_new = x_ref[...] + _mod_row(modb_ref, modc_ref, k, is_ctx) * y
    if not with_next:
        (o_ref,) = rest
        o_ref[...] = x_new
        return
    g_ref, nmodb_ref, nmodc_ref, _, o_ref, h_ref = rest
    o_ref[...] = x_new
    h = _rms(x_new, g_ref[...]) * (1.0 + _mod_row(nmodb_ref, nmodc_ref, 1, is_ctx)) + _mod_row(nmodb_ref, nmodc_ref, 0, is_ctx)
    h_ref[...] = h.astype(BF16)


def _combine(y4, wt, xs, mod3, *, row0, k, rpb, n_lat, batch, nxt=None):
    t, d = xs.shape
    rows = y4.shape[1]
    bm = _tile(rpb, 256)
    assert row0 % bm == 0 and rows % bm == 0
    tile0 = row0 // bm
    row_spec = lambda width: pl.BlockSpec((bm, width), lambda i: (i + tile0, 0))
    mod_specs = [pl.BlockSpec((1, 6, d), lambda i: (((i + tile0) * bm) // rpb, 0, 0)),
                 pl.BlockSpec((1, 6, d), lambda i: (batch, 0, 0))]
    in_specs = [pl.BlockSpec((TOP_K, bm, d), lambda i: (0, i, 0)), row_spec(LANE), row_spec(d)] + mod_specs
    args = [y4, wt, xs, mod3, mod3]
    out_specs, out_shape, aliases = [row_spec(d)], [jax.ShapeDtypeStruct((t, d), F32)], {2: 0}
    if nxt is not None:
        g_next, mod3_next, h = nxt
        in_specs += [pl.BlockSpec((1, d), lambda i: (0, 0))] + mod_specs + [pl.BlockSpec(memory_space=pl.ANY)]
        args += [g_next.reshape(1, d), mod3_next, mod3_next, h]
        out_specs.append(row_spec(d))
        out_shape.append(jax.ShapeDtypeStruct((t, d), BF16))
        aliases[len(args) - 1] = 1
    out = pl.pallas_call(
        functools.partial(_combine_kernel, k=k, bm=bm, rpb=rpb, n_lat=n_lat, tile0=tile0, with_next=nxt is not None),
        grid=(rows // bm,),
        in_specs=in_specs,
        out_specs=out_specs,
        out_shape=out_shape,
        input_output_aliases=aliases,
        compiler_params=_params(),
        name="combine",
    )(*args)
    return (out[0], out[1]) if nxt is not None else (out[0], None)


def _dispatch_plan(top_idx, n_experts, tm):
    t = top_idx.shape[0]
    chosen = (top_idx[:, :, None] == jnp.arange(n_experts, dtype=jnp.int32)[None, None, :]).any(axis=1)
    chosen = chosen.astype(jnp.int32)
    before = jnp.cumsum(chosen, axis=0) - chosen
    counts = before[-1] + chosen[-1]
    padded = ((counts + tm - 1) // tm) * tm
    group_end = jnp.cumsum(padded)
    group_start = group_end - padded
    dest = group_start[top_idx] + jnp.take_along_axis(before, top_idx, axis=1)
    n_rows = TOP_K * t + n_experts * tm
    n_tiles = n_rows // tm
    tile_expert = jnp.minimum(
        jnp.sum(jnp.arange(n_tiles, dtype=jnp.int32)[:, None] >= (group_end // tm)[None, :], axis=1), n_experts - 1
    ).astype(jnp.int32)
    n_used = (group_end[-1] // tm).astype(jnp.int32).reshape(1)
    pad = jnp.arange(tm, dtype=jnp.int32)[None, :]
    pad_row = (group_start + counts)[:, None] + pad
    pad_key = jnp.where(pad_row < group_end[:, None], pad_row, n_rows + pad_row).reshape(-1)
    keys = jnp.concatenate([dest.reshape(-1), pad_key])
    vals = jnp.concatenate([jnp.repeat(jnp.arange(t, dtype=jnp.int32), TOP_K),
                            jnp.arange(n_experts * tm, dtype=jnp.int32) % t])
    _, row_token = lax.sort((keys, vals), num_keys=1)
    return dest, row_token, tile_expert, n_used


def _moe(xs, g, mod3, router_w, router_b, w_gu, b_gu, w_d, b_d, *, layer, rpb, n_lat, batch, nxt=None):
    t, d = xs.shape
    e = router_w.shape[1]
    splits = MOE_TOKEN_SPLITS if batch % MOE_TOKEN_SPLITS == 0 else 1
    ts = t // splits
    h2, idx, wt = _modulate_route(xs, g, mod3, router_w, router_b, k=3, rpb=rpb, n_lat=n_lat, batch=batch)
    h = h2
    plans = [_dispatch_plan(idx[s * ts:(s + 1) * ts, :TOP_K], e, MOE_ROW_TILE) for s in range(splits)]
    sorted_x = [jnp.take(h2, row_token + s * ts, axis=0, mode="clip") for s, (_, row_token, _, _) in enumerate(plans)]
    for s, (dest, _, tile_expert, n_used) in enumerate(plans):
        rows = _experts(sorted_x[s], tile_expert, n_used, w_gu, b_gu, w_d, b_d, layer=layer, tm=MOE_ROW_TILE)
        y4 = jnp.take(rows, dest.T.reshape(-1), axis=0, mode="clip").reshape(TOP_K, ts, d)
        xs, h = _combine(y4, wt, xs, mod3, row0=s * ts, k=5, rpb=rpb, n_lat=n_lat, batch=batch,
                         nxt=None if nxt is None else (*nxt, h))
    return xs, h


def _final_kernel(x_ref, g_ref, o_ref):
    o_ref[0] = _rms(x_ref[0], g_ref[...])


def _final_norm(xs3, g, n_lat):
    batch, rpb, d = xs3.shape
    bm = _tile(n_lat, 256)
    return pl.pallas_call(
        _final_kernel,
        grid=(batch, n_lat // bm),
        in_specs=[pl.BlockSpec((1, bm, d), lambda b, i: (b, i, 0)), pl.BlockSpec((1, d), lambda b, i: (0, 0))],
        out_specs=pl.BlockSpec((1, bm, d), lambda b, i: (b, i, 0)),
        out_shape=jax.ShapeDtypeStruct((batch, n_lat, d), F32),
        compiler_params=_params(),
        name="final_norm",
    )(xs3, g.reshape(1, d))


def _rope_tables(n_lat, n_ctx, head_dim):
    n_freq = head_dim // 4
    inv_freq = ROPE_THETA ** (-jnp.arange(n_freq, dtype=F32) / n_freq)
    tok = jnp.arange(n_lat, dtype=jnp.int32)
    ang_r = (tok // GRID_W).astype(F32)[:, None] * inv_freq
    ang_c = (tok % GRID_W).astype(F32)[:, None] * inv_freq
    cos = jnp.concatenate([jnp.cos(ang_r), jnp.cos(ang_r), jnp.cos(ang_c), jnp.cos(ang_c)], axis=-1)
    sin = jnp.concatenate([-jnp.sin(ang_r), jnp.sin(ang_r), -jnp.sin(ang_c), jnp.sin(ang_c)], axis=-1)
    reps = LANE // head_dim
    cos = jnp.concatenate([jnp.tile(cos, (1, reps)), jnp.ones((n_ctx, LANE), F32)], axis=0)
    sin = jnp.concatenate([jnp.tile(sin, (1, reps)), jnp.zeros((n_ctx, LANE), F32)], axis=0)
    return cos, sin, n_freq


def kernel(x, c, ctx, c_ctx, mod_w, mod_b, norm1_g, norm2_g, hyb_w_in, hyb_w_out, a_q_norm_g, a_k_norm_g, b_conv_w, b_conv_b, b_gate_a_w, b_gate_a_b, b_gate_x_w, b_gate_x_b, b_lru_lambda, c_w_in, c_w_out, c_lambda_q1, c_lambda_k1, c_lambda_q2, c_lambda_k2, c_subln_g, router_w, router_b, exp_w_gu, exp_b_gu, exp_w_d, exp_b_d, final_g):
    batch, n_lat, d = x.shape
    n_ctx = ctx.shape[1]
    rpb = n_lat + n_ctx
    t = batch * rpb
    depth = mod_w.shape[0]
    geo = dict(rpb=rpb, n_lat=n_lat, batch=batch)

    xs = jnp.concatenate([x, ctx], axis=1).reshape(t, d)
    mod_rows = -(-(batch + 1) // SUBLANE) * SUBLANE
    cvec = jnp.zeros((mod_rows, d), F32).at[:batch].set(c).at[batch].set(c_ctx)
    mod = _modmat(cvec, mod_w, mod_b)

    rope_a = _rope_tables(n_lat, n_ctx, A_HEAD_DIM)
    rope_c = _rope_tables(n_lat, n_ctx, C_HEAD_DIM)
    q_w = A_GROUP * A_HEAD_DIM * (hyb_w_in.shape[2] - 2 * (d // 2)) // ((A_GROUP + 2) * A_HEAD_DIM)
    kv_w = q_w // A_GROUP
    b_w = d // 2
    nh = b_w // B_BLOCK

    mod3s = [mod[l].reshape(mod_rows, 6, d) for l in range(depth)]
    h = _modulate(xs, norm1_g[0], mod3s[0], k=0, **geo)
    for l in range(depth):
        i = l // 2
        mod3 = mod3s[l]
        if l % 2 == 0:
            w_in = hyb_w_in[i].astype(BF16)
            o = 0
            q = _proj(h, w_in[:, o:o + q_w], rpb=rpb, out_dtype=BF16, norm_g=a_q_norm_g[i], rope=rope_a,
                      q_cols=q_w, q_scale=A_HEAD_DIM ** -0.5 * LOG2E)
            o += q_w
            kk = _proj(h, w_in[:, o:o + kv_w], rpb=rpb, out_dtype=BF16, norm_g=a_k_norm_g[i], rope=rope_a)
            o += kv_w
            v = _proj(h, w_in[:, o:o + kv_w], rpb=rpb, out_dtype=BF16)
            o += kv_w
            xb = _proj(h, w_in[:, o:o + b_w], rpb=rpb, out_dtype=F32)
            o += b_w
            yb = _proj(h, w_in[:, o:o + b_w], rpb=rpb, out_dtype=F32)
            mix = _gqa(q, kk, v, h, ctx_queries=False, **geo)
            mix = _gqa(q, kk, v, mix, ctx_queries=True, **geo)
            gw = jnp.concatenate([b_gate_a_w[i, 0], b_gate_x_w[i, 0], b_gate_a_w[i, 1], b_gate_x_w[i, 1]],
                                 axis=-1).astype(BF16)
            gb = jnp.concatenate([b_gate_a_b[i, 0].reshape(nh, 1, B_BLOCK), b_gate_x_b[i, 0].reshape(nh, 1, B_BLOCK),
                                  b_gate_a_b[i, 1].reshape(nh, 1, B_BLOCK), b_gate_x_b[i, 1].reshape(nh, 1, B_BLOCK)],
                                 axis=-1)
            mix = _rglru(xb, yb, b_conv_w[i], b_conv_b[i], gw, gb, b_lru_lambda[i], mix, q_w, **geo)
            pairs = [(mix, hyb_w_out[i].astype(BF16))]
        else:
            lam_init = 0.8 - 0.6 * math.exp(-0.3 * l)
            w_in = c_w_in[i].astype(BF16)
            qk_w = 2 * (w_in.shape[1] // 3)
            qk = _proj(h, w_in[:, :qk_w], rpb=rpb, out_dtype=BF16, rope=rope_c,
                       q_cols=qk_w // 2, q_scale=C_HEAD_DIM ** -0.5 * LOG2E)
            v = _proj(h, w_in[:, qk_w:], rpb=rpb, out_dtype=BF16)
            lamvec = jnp.stack([c_lambda_q1[i], c_lambda_k1[i], c_lambda_q2[i], c_lambda_k2[i]]).astype(F32)
            att = _diff_attn(qk, v, lamvec, c_subln_g[i], h, lam_init=lam_init, ctx_queries=False, **geo)
            att = _diff_attn(qk, v, lamvec, c_subln_g[i], att, lam_init=lam_init, ctx_queries=True, **geo)
            pairs = [(att, c_w_out[i].astype(BF16))]
        xs = _resid_proj(xs, mod3, pairs, k=2, **geo)
        xs, h = _moe(xs, norm2_g[l], mod3, router_w[l], router_b[l], exp_w_gu, exp_b_gu, exp_w_d, exp_b_d,
                     layer=l, nxt=(norm1_g[l + 1], mod3s[l + 1]) if l + 1 < depth else None, **geo)
    return _final_norm(xs.reshape(batch, rpb, d), final_g, n_lat)
```

```python
import functools
import math

import jax
import jax.numpy as jnp
from jax import lax
from jax.experimental import pallas as pl
from jax.experimental.pallas import tpu as pltpu

F32 = jnp.float32
BF16 = jnp.bfloat16

GRID_W = 64
ROPE_THETA = 10000.0
NORM_EPS = 1e-6
A_HEAD_DIM = 128
A_GROUP = 4
B_BLOCK = 128
LRU_C = 8.0
C_HEAD_DIM = 64
TOP_K = 4
SWIGLU_LIMIT = 7.0
SWIGLU_ALPHA = 1.702
LOG2E = math.log2(math.e)

LANE = 128
SUBLANE = 8
VMEM_LIMIT_BYTES = 56 * 1024 * 1024
ATTN_Q_ROWS = 1024
CTX_HEADS_PER_STEP = 8
ATTN_KEY_CHUNK = 256
PROJ_COLS = 2048
PROJ_TILE_ELEMS = 1152 * 1024
PROJ_EPILOGUE_COLS = 256
MOE_ROW_TILE = 512
MOE_TOKEN_SPLITS = 2


def _params():
    return pltpu.CompilerParams(vmem_limit_bytes=VMEM_LIMIT_BYTES)


def _tile(n, target, mult=SUBLANE):
    best = None
    for t in range(mult, min(n, target) + 1, mult):
        if n % t == 0:
            best = t
    assert best is not None, (n, target, mult)
    return best


def _ctx_rows(i, bm, rpb, n_lat):
    pos0 = lax.rem(i * bm, rpb)
    row = lax.broadcasted_iota(jnp.int32, (bm, 1), 0)
    return (pos0 + row) >= n_lat


def _mod_row(modb_ref, modc_ref, k, is_ctx):
    return jnp.where(is_ctx, modc_ref[0, k:k + 1, :], modb_ref[0, k:k + 1, :])


def _rms(x, g):
    return x * lax.rsqrt(jnp.mean(x * x, axis=-1, keepdims=True) + NORM_EPS) * g


def _dot(a, b):
    return jnp.dot(a, b, preferred_element_type=F32)


def _dot_nt(a, b):
    return lax.dot_general(a, b, (((1,), (1,)), ((), ())), preferred_element_type=F32)


def _modmat_kernel(c_ref, w_ref, b_ref, o_ref):
    c = c_ref[...]
    s = (c * jax.nn.sigmoid(c)).astype(BF16)
    o_ref[0] = _dot(s, w_ref[0].astype(BF16)) + b_ref[0]


def _modmat(cvec, mod_w, mod_b):
    depth, d, n = mod_w.shape
    r = cvec.shape[0]
    bn = _tile(n, 512, LANE)
    return pl.pallas_call(
        _modmat_kernel,
        grid=(depth, n // bn),
        in_specs=[
            pl.BlockSpec((r, d), lambda l, j: (0, 0)),
            pl.BlockSpec((1, d, bn), lambda l, j: (l, 0, j)),
            pl.BlockSpec((1, 1, bn), lambda l, j: (l, 0, j)),
        ],
        out_specs=pl.BlockSpec((1, r, bn), lambda l, j: (l, 0, j)),
        out_shape=jax.ShapeDtypeStruct((depth, r, n), F32),
        compiler_params=_params(),
        name="modmat",
    )(cvec, mod_w, mod_b.reshape(depth, 1, n))


def _modulate_kernel(x_ref, g_ref, modb_ref, modc_ref, h_ref, *, k, bm, rpb, n_lat):
    is_ctx = _ctx_rows(pl.program_id(0), bm, rpb, n_lat)
    y = _rms(x_ref[...], g_ref[...])
    h = y * (1.0 + _mod_row(modb_ref, modc_ref, k + 1, is_ctx)) + _mod_row(modb_ref, modc_ref, k, is_ctx)
    h_ref[...] = h.astype(BF16)


def _mod_specs(bm, rpb, batch, d):
    return [
        pl.BlockSpec((1, 6, d), lambda i: ((i * bm) // rpb, 0, 0)),
        pl.BlockSpec((1, 6, d), lambda i: (batch, 0, 0)),
    ]


def _modulate(xs, g, mod3, *, k, rpb, n_lat, batch):
    t, d = xs.shape
    bm = _tile(rpb, 256)
    return pl.pallas_call(
        functools.partial(_modulate_kernel, k=k, bm=bm, rpb=rpb, n_lat=n_lat),
        grid=(t // bm,),
        in_specs=[pl.BlockSpec((bm, d), lambda i: (i, 0)), pl.BlockSpec((1, d), lambda i: (0, 0))]
        + _mod_specs(bm, rpb, batch, d),
        out_specs=pl.BlockSpec((bm, d), lambda i: (i, 0)),
        out_shape=jax.ShapeDtypeStruct((t, d), BF16),
        compiler_params=_params(),
        name="modulate",
    )(xs, g.reshape(1, d), mod3, mod3)


def _router_kernel(x_ref, g_ref, modb_ref, modc_ref, rwh_ref, rwl_ref, rb_ref, h_ref, idx_ref, wt_ref,
                   *, k, bm, rpb, n_lat):
    is_ctx = _ctx_rows(pl.program_id(0), bm, rpb, n_lat)
    y = _rms(x_ref[...], g_ref[...])
    h = y * (1.0 + _mod_row(modb_ref, modc_ref, k + 1, is_ctx)) + _mod_row(modb_ref, modc_ref, k, is_ctx)
    h_hi = h.astype(BF16)
    h_lo = (h - h_hi.astype(F32)).astype(BF16)
    h_ref[...] = h_hi
    logits = _dot(h_hi, rwh_ref[...]) + (_dot(h_lo, rwh_ref[...]) + _dot(h_hi, rwl_ref[...])) + rb_ref[...]
    lane = lax.broadcasted_iota(jnp.int32, logits.shape, 1)
    vals, idxs = [], []
    for _ in range(TOP_K):
        m = jnp.max(logits, axis=-1, keepdims=True)
        ik = jnp.min(jnp.where(logits == m, lane, LANE), axis=-1, keepdims=True)
        vals.append(m)
        idxs.append(ik)
        logits = jnp.where(lane == ik, -jnp.inf, logits)
    es = [jnp.exp(v - vals[0]) for v in vals]
    den = es[0] + es[1] + es[2] + es[3]
    idx_out = jnp.zeros(lane.shape, jnp.int32)
    wt_out = jnp.zeros(lane.shape, F32)
    for j in range(TOP_K):
        idx_out = jnp.where(lane == j, idxs[j], idx_out)
        wt_out = jnp.where(lane == j, es[j] / den, wt_out)
    idx_ref[...] = idx_out
    wt_ref[...] = wt_out


def _modulate_route(xs, g, mod3, router_w, router_b, *, k, rpb, n_lat, batch):
    t, d = xs.shape
    e = router_w.shape[1]
    bm = _tile(rpb, 256)
    rw = jnp.zeros((d, LANE), F32).at[:, :e].set(router_w)
    rw_hi = rw.astype(BF16)
    rw_lo = (rw - rw_hi.astype(F32)).astype(BF16)
    rb = jnp.full((1, LANE), -1e30, F32).at[0, :e].set(router_b)
    return pl.pallas_call(
        functools.partial(_router_kernel, k=k, bm=bm, rpb=rpb, n_lat=n_lat),
        grid=(t // bm,),
        in_specs=[pl.BlockSpec((bm, d), lambda i: (i, 0)), pl.BlockSpec((1, d), lambda i: (0, 0))]
        + _mod_specs(bm, rpb, batch, d)
        + [pl.BlockSpec((d, LANE), lambda i: (0, 0))] * 2 + [pl.BlockSpec((1, LANE), lambda i: (0, 0))],
        out_specs=[pl.BlockSpec((bm, d), lambda i: (i, 0)),
                   pl.BlockSpec((bm, LANE), lambda i: (i, 0)),
                   pl.BlockSpec((bm, LANE), lambda i: (i, 0))],
        out_shape=[jax.ShapeDtypeStruct((t, d), BF16),
                   jax.ShapeDtypeStruct((t, LANE), jnp.int32),
                   jax.ShapeDtypeStruct((t, LANE), F32)],
        compiler_params=_params(),
        name="modulate_route",
    )(xs, g.reshape(1, d), mod3, mod3, rw_hi, rw_lo, rb)


def _gelu(x):
    return 0.5 * x * (1.0 + jnp.tanh(math.sqrt(2.0 / math.pi) * (x + 0.044715 * (x * x * x))))


def _proj_kernel(*refs, norm, rope_shift, q_cols, q_scale, bn):
    a_ref, w_ref = refs[0], refs[1]
    o_ref = refs[-1]
    pos = 2
    g_ref = cos_ref = sin_ref = None
    if norm:
        g_ref = refs[pos]
        pos += 1
    if rope_shift:
        cos_ref, sin_ref = refs[pos], refs[pos + 1]
    if not norm and not rope_shift:
        o_ref[...] = _dot(a_ref[...], w_ref[...]).astype(o_ref.dtype)
        return
    col_scale = jnp.where(pl.program_id(0) * bn < q_cols, q_scale, 1.0)
    lane = lax.broadcasted_iota(jnp.int32, (a_ref.shape[0], LANE), 1)
    first_half = jnp.bitwise_and(lane, 2 * rope_shift - 1) < rope_shift
    step = math.gcd(bn, PROJ_EPILOGUE_COLS)
    for c0 in range(0, bn, step):
        acc = _dot(a_ref[...], w_ref[:, c0:c0 + step])
        for c in range(0, step, LANE):
            y = acc[:, c:c + LANE]
            if norm:
                y = _rms(y, g_ref[...])
            partner = jnp.where(first_half, pltpu.roll(y, LANE - rope_shift, 1), pltpu.roll(y, rope_shift, 1))
            y = (y * cos_ref[...] + partner * sin_ref[...]) * col_scale
            o_ref[:, c0 + c:c0 + c + LANE] = y.astype(o_ref.dtype)


def _proj(a, w, *, rpb, out_dtype, norm_g=None, rope=None, q_cols=0, q_scale=1.0):
    t, kdim = a.shape
    n = w.shape[1]
    bn = _tile(math.gcd(n, q_cols) if q_cols else n, PROJ_COLS, LANE)
    bm = _tile(rpb, PROJ_TILE_ELEMS // bn, 16)
    tiles_per_sample = rpb // bm
    in_specs = [pl.BlockSpec((bm, kdim), lambda j, i: (i, 0)), pl.BlockSpec((kdim, bn), lambda j, i: (0, j))]
    args = [a, w]
    if norm_g is not None:
        in_specs.append(pl.BlockSpec((1, LANE), lambda j, i: (0, 0)))
        args.append(norm_g.reshape(1, LANE).astype(F32))
    rope_shift = 0
    if rope is not None:
        cos, sin, rope_shift = rope
        in_specs += [pl.BlockSpec((bm, LANE), lambda j, i: (i % tiles_per_sample, 0))] * 2
        args += [cos, sin]
    return pl.pallas_call(
        functools.partial(_proj_kernel, norm=norm_g is not None, rope_shift=rope_shift, q_cols=q_cols,
                          q_scale=q_scale, bn=bn),
        grid=(n // bn, t // bm),
        in_specs=in_specs,
        out_specs=pl.BlockSpec((bm, bn), lambda j, i: (i, j)),
        out_shape=jax.ShapeDtypeStruct((t, n), out_dtype),
        compiler_params=_params(),
        name="proj",
    )(*args)


def _resid_kernel(*refs, n_pairs, k, bm, rpb, n_lat):
    x_ref, modb_ref, modc_ref = refs[0], refs[1], refs[2]
    o_ref = refs[-1]
    acc = None
    for p in range(n_pairs):
        part = _dot(refs[3 + 2 * p][...], refs[4 + 2 * p][...])
        acc = part if acc is None else acc + part
    is_ctx = _ctx_rows(pl.program_id(1), bm, rpb, n_lat)
    o_ref[...] = x_ref[...] + _mod_row(modb_ref, modc_ref, k, is_ctx) * acc


def _resid_proj(xs, mod3, pairs, *, k, rpb, n_lat, batch):
    t, d = xs.shape
    bm = _tile(rpb, 768, 16)
    bn = _tile(d, 1024, LANE)
    in_specs = [
        pl.BlockSpec((bm, bn), lambda j, i: (i, j)),
        pl.BlockSpec((1, 6, bn), lambda j, i: ((i * bm) // rpb, 0, j)),
        pl.BlockSpec((1, 6, bn), lambda j, i: (batch, 0, j)),
    ]
    args = [xs, mod3, mod3]
    for a, w in pairs:
        kdim = a.shape[1]
        in_specs += [pl.BlockSpec((bm, kdim), lambda j, i: (i, 0)), pl.BlockSpec((kdim, bn), lambda j, i: (0, j))]
        args += [a, w]
    return pl.pallas_call(
        functools.partial(_resid_kernel, n_pairs=len(pairs), k=k, bm=bm, rpb=rpb, n_lat=n_lat),
        grid=(d // bn, t // bm),
        in_specs=in_specs,
        out_specs=pl.BlockSpec((bm, bn), lambda j, i: (i, j)),
        out_shape=jax.ShapeDtypeStruct((t, d), F32),
        compiler_params=_params(),
        name="resid_proj",
    )(*args)


def _fill_v_ext(v_ref, vext_ref):
    width = v_ref.shape[1]
    vext_ref[:, :width] = v_ref[...]
    vext_ref[:, width:] = jnp.ones((v_ref.shape[0], vext_ref.shape[1] - width), vext_ref.dtype)


def _attend(qs, k_ref, vext_ref, s_refs, p_refs):
    tq = qs[0].shape[0]
    lk = k_ref.shape[0]
    kc = _tile(lk, ATTN_KEY_CHUNK, LANE)

    def scores(q, c, s_ref, lane_max):
        s = _dot_nt(q, k_ref[c * kc:(c + 1) * kc])
        s_ref[:, c * kc:(c + 1) * kc] = s
        for j in range(kc // LANE):
            lane_max = jnp.maximum(lane_max, s[:, j * LANE:(j + 1) * LANE])
        return lane_max

    def exponentials(s_ref, p_ref, c, row_max):
        p_ref[:, c * kc:(c + 1) * kc] = jnp.exp2(s_ref[:, c * kc:(c + 1) * kc] - row_max).astype(BF16)

    lowest = jnp.full((tq, LANE), -jnp.inf, F32)
    lane_max = lowest
    for c in range(lk // kc):
        lane_max = scores(qs[0], c, s_refs[0], lane_max)
    row_max = jnp.max(lane_max, axis=-1, keepdims=True)
    for g in range(len(qs)):
        cur, nxt = g % 2, (g + 1) % 2
        lane_max = lowest
        for c in range(lk // kc):
            if g + 1 < len(qs):
                lane_max = scores(qs[g + 1], c, s_refs[nxt], lane_max)
            exponentials(s_refs[cur], p_refs[cur], c, row_max)
        if g + 1 < len(qs):
            row_max = jnp.max(lane_max, axis=-1, keepdims=True)
        yield _dot(p_refs[cur][...], vext_ref[...])


def _attn_geometry(rpb, n_lat, ctx_queries):
    n_ctx = rpb - n_lat
    if ctx_queries:
        assert n_lat % n_ctx == 0
        return n_ctx, 1, n_ctx, (lambda i: n_lat // n_ctx), n_lat // n_ctx
    tq = _tile(n_lat, ATTN_Q_ROWS)
    return tq, n_lat // tq, rpb, (lambda i: i), 0


def _attn_scratch(tq, lk, width):
    return [pltpu.VMEM((lk, 2 * width), BF16),
            pltpu.VMEM((tq, lk), F32), pltpu.VMEM((tq, lk), F32),
            pltpu.VMEM((tq, lk), BF16), pltpu.VMEM((tq, lk), BF16)]


def _gqa_kernel(q_ref, k_ref, v_ref, prev_ref, o_ref, vext_ref, s0_ref, s1_ref, p0_ref, p1_ref, *, group):
    del prev_ref

    @pl.when(pl.program_id(2) == 0)
    def _():
        _fill_v_ext(v_ref, vext_ref)

    qs = [q_ref[:, g * A_HEAD_DIM:(g + 1) * A_HEAD_DIM] for g in range(group)]
    for g, oe in enumerate(_attend(qs, k_ref, vext_ref, (s0_ref, s1_ref), (p0_ref, p1_ref))):
        o_ref[:, g * A_HEAD_DIM:(g + 1) * A_HEAD_DIM] = (oe[:, :A_HEAD_DIM] / oe[:, A_HEAD_DIM:]).astype(o_ref.dtype)


def _gqa(q, k, v, out, *, rpb, n_lat, batch, ctx_queries):
    t = q.shape[0]
    kvh = k.shape[1] // A_HEAD_DIM
    gw = A_GROUP * A_HEAD_DIM
    tq, nq, lk, q_blk, k_blk = _attn_geometry(rpb, n_lat, ctx_queries)
    per_sample = lambda a: a.reshape(batch, rpb, a.shape[1])
    kernel = functools.partial(_gqa_kernel, group=A_GROUP)
    return pl.pallas_call(
        kernel,
        grid=(batch, kvh, nq),
        in_specs=[
            pl.BlockSpec((None, tq, gw), lambda b, h, i: (b, q_blk(i), h)),
            pl.BlockSpec((None, lk, A_HEAD_DIM), lambda b, h, i: (b, k_blk, h)),
            pl.BlockSpec((None, lk, A_HEAD_DIM), lambda b, h, i: (b, k_blk, h)),
            pl.BlockSpec(memory_space=pl.ANY),
        ],
        out_specs=pl.BlockSpec((None, tq, gw), lambda b, h, i: (b, q_blk(i), h)),
        out_shape=jax.ShapeDtypeStruct((batch, rpb, out.shape[1]), BF16),
        scratch_shapes=_attn_scratch(tq, lk, A_HEAD_DIM),
        input_output_aliases={3: 0},
        compiler_params=_params(),
        name="gqa_ctx" if ctx_queries else "gqa_lat",
    )(per_sample(q), per_sample(k), per_sample(v), per_sample(out)).reshape(t, out.shape[1])


def _diff_kernel(lam_ref, g_ref, q_ref, k_ref, v_ref, prev_ref, o_ref, vext_ref, s0_ref, s1_ref, p0_ref, p1_ref,
                 *, lam_init, heads_per_step):
    del prev_ref
    hw = 2 * C_HEAD_DIM
    lv = lam_ref[...]
    lam = (jnp.exp(jnp.sum(lv[0:1] * lv[1:2], axis=-1, keepdims=True))
           - jnp.exp(jnp.sum(lv[2:3] * lv[3:4], axis=-1, keepdims=True)) + lam_init)
    for hh in range(heads_per_step):
        cols = slice(hh * hw, (hh + 1) * hw)

        @pl.when((pl.program_id(2) == 0) | (heads_per_step > 1))
        def _():
            _fill_v_ext(v_ref.at[:, cols], vext_ref)

        q = q_ref[:, cols]
        lane = lax.broadcasted_iota(jnp.int32, q.shape, 1)
        zero = jnp.zeros_like(q)
        qs = [jnp.where(lane < C_HEAD_DIM, q, zero), jnp.where(lane >= C_HEAD_DIM, q, zero)]
        oe1, oe2 = _attend(qs, k_ref.at[:, cols], vext_ref, (s0_ref, s1_ref), (p0_ref, p1_ref))
        o = oe1[:, :hw] / oe1[:, hw:] - lam * (oe2[:, :hw] / oe2[:, hw:])
        o_ref[:, cols] = (_rms(o, g_ref[...]) * (1.0 - lam_init)).astype(o_ref.dtype)


def _diff_attn(qk, v, lamvec, subln_g, out, *, lam_init, rpb, n_lat, batch, ctx_queries):
    t = qk.shape[0]
    heads = v.shape[1] // (2 * C_HEAD_DIM)
    hw = 2 * C_HEAD_DIM
    tq, nq, lk, q_blk, k_blk = _attn_geometry(rpb, n_lat, ctx_queries)
    hs = math.gcd(heads, CTX_HEADS_PER_STEP) if ctx_queries else 1
    assert nq == 1 or hs == 1
    gw = hs * hw
    per_sample = lambda a: a.reshape(batch, rpb, a.shape[1])
    kernel = functools.partial(_diff_kernel, lam_init=lam_init, heads_per_step=hs)
    return pl.pallas_call(
        kernel,
        grid=(batch, heads // hs, nq),
        in_specs=[
            pl.BlockSpec(lamvec.shape, lambda b, h, i: (0, 0)),
            pl.BlockSpec((1, hw), lambda b, h, i: (0, 0)),
            pl.BlockSpec((None, tq, gw), lambda b, h, i: (b, q_blk(i), h)),
            pl.BlockSpec((None, lk, gw), lambda b, h, i: (b, k_blk, heads // hs + h)),
            pl.BlockSpec((None, lk, gw), lambda b, h, i: (b, k_blk, h)),
            pl.BlockSpec(memory_space=pl.ANY),
        ],
        out_specs=pl.BlockSpec((None, tq, gw), lambda b, h, i: (b, q_blk(i), h)),
        out_shape=jax.ShapeDtypeStruct((batch, rpb, v.shape[1]), BF16),
        scratch_shapes=_attn_scratch(tq, lk, hw),
        input_output_aliases={5: 0},
        compiler_params=_params(),
        name="diff_ctx" if ctx_queries else "diff_lat",
    )(lamvec, subln_g.reshape(1, hw).astype(F32), per_sample(qk), per_sample(qk), per_sample(v),
      per_sample(out)).reshape(t, v.shape[1])


CONV_PAD = SUBLANE
SCAN_UNROLL = 8


def _sigmoid(x):
    return 0.5 * jnp.tanh(0.5 * x) + 0.5


def _block_scan(a, u, carry, reverse):
    row = lax.broadcasted_iota(jnp.int32, a.shape, 0)
    for s in (1, 2, 4):
        shift = SUBLANE - s if reverse else s
        valid = (row < SUBLANE - s) if reverse else (row >= s)
        a_prev = pltpu.roll(a, shift, 0)
        u_prev = pltpu.roll(u, shift, 0)
        u = jnp.where(valid, a * u_prev + u, u)
        a = jnp.where(valid, a * a_prev, a)
    h = u + a * carry
    last = h[0:1] if reverse else h[SUBLANE - 1:SUBLANE]
    return h, jnp.broadcast_to(last, h.shape)


def _rglru_kernel(xb_ref, yb_ref, cw_ref, cb_ref, gw_ref, gb_ref, lam_ref, buf_ref, o_ref,
                  xpad_ref, xc_ref, a_ref, u_ref, hf_ref, hb_ref, *, n_lat, n_ctx):
    del buf_ref
    cw = cw_ref[...]
    taps = cw.shape[0]
    zpad = jnp.zeros((CONV_PAD, B_BLOCK), F32)

    def conv_segment(r0, n):
        xpad_ref[0:CONV_PAD] = zpad
        xpad_ref[CONV_PAD:CONV_PAD + n] = xb_ref[r0:r0 + n]
        xpad_ref[CONV_PAD + n:2 * CONV_PAD + n] = zpad
        acc = jnp.broadcast_to(cb_ref[...], (n, B_BLOCK))
        for j in range(taps):
            off = CONV_PAD + j - taps // 2
            acc = acc + cw[j:j + 1] * xpad_ref[off:off + n]
        xc_ref[r0:r0 + n] = acc

    conv_segment(0, n_lat)
    conv_segment(n_lat, n_ctx)

    xc = xc_ref[...]
    gates = _dot(xc.astype(BF16), gw_ref[0]) + gb_ref[0]
    lam = lam_ref[...]
    softplus_neg = jnp.maximum(-lam, 0.0) + jnp.log(1.0 + jnp.exp(-jnp.abs(lam)))
    log2_base = (-LRU_C * LOG2E) * softplus_neg
    for d in range(2):
        r = _sigmoid(gates[:, (2 * d) * B_BLOCK:(2 * d + 1) * B_BLOCK])
        i = _sigmoid(gates[:, (2 * d + 1) * B_BLOCK:(2 * d + 2) * B_BLOCK])
        a = jnp.exp2(r * log2_base[d:d + 1])
        a_ref[d] = a
        u_ref[d] = jnp.sqrt(1.0 - a * a) * i * xc

    def scan_segment(r0, n, carry):
        nblk = n // SUBLANE

        def body(b, c):
            cf, cb = c
            rf = pl.multiple_of(r0 + b * SUBLANE, SUBLANE)
            rb = pl.multiple_of(r0 + (nblk - 1 - b) * SUBLANE, SUBLANE)
            hf, cf = _block_scan(a_ref[0, pl.ds(rf, SUBLANE), :], u_ref[0, pl.ds(rf, SUBLANE), :], cf, False)
            hb, cb = _block_scan(a_ref[1, pl.ds(rb, SUBLANE), :], u_ref[1, pl.ds(rb, SUBLANE), :], cb, True)
            hf_ref[pl.ds(rf, SUBLANE), :] = hf
            hb_ref[pl.ds(rb, SUBLANE), :] = hb
            return cf, cb

        return lax.fori_loop(0, nblk, body, carry, unroll=math.gcd(nblk, SCAN_UNROLL))

    zero = jnp.zeros((SUBLANE, B_BLOCK), F32)
    carry = scan_segment(n_lat, n_ctx, (zero, zero))
    scan_segment(0, n_lat, carry)
    o_ref[...] = ((hf_ref[...] + hb_ref[...]) * _gelu(yb_ref[...])).astype(o_ref.dtype)


def _rglru(xb, yb, conv_w, conv_b, gw, gb, lam, out, col0, *, rpb, n_lat, batch):
    t, width = xb.shape
    assert col0 % B_BLOCK == 0
    nh = width // B_BLOCK
    n_ctx = rpb - n_lat
    taps = conv_w.shape[0]
    return pl.pallas_call(
        functools.partial(_rglru_kernel, n_lat=n_lat, n_ctx=n_ctx),
        grid=(batch, nh),
        in_specs=[
            pl.BlockSpec((rpb, B_BLOCK), lambda b, h: (b, h)),
            pl.BlockSpec((rpb, B_BLOCK), lambda b, h: (b, h)),
            pl.BlockSpec((taps, B_BLOCK), lambda b, h: (0, h)),
            pl.BlockSpec((1, B_BLOCK), lambda b, h: (0, h)),
            pl.BlockSpec((1, B_BLOCK, 4 * B_BLOCK), lambda b, h: (h, 0, 0)),
            pl.BlockSpec((1, 1, 4 * B_BLOCK), lambda b, h: (h, 0, 0)),
            pl.BlockSpec((2, B_BLOCK), lambda b, h: (0, h)),
            pl.BlockSpec(memory_space=pl.ANY),
        ],
        out_specs=pl.BlockSpec((rpb, B_BLOCK), lambda b, h: (b, col0 // B_BLOCK + h)),
        out_shape=jax.ShapeDtypeStruct(out.shape, BF16),
        input_output_aliases={7: 0},
        scratch_shapes=[
            pltpu.VMEM((n_lat + 2 * CONV_PAD, B_BLOCK), F32),
            pltpu.VMEM((rpb, B_BLOCK), F32),
            pltpu.VMEM((2, rpb, B_BLOCK), F32),
            pltpu.VMEM((2, rpb, B_BLOCK), F32),
            pltpu.VMEM((rpb, B_BLOCK), F32),
            pltpu.VMEM((rpb, B_BLOCK), F32),
        ],
        compiler_params=_params(),
        name="rglru",
    )(xb, yb, conv_w, conv_b.reshape(1, width), gw, gb, lam, out)


CAST_ROWS = 256


def _expert_weight_copies(wgu_hbm, wd_hbm, wgu_f32, wd_f32, sem, layer, expert, slot):
    return (pltpu.make_async_copy(wgu_hbm.at[layer, expert], wgu_f32.at[slot], sem.at[slot, 0]),
            pltpu.make_async_copy(wd_hbm.at[layer, expert], wd_f32.at[slot], sem.at[slot, 1]))


def _expert_kernel(te_ref, nx_ref, sl_ref, nu_ref, x_ref, wgu_hbm, bgu_ref, wd_hbm, bd_ref, o_ref,
                   wgu_f32, wd_f32, wgu_bf, wd_bf, sem, *, f, layer):
    j = pl.program_id(0)
    expert = te_ref[j]
    slot = sl_ref[j]
    copies = functools.partial(_expert_weight_copies, wgu_hbm, wd_hbm, wgu_f32, wd_f32, sem, layer)

    @pl.when(j == 0)
    def _():
        for c in copies(expert, slot):
            c.start()

    @pl.when((j == 0) | (expert != te_ref[jnp.maximum(j - 1, 0)]))
    def _():
        for c in copies(expert, slot):
            c.wait()
        nxt = nx_ref[j]

        @pl.when(nxt >= 0)
        def _():
            for c in copies(nxt, 1 - slot):
                c.start()

        for r in range(0, wgu_bf.shape[0], CAST_ROWS):
            wgu_bf[r:r + CAST_ROWS] = wgu_f32[slot, r:r + CAST_ROWS].astype(BF16)
        for r in range(0, wd_bf.shape[0], CAST_ROWS):
            wd_bf[r:r + CAST_ROWS] = wd_f32[slot, r:r + CAST_ROWS].astype(BF16)

    @pl.when(j < nu_ref[0])
    def _():
        gu = _dot(x_ref[...], wgu_bf[...]) + bgu_ref[0, 0]
        gate = jnp.minimum(gu[:, :f], SWIGLU_LIMIT)
        up = jnp.clip(gu[:, f:], -SWIGLU_LIMIT, SWIGLU_LIMIT)
        act = (up + 1.0) * gate * jax.nn.sigmoid(gate * SWIGLU_ALPHA)
        o_ref[...] = (_dot(act.astype(BF16), wd_bf[...]) + bd_ref[0, 0]).astype(o_ref.dtype)

    @pl.when(j >= nu_ref[0])
    def _():
        o_ref[...] = jnp.zeros(o_ref.shape, o_ref.dtype)


def _experts(x_sorted, tile_expert, n_used, w_gu, b_gu, w_d, b_d, *, layer, tm):
    r, d = x_sorted.shape
    depth, e, _, f2 = w_gu.shape
    f = f2 // 2
    assert d % CAST_ROWS == 0 and f % CAST_ROWS == 0
    n_tiles = tile_expert.shape[0]
    first = jnp.concatenate([jnp.ones((1,), jnp.bool_), tile_expert[1:] != tile_expert[:-1]])
    slot = ((jnp.cumsum(first.astype(jnp.int32)) - 1) % 2).astype(jnp.int32)
    later = jnp.where(tile_expert[None, :] > tile_expert[:, None], tile_expert[None, :], e)
    next_expert = jnp.min(later, axis=1)
    next_expert = jnp.where(next_expert < e, next_expert, -1).astype(jnp.int32)
    grid_spec = pltpu.PrefetchScalarGridSpec(
        num_scalar_prefetch=4,
        grid=(n_tiles,),
        in_specs=[
            pl.BlockSpec((tm, d), lambda j, *_: (j, 0)),
            pl.BlockSpec(memory_space=pl.ANY),
            pl.BlockSpec((1, 1, 1, f2), lambda j, te, *_: (layer, te[j], 0, 0)),
            pl.BlockSpec(memory_space=pl.ANY),
            pl.BlockSpec((1, 1, 1, d), lambda j, te, *_: (layer, te[j], 0, 0)),
        ],
        out_specs=pl.BlockSpec((tm, d), lambda j, *_: (j, 0)),
        scratch_shapes=[pltpu.VMEM((2, d, f2), F32), pltpu.VMEM((2, f, d), F32),
                        pltpu.VMEM((d, f2), BF16), pltpu.VMEM((f, d), BF16),
                        pltpu.SemaphoreType.DMA((2, 2))],
    )
    assert r == n_tiles * tm
    return pl.pallas_call(
        functools.partial(_expert_kernel, f=f, layer=layer),
        grid_spec=grid_spec,
        out_shape=jax.ShapeDtypeStruct((r, d), BF16),
        compiler_params=_params(),
        name="experts",
    )(tile_expert, next_expert, slot, n_used, x_sorted, w_gu, b_gu.reshape(depth, e, 1, f2), w_d,
      b_d.reshape(depth, e, 1, d))


def _combine_kernel(y_ref, wt_ref, x_ref, modb_ref, modc_ref, *rest, k, bm, rpb, n_lat, tile0, with_next):
    is_ctx = _ctx_rows(pl.program_id(0) + tile0, bm, rpb, n_lat)
    wt = wt_ref[...]
    y = wt[:, 0:1] * y_ref[0].astype(F32)
    for j in range(1, TOP_K):
        y = y + wt[:, j:j + 1] * y_ref[j].astype(F32)
    x_new = x_ref[...] + _mod_row(modb_ref, modc_ref, k, is_ctx) * y
    if not with_next:
        (o_ref,) = rest
        o_ref[...] = x_new
        return
    g_ref, nmodb_ref, nmodc_ref, _, o_ref, h_ref = rest
    o_ref[...] = x_new
    h = _rms(x_new, g_ref[...]) * (1.0 + _mod_row(nmodb_ref, nmodc_ref, 1, is_ctx)) + _mod_row(nmodb_ref, nmodc_ref, 0, is_ctx)
    h_ref[...] = h.astype(BF16)


def _combine(y4, wt, xs, mod3, *, row0, k, rpb, n_lat, batch, nxt=None):
    t, d = xs.shape
    rows = y4.shape[1]
    bm = _tile(rpb, 256)
    assert row0 % bm == 0 and rows % bm == 0
    tile0 = row0 // bm
    row_spec = lambda width: pl.BlockSpec((bm, width), lambda i: (i + tile0, 0))
    mod_specs = [pl.BlockSpec((1, 6, d), lambda i: (((i + tile0) * bm) // rpb, 0, 0)),
                 pl.BlockSpec((1, 6, d), lambda i: (batch, 0, 0))]
    in_specs = [pl.BlockSpec((TOP_K, bm, d), lambda i: (0, i, 0)), row_spec(LANE), row_spec(d)] + mod_specs
    args = [y4, wt, xs, mod3, mod3]
    out_specs, out_shape, aliases = [row_spec(d)], [jax.ShapeDtypeStruct((t, d), F32)], {2: 0}
    if nxt is not None:
        g_next, mod3_next, h = nxt
        in_specs += [pl.BlockSpec((1, d), lambda i: (0, 0))] + mod_specs + [pl.BlockSpec(memory_space=pl.ANY)]
        args += [g_next.reshape(1, d), mod3_next, mod3_next, h]
        out_specs.append(row_spec(d))
        out_shape.append(jax.ShapeDtypeStruct((t, d), BF16))
        aliases[len(args) - 1] = 1
    out = pl.pallas_call(
        functools.partial(_combine_kernel, k=k, bm=bm, rpb=rpb, n_lat=n_lat, tile0=tile0, with_next=nxt is not None),
        grid=(rows // bm,),
        in_specs=in_specs,
        out_specs=out_specs,
        out_shape=out_shape,
        input_output_aliases=aliases,
        compiler_params=_params(),
        name="combine",
    )(*args)
    return (out[0], out[1]) if nxt is not None else (out[0], None)


def _dispatch_plan(top_idx, n_experts, tm):
    t = top_idx.shape[0]
    chosen = (top_idx[:, :, None] == jnp.arange(n_experts, dtype=jnp.int32)[None, None, :]).any(axis=1)
    chosen = chosen.astype(jnp.int32)
    before = jnp.cumsum(chosen, axis=0) - chosen
    counts = before[-1] + chosen[-1]
    padded = ((counts + tm - 1) // tm) * tm
    group_end = jnp.cumsum(padded)
    group_start = group_end - padded
    dest = group_start[top_idx] + jnp.take_along_axis(before, top_idx, axis=1)
    n_rows = TOP_K * t + n_experts * tm
    n_tiles = n_rows // tm
    tile_expert = jnp.minimum(
        jnp.sum(jnp.arange(n_tiles, dtype=jnp.int32)[:, None] >= (group_end // tm)[None, :], axis=1), n_experts - 1
    ).astype(jnp.int32)
    n_used = (group_end[-1] // tm).astype(jnp.int32).reshape(1)
    pad = jnp.arange(tm, dtype=jnp.int32)[None, :]
    pad_row = (group_start + counts)[:, None] + pad
    val_bits = max(t - 1, 1).bit_length()
    key_limit = 1 << (32 - val_bits)
    assert n_rows < key_limit
    pad_key = jnp.where(pad_row < group_end[:, None], pad_row, key_limit - 1).reshape(-1)
    keys = jnp.concatenate([dest.reshape(-1), pad_key])
    vals = jnp.concatenate([jnp.repeat(jnp.arange(t, dtype=jnp.int32), TOP_K),
                            jnp.arange(n_experts * tm, dtype=jnp.int32) % t])
    packed = (keys.astype(jnp.uint32) << val_bits) | vals.astype(jnp.uint32)
    row_token = (lax.sort(packed) & jnp.uint32((1 << val_bits) - 1)).astype(jnp.int32)
    return dest, row_token, tile_expert, n_used


def _moe(xs, g, mod3, router_w, router_b, w_gu, b_gu, w_d, b_d, *, layer, rpb, n_lat, batch, nxt=None):
    t, d = xs.shape
    e = router_w.shape[1]
    splits = MOE_TOKEN_SPLITS if batch % MOE_TOKEN_SPLITS == 0 else 1
    ts = t // splits
    h2, idx, wt = _modulate_route(xs, g, mod3, router_w, router_b, k=3, rpb=rpb, n_lat=n_lat, batch=batch)
    h = h2
    plans = [_dispatch_plan(idx[s * ts:(s + 1) * ts, :TOP_K], e, MOE_ROW_TILE) for s in range(splits)]
    sorted_x = [jnp.take(h2, row_token + s * ts, axis=0, mode="clip") for s, (_, row_token, _, _) in enumerate(plans)]
    for s, (dest, _, tile_expert, n_used) in enumerate(plans):
        rows = _experts(sorted_x[s], tile_expert, n_used, w_gu, b_gu, w_d, b_d, layer=layer, tm=MOE_ROW_TILE)
        y4 = jnp.take(rows, dest.T.reshape(-1), axis=0, mode="clip").reshape(TOP_K, ts, d)
        xs, h = _combine(y4, wt, xs, mod3, row0=s * ts, k=5, rpb=rpb, n_lat=n_lat, batch=batch,
                         nxt=None if nxt is None else (*nxt, h))
    return xs, h


def _final_kernel(x_ref, g_ref, o_ref):
    o_ref[0] = _rms(x_ref[0], g_ref[...])


def _final_norm(xs3, g, n_lat):
    batch, rpb, d = xs3.shape
    bm = _tile(n_lat, 256)
    return pl.pallas_call(
        _final_kernel,
        grid=(batch, n_lat // bm),
        in_specs=[pl.BlockSpec((1, bm, d), lambda b, i: (b, i, 0)), pl.BlockSpec((1, d), lambda b, i: (0, 0))],
        out_specs=pl.BlockSpec((1, bm, d), lambda b, i: (b, i, 0)),
        out_shape=jax.ShapeDtypeStruct((batch, n_lat, d), F32),
        compiler_params=_params(),
        name="final_norm",
    )(xs3, g.reshape(1, d))


def _rope_tables(n_lat, n_ctx, head_dim):
    n_freq = head_dim // 4
    inv_freq = ROPE_THETA ** (-jnp.arange(n_freq, dtype=F32) / n_freq)
    tok = jnp.arange(n_lat, dtype=jnp.int32)
    ang_r = (tok // GRID_W).astype(F32)[:, None] * inv_freq
    ang_c = (tok % GRID_W).astype(F32)[:, None] * inv_freq
    cos = jnp.concatenate([jnp.cos(ang_r), jnp.cos(ang_r), jnp.cos(ang_c), jnp.cos(ang_c)], axis=-1)
    sin = jnp.concatenate([-jnp.sin(ang_r), jnp.sin(ang_r), -jnp.sin(ang_c), jnp.sin(ang_c)], axis=-1)
    reps = LANE // head_dim
    cos = jnp.concatenate([jnp.tile(cos, (1, reps)), jnp.ones((n_ctx, LANE), F32)], axis=0)
    sin = jnp.concatenate([jnp.tile(sin, (1, reps)), jnp.zeros((n_ctx, LANE), F32)], axis=0)
    return cos, sin, n_freq


def kernel(x, c, ctx, c_ctx, mod_w, mod_b, norm1_g, norm2_g, hyb_w_in, hyb_w_out, a_q_norm_g, a_k_norm_g, b_conv_w, b_conv_b, b_gate_a_w, b_gate_a_b, b_gate_x_w, b_gate_x_b, b_lru_lambda, c_w_in, c_w_out, c_lambda_q1, c_lambda_k1, c_lambda_q2, c_lambda_k2, c_subln_g, router_w, router_b, exp_w_gu, exp_b_gu, exp_w_d, exp_b_d, final_g):
    batch, n_lat, d = x.shape
    n_ctx = ctx.shape[1]
    rpb = n_lat + n_ctx
    t = batch * rpb
    depth = mod_w.shape[0]
    geo = dict(rpb=rpb, n_lat=n_lat, batch=batch)

    xs = jnp.concatenate([x, ctx], axis=1).reshape(t, d)
    mod_rows = -(-(batch + 1) // SUBLANE) * SUBLANE
    cvec = jnp.zeros((mod_rows, d), F32).at[:batch].set(c).at[batch].set(c_ctx)
    mod = _modmat(cvec, mod_w, mod_b)

    rope_a = _rope_tables(n_lat, n_ctx, A_HEAD_DIM)
    rope_c = _rope_tables(n_lat, n_ctx, C_HEAD_DIM)
    q_w = A_GROUP * A_HEAD_DIM * (hyb_w_in.shape[2] - 2 * (d // 2)) // ((A_GROUP + 2) * A_HEAD_DIM)
    kv_w = q_w // A_GROUP
    b_w = d // 2
    nh = b_w // B_BLOCK

    mod3s = [mod[l].reshape(mod_rows, 6, d) for l in range(depth)]
    h = _modulate(xs, norm1_g[0], mod3s[0], k=0, **geo)
    for l in range(depth):
        i = l // 2
        mod3 = mod3s[l]
        if l % 2 == 0:
            w_in = hyb_w_in[i].astype(BF16)
            o = 0
            q = _proj(h, w_in[:, o:o + q_w], rpb=rpb, out_dtype=BF16, norm_g=a_q_norm_g[i], rope=rope_a,
                      q_cols=q_w, q_scale=A_HEAD_DIM ** -0.5 * LOG2E)
            o += q_w
            kk = _proj(h, w_in[:, o:o + kv_w], rpb=rpb, out_dtype=BF16, norm_g=a_k_norm_g[i], rope=rope_a)
            o += kv_w
            v = _proj(h, w_in[:, o:o + kv_w], rpb=rpb, out_dtype=BF16)
            o += kv_w
            xb = _proj(h, w_in[:, o:o + b_w], rpb=rpb, out_dtype=F32)
            o += b_w
            yb = _proj(h, w_in[:, o:o + b_w], rpb=rpb, out_dtype=F32)
            mix = _gqa(q, kk, v, h, ctx_queries=False, **geo)
            mix = _gqa(q, kk, v, mix, ctx_queries=True, **geo)
            gw = jnp.concatenate([b_gate_a_w[i, 0], b_gate_x_w[i, 0], b_gate_a_w[i, 1], b_gate_x_w[i, 1]],
                                 axis=-1).astype(BF16)
            gb = jnp.concatenate([b_gate_a_b[i, 0].reshape(nh, 1, B_BLOCK), b_gate_x_b[i, 0].reshape(nh, 1, B_BLOCK),
                                  b_gate_a_b[i, 1].reshape(nh, 1, B_BLOCK), b_gate_x_b[i, 1].reshape(nh, 1, B_BLOCK)],
                                 axis=-1)
            mix = _rglru(xb, yb, b_conv_w[i], b_conv_b[i], gw, gb, b_lru_lambda[i], mix, q_w, **geo)
            pairs = [(mix, hyb_w_out[i].astype(BF16))]
        else:
            lam_init = 0.8 - 0.6 * math.exp(-0.3 * l)
            w_in = c_w_in[i].astype(BF16)
            qk_w = 2 * (w_in.shape[1] // 3)
            qk = _proj(h, w_in[:, :qk_w], rpb=rpb, out_dtype=BF16, rope=rope_c,
                       q_cols=qk_w // 2, q_scale=C_HEAD_DIM ** -0.5 * LOG2E)
            v = _proj(h, w_in[:, qk_w:], rpb=rpb, out_dtype=BF16)
            lamvec = jnp.stack([c_lambda_q1[i], c_lambda_k1[i], c_lambda_q2[i], c_lambda_k2[i]]).astype(F32)
            att = _diff_attn(qk, v, lamvec, c_subln_g[i], h, lam_init=lam_init, ctx_queries=False, **geo)
            att = _diff_attn(qk, v, lamvec, c_subln_g[i], att, lam_init=lam_init, ctx_queries=True, **geo)
            pairs = [(att, c_w_out[i].astype(BF16))]
        xs = _resid_proj(xs, mod3, pairs, k=2, **geo)
        xs, h = _moe(xs, norm2_g[l], mod3, router_w[l], router_b[l], exp_w_gu, exp_b_gu, exp_w_d, exp_b_d,
                     layer=l, nxt=(norm1_g[l + 1], mod3s[l + 1]) if l + 1 < depth else None, **geo)
    return _final_norm(xs.reshape(batch, rpb, d), final_g, n_lat)
```

```python
import functools
import math

import jax
import jax.numpy as jnp
from jax import lax
from jax.experimental import pallas as pl
from jax.experimental.pallas import tpu as pltpu

F32 = jnp.float32
BF16 = jnp.bfloat16

GRID_W = 64
ROPE_THETA = 10000.0
NORM_EPS = 1e-6
A_HEAD_DIM = 128
A_GROUP = 4
B_BLOCK = 128
LRU_C = 8.0
C_HEAD_DIM = 64
TOP_K = 4
SWIGLU_LIMIT = 7.0
SWIGLU_ALPHA = 1.702
LOG2E = math.log2(math.e)

LANE = 128
SUBLANE = 8
VMEM_LIMIT_BYTES = 56 * 1024 * 1024
ATTN_Q_ROWS = 1024
CTX_HEADS_PER_STEP = 8
ATTN_KEY_CHUNK = 256
PROJ_COLS = 2048
PROJ_TILE_ELEMS = 1152 * 1024
PROJ_EPILOGUE_COLS = 256
MOE_ROW_TILE = 512
ROW_RING = 3
MOE_TOKEN_SPLITS = 2


def _params():
    return pltpu.CompilerParams(vmem_limit_bytes=VMEM_LIMIT_BYTES)


def _tile(n, target, mult=SUBLANE):
    best = None
    for t in range(mult, min(n, target) + 1, mult):
        if n % t == 0:
            best = t
    assert best is not None, (n, target, mult)
    return best


def _ctx_rows(i, bm, rpb, n_lat):
    pos0 = lax.rem(i * bm, rpb)
    row = lax.broadcasted_iota(jnp.int32, (bm, 1), 0)
    return (pos0 + row) >= n_lat


def _mod_row(modb_ref, modc_ref, k, is_ctx):
    return jnp.where(is_ctx, modc_ref[0, k:k + 1, :], modb_ref[0, k:k + 1, :])


def _rms(x, g):
    return x * lax.rsqrt(jnp.mean(x * x, axis=-1, keepdims=True) + NORM_EPS) * g


def _dot(a, b):
    return jnp.dot(a, b, preferred_element_type=F32)


def _dot_nt(a, b):
    return lax.dot_general(a, b, (((1,), (1,)), ((), ())), preferred_element_type=F32)


def _modmat_kernel(c_ref, w_ref, b_ref, o_ref):
    c = c_ref[...]
    s = (c * jax.nn.sigmoid(c)).astype(BF16)
    o_ref[0] = _dot(s, w_ref[0].astype(BF16)) + b_ref[0]


def _modmat(cvec, mod_w, mod_b):
    depth, d, n = mod_w.shape
    r = cvec.shape[0]
    bn = _tile(n, 512, LANE)
    return pl.pallas_call(
        _modmat_kernel,
        grid=(depth, n // bn),
        in_specs=[
            pl.BlockSpec((r, d), lambda l, j: (0, 0)),
            pl.BlockSpec((1, d, bn), lambda l, j: (l, 0, j)),
            pl.BlockSpec((1, 1, bn), lambda l, j: (l, 0, j)),
        ],
        out_specs=pl.BlockSpec((1, r, bn), lambda l, j: (l, 0, j)),
        out_shape=jax.ShapeDtypeStruct((depth, r, n), F32),
        compiler_params=_params(),
        name="modmat",
    )(cvec, mod_w, mod_b.reshape(depth, 1, n))


def _modulate_kernel(x_ref, g_ref, modb_ref, modc_ref, h_ref, *, k, bm, rpb, n_lat):
    is_ctx = _ctx_rows(pl.program_id(0), bm, rpb, n_lat)
    y = _rms(x_ref[...], g_ref[...])
    h = y * (1.0 + _mod_row(modb_ref, modc_ref, k + 1, is_ctx)) + _mod_row(modb_ref, modc_ref, k, is_ctx)
    h_ref[...] = h.astype(BF16)


def _mod_specs(bm, rpb, batch, d):
    return [
        pl.BlockSpec((1, 6, d), lambda i: ((i * bm) // rpb, 0, 0)),
        pl.BlockSpec((1, 6, d), lambda i: (batch, 0, 0)),
    ]


def _modulate(xs, g, mod3, *, k, rpb, n_lat, batch):
    t, d = xs.shape
    bm = _tile(rpb, 256)
    return pl.pallas_call(
        functools.partial(_modulate_kernel, k=k, bm=bm, rpb=rpb, n_lat=n_lat),
        grid=(t // bm,),
        in_specs=[pl.BlockSpec((bm, d), lambda i: (i, 0)), pl.BlockSpec((1, d), lambda i: (0, 0))]
        + _mod_specs(bm, rpb, batch, d),
        out_specs=pl.BlockSpec((bm, d), lambda i: (i, 0)),
        out_shape=jax.ShapeDtypeStruct((t, d), BF16),
        compiler_params=_params(),
        name="modulate",
    )(xs, g.reshape(1, d), mod3, mod3)


def _router_kernel(x_ref, g_ref, modb_ref, modc_ref, rwh_ref, rwl_ref, rb_ref, h_ref, idx_ref, wt_ref,
                   *, k, bm, rpb, n_lat):
    is_ctx = _ctx_rows(pl.program_id(0), bm, rpb, n_lat)
    y = _rms(x_ref[...], g_ref[...])
    h = y * (1.0 + _mod_row(modb_ref, modc_ref, k + 1, is_ctx)) + _mod_row(modb_ref, modc_ref, k, is_ctx)
    h_hi = h.astype(BF16)
    h_lo = (h - h_hi.astype(F32)).astype(BF16)
    h_ref[...] = h_hi
    logits = _dot(h_hi, rwh_ref[...]) + (_dot(h_lo, rwh_ref[...]) + _dot(h_hi, rwl_ref[...])) + rb_ref[...]
    lane = lax.broadcasted_iota(jnp.int32, logits.shape, 1)
    vals, idxs = [], []
    for _ in range(TOP_K):
        m = jnp.max(logits, axis=-1, keepdims=True)
        ik = jnp.min(jnp.where(logits == m, lane, LANE), axis=-1, keepdims=True)
        vals.append(m)
        idxs.append(ik)
        logits = jnp.where(lane == ik, -jnp.inf, logits)
    es = [jnp.exp(v - vals[0]) for v in vals]
    den = es[0] + es[1] + es[2] + es[3]
    idx_out = jnp.zeros(lane.shape, jnp.int32)
    wt_out = jnp.zeros(lane.shape, F32)
    for j in range(TOP_K):
        idx_out = jnp.where(lane == j, idxs[j], idx_out)
        wt_out = jnp.where(lane == j, es[j] / den, wt_out)
    idx_ref[...] = idx_out
    wt_ref[...] = wt_out


def _modulate_route(xs, g, mod3, router_w, router_b, *, k, rpb, n_lat, batch):
    t, d = xs.shape
    e = router_w.shape[1]
    bm = _tile(rpb, 256)
    rw = jnp.zeros((d, LANE), F32).at[:, :e].set(router_w)
    rw_hi = rw.astype(BF16)
    rw_lo = (rw - rw_hi.astype(F32)).astype(BF16)
    rb = jnp.full((1, LANE), -1e30, F32).at[0, :e].set(router_b)
    return pl.pallas_call(
        functools.partial(_router_kernel, k=k, bm=bm, rpb=rpb, n_lat=n_lat),
        grid=(t // bm,),
        in_specs=[pl.BlockSpec((bm, d), lambda i: (i, 0)), pl.BlockSpec((1, d), lambda i: (0, 0))]
        + _mod_specs(bm, rpb, batch, d)
        + [pl.BlockSpec((d, LANE), lambda i: (0, 0))] * 2 + [pl.BlockSpec((1, LANE), lambda i: (0, 0))],
        out_specs=[pl.BlockSpec((bm, d), lambda i: (i, 0)),
                   pl.BlockSpec((bm, LANE), lambda i: (i, 0)),
                   pl.BlockSpec((bm, LANE), lambda i: (i, 0))],
        out_shape=[jax.ShapeDtypeStruct((t, d), BF16),
                   jax.ShapeDtypeStruct((t, LANE), jnp.int32),
                   jax.ShapeDtypeStruct((t, LANE), F32)],
        compiler_params=_params(),
        name="modulate_route",
    )(xs, g.reshape(1, d), mod3, mod3, rw_hi, rw_lo, rb)


def _gelu(x):
    return 0.5 * x * (1.0 + jnp.tanh(math.sqrt(2.0 / math.pi) * (x + 0.044715 * (x * x * x))))


def _proj_kernel(*refs, norm, rope_shift, q_cols, q_scale, bn):
    a_ref, w_ref = refs[0], refs[1]
    o_ref = refs[-1]
    pos = 2
    g_ref = cos_ref = sin_ref = None
    if norm:
        g_ref = refs[pos]
        pos += 1
    if rope_shift:
        cos_ref, sin_ref = refs[pos], refs[pos + 1]
    if not norm and not rope_shift:
        o_ref[...] = _dot(a_ref[...], w_ref[...]).astype(o_ref.dtype)
        return
    col_scale = jnp.where(pl.program_id(0) * bn < q_cols, q_scale, 1.0)
    lane = lax.broadcasted_iota(jnp.int32, (a_ref.shape[0], LANE), 1)
    first_half = jnp.bitwise_and(lane, 2 * rope_shift - 1) < rope_shift
    step = math.gcd(bn, PROJ_EPILOGUE_COLS)
    for c0 in range(0, bn, step):
        acc = _dot(a_ref[...], w_ref[:, c0:c0 + step])
        for c in range(0, step, LANE):
            y = acc[:, c:c + LANE]
            if norm:
                y = _rms(y, g_ref[...])
            partner = jnp.where(first_half, pltpu.roll(y, LANE - rope_shift, 1), pltpu.roll(y, rope_shift, 1))
            y = (y * cos_ref[...] + partner * sin_ref[...]) * col_scale
            o_ref[:, c0 + c:c0 + c + LANE] = y.astype(o_ref.dtype)


def _proj(a, w, *, rpb, out_dtype, norm_g=None, rope=None, q_cols=0, q_scale=1.0):
    t, kdim = a.shape
    n = w.shape[1]
    bn = _tile(math.gcd(n, q_cols) if q_cols else n, PROJ_COLS, LANE)
    bm = _tile(rpb, PROJ_TILE_ELEMS // bn, 16)
    tiles_per_sample = rpb // bm
    in_specs = [pl.BlockSpec((bm, kdim), lambda j, i: (i, 0)), pl.BlockSpec((kdim, bn), lambda j, i: (0, j))]
    args = [a, w]
    if norm_g is not None:
        in_specs.append(pl.BlockSpec((1, LANE), lambda j, i: (0, 0)))
        args.append(norm_g.reshape(1, LANE).astype(F32))
    rope_shift = 0
    if rope is not None:
        cos, sin, rope_shift = rope
        in_specs += [pl.BlockSpec((bm, LANE), lambda j, i: (i % tiles_per_sample, 0))] * 2
        args += [cos, sin]
    return pl.pallas_call(
        functools.partial(_proj_kernel, norm=norm_g is not None, rope_shift=rope_shift, q_cols=q_cols,
                          q_scale=q_scale, bn=bn),
        grid=(n // bn, t // bm),
        in_specs=in_specs,
        out_specs=pl.BlockSpec((bm, bn), lambda j, i: (i, j)),
        out_shape=jax.ShapeDtypeStruct((t, n), out_dtype),
        compiler_params=_params(),
        name="proj",
    )(*args)


def _resid_kernel(*refs, n_pairs, k, bm, rpb, n_lat):
    x_ref, modb_ref, modc_ref = refs[0], refs[1], refs[2]
    o_ref = refs[-1]
    acc = None
    for p in range(n_pairs):
        part = _dot(refs[3 + 2 * p][...], refs[4 + 2 * p][...])
        acc = part if acc is None else acc + part
    is_ctx = _ctx_rows(pl.program_id(1), bm, rpb, n_lat)
    o_ref[...] = x_ref[...] + _mod_row(modb_ref, modc_ref, k, is_ctx) * acc


def _resid_proj(xs, mod3, pairs, *, k, rpb, n_lat, batch):
    t, d = xs.shape
    bm = _tile(rpb, 768, 16)
    bn = _tile(d, 1024, LANE)
    in_specs = [
        pl.BlockSpec((bm, bn), lambda j, i: (i, j)),
        pl.BlockSpec((1, 6, bn), lambda j, i: ((i * bm) // rpb, 0, j)),
        pl.BlockSpec((1, 6, bn), lambda j, i: (batch, 0, j)),
    ]
    args = [xs, mod3, mod3]
    for a, w in pairs:
        kdim = a.shape[1]
        in_specs += [pl.BlockSpec((bm, kdim), lambda j, i: (i, 0)), pl.BlockSpec((kdim, bn), lambda j, i: (0, j))]
        args += [a, w]
    return pl.pallas_call(
        functools.partial(_resid_kernel, n_pairs=len(pairs), k=k, bm=bm, rpb=rpb, n_lat=n_lat),
        grid=(d // bn, t // bm),
        in_specs=in_specs,
        out_specs=pl.BlockSpec((bm, bn), lambda j, i: (i, j)),
        out_shape=jax.ShapeDtypeStruct((t, d), F32),
        compiler_params=_params(),
        name="resid_proj",
    )(*args)


def _fill_v_ext(v_ref, vext_ref):
    width = v_ref.shape[1]
    vext_ref[:, :width] = v_ref[...]
    vext_ref[:, width:] = jnp.ones((v_ref.shape[0], vext_ref.shape[1] - width), vext_ref.dtype)


def _attend(qs, k_ref, vext_ref, s_refs, p_refs):
    tq = qs[0].shape[0]
    lk = k_ref.shape[0]
    kc = _tile(lk, ATTN_KEY_CHUNK, LANE)

    def scores(q, c, s_ref, lane_max):
        s = _dot_nt(q, k_ref[c * kc:(c + 1) * kc])
        s_ref[:, c * kc:(c + 1) * kc] = s
        for j in range(kc // LANE):
            lane_max = jnp.maximum(lane_max, s[:, j * LANE:(j + 1) * LANE])
        return lane_max

    def exponentials(s_ref, p_ref, c, row_max):
        p_ref[:, c * kc:(c + 1) * kc] = jnp.exp2(s_ref[:, c * kc:(c + 1) * kc] - row_max).astype(BF16)

    lowest = jnp.full((tq, LANE), -jnp.inf, F32)
    lane_max = lowest
    for c in range(lk // kc):
        lane_max = scores(qs[0], c, s_refs[0], lane_max)
    row_max = jnp.max(lane_max, axis=-1, keepdims=True)
    for g in range(len(qs)):
        cur, nxt = g % 2, (g + 1) % 2
        lane_max = lowest
        for c in range(lk // kc):
            if g + 1 < len(qs):
                lane_max = scores(qs[g + 1], c, s_refs[nxt], lane_max)
            exponentials(s_refs[cur], p_refs[cur], c, row_max)
        if g + 1 < len(qs):
            row_max = jnp.max(lane_max, axis=-1, keepdims=True)
        yield _dot(p_refs[cur][...], vext_ref[...])


def _attn_geometry(rpb, n_lat, ctx_queries):
    n_ctx = rpb - n_lat
    if ctx_queries:
        assert n_lat % n_ctx == 0
        return n_ctx, 1, n_ctx, (lambda i: n_lat // n_ctx), n_lat // n_ctx
    tq = _tile(n_lat, ATTN_Q_ROWS)
    return tq, n_lat // tq, rpb, (lambda i: i), 0


def _attn_scratch(tq, lk, width):
    return [pltpu.VMEM((lk, 2 * width), BF16),
            pltpu.VMEM((tq, lk), F32), pltpu.VMEM((tq, lk), F32),
            pltpu.VMEM((tq, lk), BF16), pltpu.VMEM((tq, lk), BF16)]


def _gqa_kernel(q_ref, k_ref, v_ref, prev_ref, o_ref, vext_ref, s0_ref, s1_ref, p0_ref, p1_ref, *, group):
    del prev_ref

    @pl.when(pl.program_id(2) == 0)
    def _():
        _fill_v_ext(v_ref, vext_ref)

    qs = [q_ref[:, g * A_HEAD_DIM:(g + 1) * A_HEAD_DIM] for g in range(group)]
    for g, oe in enumerate(_attend(qs, k_ref, vext_ref, (s0_ref, s1_ref), (p0_ref, p1_ref))):
        o_ref[:, g * A_HEAD_DIM:(g + 1) * A_HEAD_DIM] = (oe[:, :A_HEAD_DIM] / oe[:, A_HEAD_DIM:]).astype(o_ref.dtype)


def _gqa(q, k, v, out, *, rpb, n_lat, batch, ctx_queries):
    t = q.shape[0]
    kvh = k.shape[1] // A_HEAD_DIM
    gw = A_GROUP * A_HEAD_DIM
    tq, nq, lk, q_blk, k_blk = _attn_geometry(rpb, n_lat, ctx_queries)
    per_sample = lambda a: a.reshape(batch, rpb, a.shape[1])
    kernel = functools.partial(_gqa_kernel, group=A_GROUP)
    return pl.pallas_call(
        kernel,
        grid=(batch, kvh, nq),
        in_specs=[
            pl.BlockSpec((None, tq, gw), lambda b, h, i: (b, q_blk(i), h)),
            pl.BlockSpec((None, lk, A_HEAD_DIM), lambda b, h, i: (b, k_blk, h)),
            pl.BlockSpec((None, lk, A_HEAD_DIM), lambda b, h, i: (b, k_blk, h)),
            pl.BlockSpec(memory_space=pl.ANY),
        ],
        out_specs=pl.BlockSpec((None, tq, gw), lambda b, h, i: (b, q_blk(i), h)),
        out_shape=jax.ShapeDtypeStruct((batch, rpb, out.shape[1]), BF16),
        scratch_shapes=_attn_scratch(tq, lk, A_HEAD_DIM),
        input_output_aliases={3: 0},
        compiler_params=_params(),
        name="gqa_ctx" if ctx_queries else "gqa_lat",
    )(per_sample(q), per_sample(k), per_sample(v), per_sample(out)).reshape(t, out.shape[1])


def _diff_kernel(lam_ref, g_ref, q_ref, k_ref, v_ref, prev_ref, o_ref, vext_ref, s0_ref, s1_ref, p0_ref, p1_ref,
                 *, lam_init, heads_per_step):
    del prev_ref
    hw = 2 * C_HEAD_DIM
    lv = lam_ref[...]
    lam = (jnp.exp(jnp.sum(lv[0:1] * lv[1:2], axis=-1, keepdims=True))
           - jnp.exp(jnp.sum(lv[2:3] * lv[3:4], axis=-1, keepdims=True)) + lam_init)
    for hh in range(heads_per_step):
        cols = slice(hh * hw, (hh + 1) * hw)

        @pl.when((pl.program_id(2) == 0) | (heads_per_step > 1))
        def _():
            _fill_v_ext(v_ref.at[:, cols], vext_ref)

        q = q_ref[:, cols]
        lane = lax.broadcasted_iota(jnp.int32, q.shape, 1)
        zero = jnp.zeros_like(q)
        qs = [jnp.where(lane < C_HEAD_DIM, q, zero), jnp.where(lane >= C_HEAD_DIM, q, zero)]
        oe1, oe2 = _attend(qs, k_ref.at[:, cols], vext_ref, (s0_ref, s1_ref), (p0_ref, p1_ref))
        o = oe1[:, :hw] / oe1[:, hw:] - lam * (oe2[:, :hw] / oe2[:, hw:])
        o_ref[:, cols] = (_rms(o, g_ref[...]) * (1.0 - lam_init)).astype(o_ref.dtype)


def _diff_attn(qk, v, lamvec, subln_g, out, *, lam_init, rpb, n_lat, batch, ctx_queries):
    t = qk.shape[0]
    heads = v.shape[1] // (2 * C_HEAD_DIM)
    hw = 2 * C_HEAD_DIM
    tq, nq, lk, q_blk, k_blk = _attn_geometry(rpb, n_lat, ctx_queries)
    hs = math.gcd(heads, CTX_HEADS_PER_STEP) if ctx_queries else 1
    assert nq == 1 or hs == 1
    gw = hs * hw
    per_sample = lambda a: a.reshape(batch, rpb, a.shape[1])
    kernel = functools.partial(_diff_kernel, lam_init=lam_init, heads_per_step=hs)
    return pl.pallas_call(
        kernel,
        grid=(batch, heads // hs, nq),
        in_specs=[
            pl.BlockSpec(lamvec.shape, lambda b, h, i: (0, 0)),
            pl.BlockSpec((1, hw), lambda b, h, i: (0, 0)),
            pl.BlockSpec((None, tq, gw), lambda b, h, i: (b, q_blk(i), h)),
            pl.BlockSpec((None, lk, gw), lambda b, h, i: (b, k_blk, heads // hs + h)),
            pl.BlockSpec((None, lk, gw), lambda b, h, i: (b, k_blk, h)),
            pl.BlockSpec(memory_space=pl.ANY),
        ],
        out_specs=pl.BlockSpec((None, tq, gw), lambda b, h, i: (b, q_blk(i), h)),
        out_shape=jax.ShapeDtypeStruct((batch, rpb, v.shape[1]), BF16),
        scratch_shapes=_attn_scratch(tq, lk, hw),
        input_output_aliases={5: 0},
        compiler_params=_params(),
        name="diff_ctx" if ctx_queries else "diff_lat",
    )(lamvec, subln_g.reshape(1, hw).astype(F32), per_sample(qk), per_sample(qk), per_sample(v),
      per_sample(out)).reshape(t, v.shape[1])


CONV_PAD = SUBLANE
SCAN_UNROLL = 8


def _sigmoid(x):
    return 0.5 * jnp.tanh(0.5 * x) + 0.5


def _block_scan(a, u, carry, reverse):
    row = lax.broadcasted_iota(jnp.int32, a.shape, 0)
    for s in (1, 2, 4):
        shift = SUBLANE - s if reverse else s
        valid = (row < SUBLANE - s) if reverse else (row >= s)
        a_prev = pltpu.roll(a, shift, 0)
        u_prev = pltpu.roll(u, shift, 0)
        u = jnp.where(valid, a * u_prev + u, u)
        a = jnp.where(valid, a * a_prev, a)
    h = u + a * carry
    last = h[0:1] if reverse else h[SUBLANE - 1:SUBLANE]
    return h, jnp.broadcast_to(last, h.shape)


def _rglru_kernel(xb_ref, yb_ref, cw_ref, cb_ref, gw_ref, gb_ref, lam_ref, buf_ref, o_ref,
                  xpad_ref, xc_ref, a_ref, u_ref, hf_ref, hb_ref, *, n_lat, n_ctx):
    del buf_ref
    cw = cw_ref[...]
    taps = cw.shape[0]
    zpad = jnp.zeros((CONV_PAD, B_BLOCK), F32)

    def conv_segment(r0, n):
        xpad_ref[0:CONV_PAD] = zpad
        xpad_ref[CONV_PAD:CONV_PAD + n] = xb_ref[r0:r0 + n]
        xpad_ref[CONV_PAD + n:2 * CONV_PAD + n] = zpad
        acc = jnp.broadcast_to(cb_ref[...], (n, B_BLOCK))
        for j in range(taps):
            off = CONV_PAD + j - taps // 2
            acc = acc + cw[j:j + 1] * xpad_ref[off:off + n]
        xc_ref[r0:r0 + n] = acc

    conv_segment(0, n_lat)
    conv_segment(n_lat, n_ctx)

    xc = xc_ref[...]
    gates = _dot(xc.astype(BF16), gw_ref[0]) + gb_ref[0]
    lam = lam_ref[...]
    softplus_neg = jnp.maximum(-lam, 0.0) + jnp.log(1.0 + jnp.exp(-jnp.abs(lam)))
    log2_base = (-LRU_C * LOG2E) * softplus_neg
    for d in range(2):
        r = _sigmoid(gates[:, (2 * d) * B_BLOCK:(2 * d + 1) * B_BLOCK])
        i = _sigmoid(gates[:, (2 * d + 1) * B_BLOCK:(2 * d + 2) * B_BLOCK])
        a = jnp.exp2(r * log2_base[d:d + 1])
        a_ref[d] = a
        u_ref[d] = jnp.sqrt(1.0 - a * a) * i * xc

    def scan_segment(r0, n, carry):
        nblk = n // SUBLANE

        def body(b, c):
            cf, cb = c
            rf = pl.multiple_of(r0 + b * SUBLANE, SUBLANE)
            rb = pl.multiple_of(r0 + (nblk - 1 - b) * SUBLANE, SUBLANE)
            hf, cf = _block_scan(a_ref[0, pl.ds(rf, SUBLANE), :], u_ref[0, pl.ds(rf, SUBLANE), :], cf, False)
            hb, cb = _block_scan(a_ref[1, pl.ds(rb, SUBLANE), :], u_ref[1, pl.ds(rb, SUBLANE), :], cb, True)
            hf_ref[pl.ds(rf, SUBLANE), :] = hf
            hb_ref[pl.ds(rb, SUBLANE), :] = hb
            return cf, cb

        return lax.fori_loop(0, nblk, body, carry, unroll=math.gcd(nblk, SCAN_UNROLL))

    zero = jnp.zeros((SUBLANE, B_BLOCK), F32)
    carry = scan_segment(n_lat, n_ctx, (zero, zero))
    scan_segment(0, n_lat, carry)
    o_ref[...] = ((hf_ref[...] + hb_ref[...]) * _gelu(yb_ref[...])).astype(o_ref.dtype)


def _rglru(xb, yb, conv_w, conv_b, gw, gb, lam, out, col0, *, rpb, n_lat, batch):
    t, width = xb.shape
    assert col0 % B_BLOCK == 0
    nh = width // B_BLOCK
    n_ctx = rpb - n_lat
    taps = conv_w.shape[0]
    return pl.pallas_call(
        functools.partial(_rglru_kernel, n_lat=n_lat, n_ctx=n_ctx),
        grid=(batch, nh),
        in_specs=[
            pl.BlockSpec((rpb, B_BLOCK), lambda b, h: (b, h)),
            pl.BlockSpec((rpb, B_BLOCK), lambda b, h: (b, h)),
            pl.BlockSpec((taps, B_BLOCK), lambda b, h: (0, h)),
            pl.BlockSpec((1, B_BLOCK), lambda b, h: (0, h)),
            pl.BlockSpec((1, B_BLOCK, 4 * B_BLOCK), lambda b, h: (h, 0, 0)),
            pl.BlockSpec((1, 1, 4 * B_BLOCK), lambda b, h: (h, 0, 0)),
            pl.BlockSpec((2, B_BLOCK), lambda b, h: (0, h)),
            pl.BlockSpec(memory_space=pl.ANY),
        ],
        out_specs=pl.BlockSpec((rpb, B_BLOCK), lambda b, h: (b, col0 // B_BLOCK + h)),
        out_shape=jax.ShapeDtypeStruct(out.shape, BF16),
        input_output_aliases={7: 0},
        scratch_shapes=[
            pltpu.VMEM((n_lat + 2 * CONV_PAD, B_BLOCK), F32),
            pltpu.VMEM((rpb, B_BLOCK), F32),
            pltpu.VMEM((2, rpb, B_BLOCK), F32),
            pltpu.VMEM((2, rpb, B_BLOCK), F32),
            pltpu.VMEM((rpb, B_BLOCK), F32),
            pltpu.VMEM((rpb, B_BLOCK), F32),
        ],
        compiler_params=_params(),
        name="rglru",
    )(xb, yb, conv_w, conv_b.reshape(1, width), gw, gb, lam, out)


CAST_ROWS = 256


def _expert_weight_copies(wgu_hbm, wd_hbm, wgu_f32, wd_f32, sem, layer, expert, slot):
    return (pltpu.make_async_copy(wgu_hbm.at[layer, expert], wgu_f32.at[slot], sem.at[slot, 0]),
            pltpu.make_async_copy(wd_hbm.at[layer, expert], wd_f32.at[slot], sem.at[slot, 1]))


def _row_tile_copy(x_hbm, xbuf, xsem, tile, tm):
    slot = lax.rem(tile, ROW_RING)
    return pltpu.make_async_copy(x_hbm.at[pl.ds(pl.multiple_of(tile * tm, tm), tm)], xbuf.at[slot], xsem.at[slot])


def _expert_kernel(te_ref, nx_ref, sl_ref, nu_ref, x_hbm, wgu_hbm, bgu_ref, wd_hbm, bd_ref, o_ref,
                   wgu_f32, wd_f32, wgu_bf, wd_bf, sem, xbuf, xsem, *, f, layer):
    j = pl.program_id(0)
    expert = te_ref[j]
    slot = sl_ref[j]
    n_used = nu_ref[0]
    tm = xbuf.shape[1]
    copies = functools.partial(_expert_weight_copies, wgu_hbm, wd_hbm, wgu_f32, wd_f32, sem, layer)
    row_tile = functools.partial(_row_tile_copy, x_hbm, xbuf, xsem, tm=tm)

    @pl.when(j == 0)
    def _():
        for ahead in range(ROW_RING - 1):
            @pl.when(ahead < n_used)
            def _():
                row_tile(jnp.int32(ahead)).start()

    @pl.when(j + (ROW_RING - 1) < n_used)
    def _():
        row_tile(j + (ROW_RING - 1)).start()

    @pl.when(j == 0)
    def _():
        for c in copies(expert, slot):
            c.start()

    @pl.when((j == 0) | (expert != te_ref[jnp.maximum(j - 1, 0)]))
    def _():
        for c in copies(expert, slot):
            c.wait()
        nxt = nx_ref[j]

        @pl.when(nxt >= 0)
        def _():
            for c in copies(nxt, 1 - slot):
                c.start()

        for r in range(0, wgu_bf.shape[0], CAST_ROWS):
            wgu_bf[r:r + CAST_ROWS] = wgu_f32[slot, r:r + CAST_ROWS].astype(BF16)
        for r in range(0, wd_bf.shape[0], CAST_ROWS):
            wd_bf[r:r + CAST_ROWS] = wd_f32[slot, r:r + CAST_ROWS].astype(BF16)

    @pl.when(j < n_used)
    def _():
        row_tile(j).wait()
        gu = _dot(xbuf[lax.rem(j, ROW_RING)], wgu_bf[...]) + bgu_ref[0, 0]
        gate = jnp.minimum(gu[:, :f], SWIGLU_LIMIT)
        up = jnp.clip(gu[:, f:], -SWIGLU_LIMIT, SWIGLU_LIMIT)
        act = (up + 1.0) * gate * jax.nn.sigmoid(gate * SWIGLU_ALPHA)
        o_ref[...] = (_dot(act.astype(BF16), wd_bf[...]) + bd_ref[0, 0]).astype(o_ref.dtype)

    @pl.when(j >= n_used)
    def _():
        o_ref[...] = jnp.zeros(o_ref.shape, o_ref.dtype)


def _experts(x_sorted, tile_expert, n_used, w_gu, b_gu, w_d, b_d, *, layer, tm):
    r, d = x_sorted.shape
    depth, e, _, f2 = w_gu.shape
    f = f2 // 2
    assert d % CAST_ROWS == 0 and f % CAST_ROWS == 0
    n_tiles = tile_expert.shape[0]
    first = jnp.concatenate([jnp.ones((1,), jnp.bool_), tile_expert[1:] != tile_expert[:-1]])
    slot = ((jnp.cumsum(first.astype(jnp.int32)) - 1) % 2).astype(jnp.int32)
    later = jnp.where(tile_expert[None, :] > tile_expert[:, None], tile_expert[None, :], e)
    next_expert = jnp.min(later, axis=1)
    next_expert = jnp.where(next_expert < e, next_expert, -1).astype(jnp.int32)
    grid_spec = pltpu.PrefetchScalarGridSpec(
        num_scalar_prefetch=4,
        grid=(n_tiles,),
        in_specs=[
            pl.BlockSpec(memory_space=pl.ANY),
            pl.BlockSpec(memory_space=pl.ANY),
            pl.BlockSpec((1, 1, 1, f2), lambda j, te, *_: (layer, te[j], 0, 0)),
            pl.BlockSpec(memory_space=pl.ANY),
            pl.BlockSpec((1, 1, 1, d), lambda j, te, *_: (layer, te[j], 0, 0)),
        ],
        out_specs=pl.BlockSpec((tm, d), lambda j, *_: (j, 0)),
        scratch_shapes=[pltpu.VMEM((2, d, f2), F32), pltpu.VMEM((2, f, d), F32),
                        pltpu.VMEM((d, f2), BF16), pltpu.VMEM((f, d), BF16),
                        pltpu.SemaphoreType.DMA((2, 2)),
                        pltpu.VMEM((ROW_RING, tm, d), BF16), pltpu.SemaphoreType.DMA((ROW_RING,))],
    )
    assert r == n_tiles * tm
    return pl.pallas_call(
        functools.partial(_expert_kernel, f=f, layer=layer),
        grid_spec=grid_spec,
        out_shape=jax.ShapeDtypeStruct((r, d), BF16),
        compiler_params=_params(),
        name="experts",
    )(tile_expert, next_expert, slot, n_used, x_sorted, w_gu, b_gu.reshape(depth, e, 1, f2), w_d,
      b_d.reshape(depth, e, 1, d))


def _combine_kernel(y_ref, wt_ref, x_ref, modb_ref, modc_ref, *rest, k, bm, rpb, n_lat, tile0, with_next):
    is_ctx = _ctx_rows(pl.program_id(0) + tile0, bm, rpb, n_lat)
    wt = wt_ref[...]
    y = wt[:, 0:1] * y_ref[0].astype(F32)
    for j in range(1, TOP_K):
        y = y + wt[:, j:j + 1] * y_ref[j].astype(F32)
    x_new = x_ref[...] + _mod_row(modb_ref, modc_ref, k, is_ctx) * y
    if not with_next:
        (o_ref,) = rest
        o_ref[...] = x_new
        return
    g_ref, nmodb_ref, nmodc_ref, _, o_ref, h_ref = rest
    o_ref[...] = x_new
    h = _rms(x_new, g_ref[...]) * (1.0 + _mod_row(nmodb_ref, nmodc_ref, 1, is_ctx)) + _mod_row(nmodb_ref, nmodc_ref, 0, is_ctx)
    h_ref[...] = h.astype(BF16)


def _combine(y4, wt, xs, mod3, *, row0, k, rpb, n_lat, batch, nxt=None):
    t, d = xs.shape
    rows = y4.shape[1]
    bm = _tile(rpb, 256)
    assert row0 % bm == 0 and rows % bm == 0
    tile0 = row0 // bm
    row_spec = lambda width: pl.BlockSpec((bm, width), lambda i: (i + tile0, 0))
    mod_specs = [pl.BlockSpec((1, 6, d), lambda i: (((i + tile0) * bm) // rpb, 0, 0)),
                 pl.BlockSpec((1, 6, d), lambda i: (batch, 0, 0))]
    in_specs = [pl.BlockSpec((TOP_K, bm, d), lambda i: (0, i, 0)), row_spec(LANE), row_spec(d)] + mod_specs
    args = [y4, wt, xs, mod3, mod3]
    out_specs, out_shape, aliases = [row_spec(d)], [jax.ShapeDtypeStruct((t, d), F32)], {2: 0}
    if nxt is not None:
        g_next, mod3_next, h = nxt
        in_specs += [pl.BlockSpec((1, d), lambda i: (0, 0))] + mod_specs + [pl.BlockSpec(memory_space=pl.ANY)]
        args += [g_next.reshape(1, d), mod3_next, mod3_next, h]
        out_specs.append(row_spec(d))
        out_shape.append(jax.ShapeDtypeStruct((t, d), BF16))
        aliases[len(args) - 1] = 1
    out = pl.pallas_call(
        functools.partial(_combine_kernel, k=k, bm=bm, rpb=rpb, n_lat=n_lat, tile0=tile0, with_next=nxt is not None),
        grid=(rows // bm,),
        in_specs=in_specs,
        out_specs=out_specs,
        out_shape=out_shape,
        input_output_aliases=aliases,
        compiler_params=_params(),
        name="combine",
    )(*args)
    return (out[0], out[1]) if nxt is not None else (out[0], None)


def _dispatch_plan(top_idx, n_experts, tm):
    t = top_idx.shape[0]
    chosen = (top_idx[:, :, None] == jnp.arange(n_experts, dtype=jnp.int32)[None, None, :]).any(axis=1)
    chosen = chosen.astype(jnp.int32)
    before = jnp.cumsum(chosen, axis=0) - chosen
    counts = before[-1] + chosen[-1]
    padded = ((counts + tm - 1) // tm) * tm
    group_end = jnp.cumsum(padded)
    group_start = group_end - padded
    dest = group_start[top_idx] + jnp.take_along_axis(before, top_idx, axis=1)
    n_rows = TOP_K * t + n_experts * tm
    n_tiles = n_rows // tm
    tile_expert = jnp.minimum(
        jnp.sum(jnp.arange(n_tiles, dtype=jnp.int32)[:, None] >= (group_end // tm)[None, :], axis=1), n_experts - 1
    ).astype(jnp.int32)
    n_used = (group_end[-1] // tm).astype(jnp.int32).reshape(1)
    pad = jnp.arange(tm, dtype=jnp.int32)[None, :]
    pad_row = (group_start + counts)[:, None] + pad
    pad_key = jnp.where(pad_row < group_end[:, None], pad_row, n_rows + pad_row).reshape(-1)
    keys = jnp.concatenate([dest.reshape(-1), pad_key])
    vals = jnp.concatenate([jnp.repeat(jnp.arange(t, dtype=jnp.int32), TOP_K),
                            jnp.arange(n_experts * tm, dtype=jnp.int32) % t])
    _, row_token = lax.sort((keys, vals), num_keys=1)
    return dest, row_token, tile_expert, n_used


def _moe(xs, g, mod3, router_w, router_b, w_gu, b_gu, w_d, b_d, *, layer, rpb, n_lat, batch, nxt=None):
    t, d = xs.shape
    e = router_w.shape[1]
    splits = MOE_TOKEN_SPLITS if batch % MOE_TOKEN_SPLITS == 0 else 1
    ts = t // splits
    h2, idx, wt = _modulate_route(xs, g, mod3, router_w, router_b, k=3, rpb=rpb, n_lat=n_lat, batch=batch)
    h = h2
    plans = [_dispatch_plan(idx[s * ts:(s + 1) * ts, :TOP_K], e, MOE_ROW_TILE) for s in range(splits)]
    sorted_x = [jnp.take(h2, row_token + s * ts, axis=0, mode="clip") for s, (_, row_token, _, _) in enumerate(plans)]
    for s, (dest, _, tile_expert, n_used) in enumerate(plans):
        rows = _experts(sorted_x[s], tile_expert, n_used, w_gu, b_gu, w_d, b_d, layer=layer, tm=MOE_ROW_TILE)
        y4 = jnp.take(rows, dest.T.reshape(-1), axis=0, mode="clip").reshape(TOP_K, ts, d)
        xs, h = _combine(y4, wt, xs, mod3, row0=s * ts, k=5, rpb=rpb, n_lat=n_lat, batch=batch,
                         nxt=None if nxt is None else (*nxt, h))
    return xs, h


def _final_kernel(x_ref, g_ref, o_ref):
    o_ref[0] = _rms(x_ref[0], g_ref[...])


def _final_norm(xs3, g, n_lat):
    batch, rpb, d = xs3.shape
    bm = _tile(n_lat, 256)
    return pl.pallas_call(
        _final_kernel,
        grid=(batch, n_lat // bm),
        in_specs=[pl.BlockSpec((1, bm, d), lambda b, i: (b, i, 0)), pl.BlockSpec((1, d), lambda b, i: (0, 0))],
        out_specs=pl.BlockSpec((1, bm, d), lambda b, i: (b, i, 0)),
        out_shape=jax.ShapeDtypeStruct((batch, n_lat, d), F32),
        compiler_params=_params(),
        name="final_norm",
    )(xs3, g.reshape(1, d))


def _rope_tables(n_lat, n_ctx, head_dim):
    n_freq = head_dim // 4
    inv_freq = ROPE_THETA ** (-jnp.arange(n_freq, dtype=F32) / n_freq)
    tok = jnp.arange(n_lat, dtype=jnp.int32)
    ang_r = (tok // GRID_W).astype(F32)[:, None] * inv_freq
    ang_c = (tok % GRID_W).astype(F32)[:, None] * inv_freq
    cos = jnp.concatenate([jnp.cos(ang_r), jnp.cos(ang_r), jnp.cos(ang_c), jnp.cos(ang_c)], axis=-1)
    sin = jnp.concatenate([-jnp.sin(ang_r), jnp.sin(ang_r), -jnp.sin(ang_c), jnp.sin(ang_c)], axis=-1)
    reps = LANE // head_dim
    cos = jnp.concatenate([jnp.tile(cos, (1, reps)), jnp.ones((n_ctx, LANE), F32)], axis=0)
    sin = jnp.concatenate([jnp.tile(sin, (1, reps)), jnp.zeros((n_ctx, LANE), F32)], axis=0)
    return cos, sin, n_freq


def kernel(x, c, ctx, c_ctx, mod_w, mod_b, norm1_g, norm2_g, hyb_w_in, hyb_w_out, a_q_norm_g, a_k_norm_g, b_conv_w, b_conv_b, b_gate_a_w, b_gate_a_b, b_gate_x_w, b_gate_x_b, b_lru_lambda, c_w_in, c_w_out, c_lambda_q1, c_lambda_k1, c_lambda_q2, c_lambda_k2, c_subln_g, router_w, router_b, exp_w_gu, exp_b_gu, exp_w_d, exp_b_d, final_g):
    batch, n_lat, d = x.shape
    n_ctx = ctx.shape[1]
    rpb = n_lat + n_ctx
    t = batch * rpb
    depth = mod_w.shape[0]
    geo = dict(rpb=rpb, n_lat=n_lat, batch=batch)

    xs = jnp.concatenate([x, ctx], axis=1).reshape(t, d)
    mod_rows = -(-(batch + 1) // SUBLANE) * SUBLANE
    cvec = jnp.zeros((mod_rows, d), F32).at[:batch].set(c).at[batch].set(c_ctx)
    mod = _modmat(cvec, mod_w, mod_b)

    rope_a = _rope_tables(n_lat, n_ctx, A_HEAD_DIM)
    rope_c = _rope_tables(n_lat, n_ctx, C_HEAD_DIM)
    q_w = A_GROUP * A_HEAD_DIM * (hyb_w_in.shape[2] - 2 * (d // 2)) // ((A_GROUP + 2) * A_HEAD_DIM)
    kv_w = q_w // A_GROUP
    b_w = d // 2
    nh = b_w // B_BLOCK

    mod3s = [mod[l].reshape(mod_rows, 6, d) for l in range(depth)]
    h = _modulate(xs, norm1_g[0], mod3s[0], k=0, **geo)
    for l in range(depth):
        i = l // 2
        mod3 = mod3s[l]
        if l % 2 == 0:
            w_in = hyb_w_in[i].astype(BF16)
            o = 0
            q = _proj(h, w_in[:, o:o + q_w], rpb=rpb, out_dtype=BF16, norm_g=a_q_norm_g[i], rope=rope_a,
                      q_cols=q_w, q_scale=A_HEAD_DIM ** -0.5 * LOG2E)
            o += q_w
            kk = _proj(h, w_in[:, o:o + kv_w], rpb=rpb, out_dtype=BF16, norm_g=a_k_norm_g[i], rope=rope_a)
            o += kv_w
            v = _proj(h, w_in[:, o:o + kv_w], rpb=rpb, out_dtype=BF16)
            o += kv_w
            xb = _proj(h, w_in[:, o:o + b_w], rpb=rpb, out_dtype=F32)
            o += b_w
            yb = _proj(h, w_in[:, o:o + b_w], rpb=rpb, out_dtype=F32)
            mix = _gqa(q, kk, v, h, ctx_queries=False, **geo)
            mix = _gqa(q, kk, v, mix, ctx_queries=True, **geo)
            gw = jnp.concatenate([b_gate_a_w[i, 0], b_gate_x_w[i, 0], b_gate_a_w[i, 1], b_gate_x_w[i, 1]],
                                 axis=-1).astype(BF16)
            gb = jnp.concatenate([b_gate_a_b[i, 0].reshape(nh, 1, B_BLOCK), b_gate_x_b[i, 0].reshape(nh, 1, B_BLOCK),
                                  b_gate_a_b[i, 1].reshape(nh, 1, B_BLOCK), b_gate_x_b[i, 1].reshape(nh, 1, B_BLOCK)],
                                 axis=-1)
            mix = _rglru(xb, yb, b_conv_w[i], b_conv_b[i], gw, gb, b_lru_lambda[i], mix, q_w, **geo)
            pairs = [(mix, hyb_w_out[i].astype(BF16))]
        else:
            lam_init = 0.8 - 0.6 * math.exp(-0.3 * l)
            w_in = c_w_in[i].astype(BF16)
            qk_w = 2 * (w_in.shape[1] // 3)
            qk = _proj(h, w_in[:, :qk_w], rpb=rpb, out_dtype=BF16, rope=rope_c,
                       q_cols=qk_w // 2, q_scale=C_HEAD_DIM ** -0.5 * LOG2E)
            v = _proj(h, w_in[:, qk_w:], rpb=rpb, out_dtype=BF16)
            lamvec = jnp.stack([c_lambda_q1[i], c_lambda_k1[i], c_lambda_q2[i], c_lambda_k2[i]]).astype(F32)
            att = _diff_attn(qk, v, lamvec, c_subln_g[i], h, lam_init=lam_init, ctx_queries=False, **geo)
            att = _diff_attn(qk, v, lamvec, c_subln_g[i], att, lam_init=lam_init, ctx_queries=True, **geo)
            pairs = [(att, c_w_out[i].astype(BF16))]
        xs = _resid_proj(xs, mod3, pairs, k=2, **geo)
        xs, h = _moe(xs, norm2_g[l], mod3, router_w[l], router_b[l], exp_w_gu, exp_b_gu, exp_w_d, exp_b_d,
                     layer=l, nxt=(norm1_g[l + 1], mod3s[l + 1]) if l + 1 < depth else None, **geo)
    return _final_norm(xs.reshape(batch, rpb, d), final_g, n_lat)
```
